```python
import jax, jax.numpy as jnp
from jax import lax
import numpy as np

D_MODEL = 2048
BATCH = 2
SEQ = 4096
DEPTH = 2
DEC_BATCH = 32
DEC_SEQ = 1
PAST_LEN = 8192
PAGE_SIZE = 128

N_HEADS = 16
HEAD_DIM = D_MODEL // N_HEADS
KV_HEADS = 4
Q_PER_KV = N_HEADS // KV_HEADS
IDX_HEADS = 16
IDX_DIM = 64
TOPK_MAX = 256
ROPE_THETA = 10000.0
Q_BLOCK = 128
CONV_WIDTH = 31
CONV_CH = D_MODEL
PEER_HEADS = 8
PEER_NKEYS = 128
PEER_EXPERTS = PEER_NKEYS * PEER_NKEYS
PEER_QDIM = 256
PEER_HALF = PEER_QDIM // 2
PEER_TOPK = 16
PEER_BLOCK = 128
N_ATTN_LAYERS = (DEPTH + 1) // 2
N_CONV_LAYERS = DEPTH // 2
RMS_EPS = 1e-6
LN_EPS = 1e-5
NEG = -1e30
Q_COLS = N_HEADS * HEAD_DIM
KV_COLS = KV_HEADS * HEAD_DIM
IQ_COLS = IDX_HEADS * IDX_DIM
PROJ_COLS = Q_COLS + 2 * KV_COLS + IQ_COLS + IDX_DIM + IDX_HEADS

kernel_name = "dsa_conformer_peer_adaln_decode_step"


def rms_norm(x, g):
    xf = x.astype(jnp.float32)
    y = xf * lax.rsqrt(jnp.mean(xf * xf, axis=-1, keepdims=True) + RMS_EPS)
    return (y * g.astype(jnp.float32)).astype(x.dtype)


def layer_norm(x, g, b):
    xf = x.astype(jnp.float32)
    mu = jnp.mean(xf, axis=-1, keepdims=True)
    var = jnp.mean(jnp.square(xf - mu), axis=-1, keepdims=True)
    y = (xf - mu) * lax.rsqrt(var + LN_EPS)
    return (y * g.astype(jnp.float32) + b.astype(jnp.float32)).astype(x.dtype)


def rope(x, pos):
    half = x.shape[-1] // 2
    inv = ROPE_THETA ** (-jnp.arange(half, dtype=jnp.float32) / half)
    ang = pos.astype(jnp.float32)[:, None] * inv
    cos = jnp.cos(ang)[:, None, :]
    sin = jnp.sin(ang)[:, None, :]
    xf = x.astype(jnp.float32)
    x1, x2 = xf[..., :half], xf[..., half:]
    out = jnp.concatenate([x1 * cos - x2 * sin, x2 * cos + x1 * sin], axis=-1)
    return out.astype(x.dtype)


def ada_modulation(c, w_ada, b_ada):
    mod = jax.nn.silu(c) @ w_ada + b_ada
    return jnp.split(mod[:, None, :], 6, axis=-1)


def modulate(h, shift, scale):
    return h * (1 + scale) + shift


def attn_project(h, pos, w_in, q_gain, k_gain):
    B, T, _ = h.shape
    proj = h @ w_in
    cuts = [Q_COLS, Q_COLS + KV_COLS, Q_COLS + 2 * KV_COLS,
            Q_COLS + 2 * KV_COLS + IQ_COLS, Q_COLS + 2 * KV_COLS + IQ_COLS + IDX_DIM]
    q, k, v, iq, ik, iw = jnp.split(proj, cuts, axis=-1)
    q = rope(rms_norm(q.reshape(B, T, N_HEADS, HEAD_DIM), q_gain), pos)
    k = rope(rms_norm(k.reshape(B, T, KV_HEADS, HEAD_DIM), k_gain), pos)
    v = v.reshape(B, T, KV_HEADS, HEAD_DIM)
    iq = rope(iq.reshape(B, T, IDX_HEADS, IDX_DIM), pos)
    ik = rope(ik.reshape(B, T, 1, IDX_DIM), pos)[:, :, 0]
    iw = iw * (IDX_HEADS ** -0.5)
    return q, k, v, iq, ik, iw


def indexer_scores(iq, iw, ik):
    s = jnp.einsum('bqhd,bsd->bqhs', iq.astype(jnp.float32), ik.astype(jnp.float32))
    return jnp.einsum('bqhs,bqh->bqs', jax.nn.relu(s), iw.astype(jnp.float32))


def sparse_attend(q, k_sel, v_sel, sel_pos, q_pos):
    B, Tq = q.shape[:2]
    qg = q.reshape(B, Tq, KV_HEADS, Q_PER_KV, HEAD_DIM).astype(jnp.float32)
    s = jnp.einsum('bqgrd,bqkgd->bqgrk', qg, k_sel.astype(jnp.float32)) * (HEAD_DIM ** -0.5)
    ok = sel_pos <= q_pos[None, :, None]
    s = jnp.where(ok[:, :, None, None, :], s, -jnp.inf)
    p = jax.nn.softmax(s, axis=-1)
    o = jnp.einsum('bqgrk,bqkgd->bqgrd', p, v_sel.astype(jnp.float32))
    return o.reshape(B, Tq, N_HEADS * HEAD_DIM).astype(q.dtype)


def gather_rows(rows, idx):
    return jax.vmap(lambda rb, ib: rb[ib])(rows, idx)


def attention_prompt(h, w_in, q_gain, k_gain, w_out):
    B, T, _ = h.shape
    pos = jnp.arange(T, dtype=jnp.int32)
    q, k, v, iq, ik, iw = attn_project(h, pos, w_in, q_gain, k_gain)
    topk = min(TOPK_MAX, T // 4)
    n_blocks = T // Q_BLOCK

    def block(i):
        start = i * Q_BLOCK
        qb = lax.dynamic_slice_in_dim(q, start, Q_BLOCK, axis=1)
        iqb = lax.dynamic_slice_in_dim(iq, start, Q_BLOCK, axis=1)
        iwb = lax.dynamic_slice_in_dim(iw, start, Q_BLOCK, axis=1)
        qpos = start + jnp.arange(Q_BLOCK, dtype=jnp.int32)
        sc = indexer_scores(iqb, iwb, ik)
        sc = jnp.where(pos[None, None, :] <= qpos[None, :, None], sc, NEG)
        _, idx = lax.top_k(sc, topk)
        return sparse_attend(qb, gather_rows(k, idx), gather_rows(v, idx), idx, qpos)

    o = lax.map(block, jnp.arange(n_blocks, dtype=jnp.int32))
    o = o.transpose(1, 0, 2, 3).reshape(B, T, N_HEADS * HEAD_DIM)
    return o @ w_out, k, v, ik


def attention_sample(h, cache_k, cache_v, cache_ik, page_table, w_in, q_gain, k_gain, w_out):
    Bd, T, _ = h.shape
    past = page_table.shape[1] * PAGE_SIZE
    pos = past + jnp.arange(T, dtype=jnp.int32)
    q, k, v, iq, ik, iw = attn_project(h, pos, w_in, q_gain, k_gain)
    L = past + T
    topk = min(TOPK_MAX, L // 4)
    ik_past = cache_ik[page_table].reshape(Bd, past, IDX_DIM)
    ik_all = jnp.concatenate([ik_past.astype(ik.dtype), ik], axis=1)
    key_pos = jnp.arange(L, dtype=jnp.int32)
    sc = indexer_scores(iq, iw, ik_all)
    sc = jnp.where(key_pos[None, None, :] <= pos[None, :, None], sc, NEG)
    _, idx = lax.top_k(sc, topk)
    in_past = idx < past
    pc = jnp.minimum(idx, past - 1)
    phys = jnp.take_along_axis(page_table, (pc // PAGE_SIZE).reshape(Bd, -1), axis=1).reshape(idx.shape)
    off = pc % PAGE_SIZE
    nc = jnp.clip(idx - past, 0, T - 1)
    sel = in_past[..., None, None]
    k_sel = jnp.where(sel, cache_k[phys, off].astype(k.dtype), gather_rows(k, nc))
    v_sel = jnp.where(sel, cache_v[phys, off].astype(v.dtype), gather_rows(v, nc))
    o = sparse_attend(q, k_sel, v_sel, idx, pos)
    return o @ w_out, k, v, ik


def conformer_conv(h, hist, w_pw1, b_pw1, w_dw, b_dw, ln_g, ln_b, w_pw2):
    a, g = jnp.split(h @ w_pw1 + b_pw1, 2, axis=-1)
    u = a * jax.nn.sigmoid(g)
    ext = jnp.concatenate([hist.astype(u.dtype), u], axis=1)
    y = lax.conv_general_dilated(ext, w_dw[:, None, :].astype(u.dtype), (1,), 'VALID',
                                 dimension_numbers=('NWC', 'WIO', 'NWC'),
                                 feature_group_count=CONV_CH) + b_dw
    y = jax.nn.silu(layer_norm(y, ln_g, ln_b))
    return y @ w_pw2, ext[:, -(CONV_WIDTH - 1):]


def peer_tokens(x, w_q, sub_keys, u_tab, v_tab):
    N = x.shape[0]
    q = (x @ w_q).reshape(N, PEER_HEADS, 2, PEER_HALF).astype(jnp.float32)
    s = jnp.einsum('nhpd,hpkd->nhpk', q, sub_keys.astype(jnp.float32))
    sv, si = lax.top_k(s, PEER_TOPK)
    cand = (sv[:, :, 0, :, None] + sv[:, :, 1, None, :]).reshape(N, PEER_HEADS, -1)
    cand_idx = (si[:, :, 0, :, None] * PEER_NKEYS + si[:, :, 1, None, :]).reshape(N, PEER_HEADS, -1)
    cv, ci = lax.top_k(cand, PEER_TOPK)
    experts = jnp.take_along_axis(cand_idx, ci, axis=-1).reshape(N, -1)
    gates = jax.nn.softmax(cv, axis=-1).reshape(N, -1)
    hid = jax.nn.gelu(jnp.einsum('nkd,nd->nk', u_tab[experts], x).astype(jnp.float32), approximate=False)
    w = (gates * hid).astype(x.dtype)
    return jnp.einsum('nk,nkd->nd', w, v_tab[experts])


def peer(h, w_q, sub_keys, u_tab, v_tab):
    B, T, D = h.shape
    n = B * T
    blk = min(PEER_BLOCK, n)
    nb = -(-n // blk)
    xf = jnp.pad(h.reshape(n, D), ((0, nb * blk - n), (0, 0)))
    out = lax.map(lambda xb: peer_tokens(xb, w_q, sub_keys, u_tab, v_tab), xf.reshape(nb, blk, D))
    return out.reshape(nb * blk, D)[:n].reshape(B, T, D)


def setup_inputs(seed: int = 0) -> dict:
    key = jax.random.key(seed)
    ks = jax.random.split(key, 32)
    D = D_MODEL
    n_pages = PAST_LEN // PAGE_SIZE
    n_phys = (DEC_BATCH * n_pages * 5) // 4
    nrm = jax.random.normal
    f32 = jnp.float32
    page_table = jax.random.permutation(ks[0], n_phys)[:DEC_BATCH * n_pages].reshape(DEC_BATCH, n_pages).astype(jnp.int32)
    return {
        "x_prompt": nrm(ks[1], (BATCH, SEQ, D), f32),
        "x_sample": nrm(ks[2], (DEC_BATCH, DEC_SEQ, D), f32),
        "cache_k": nrm(ks[3], (N_ATTN_LAYERS, n_phys, PAGE_SIZE, KV_HEADS, HEAD_DIM), f32),
        "cache_v": nrm(ks[4], (N_ATTN_LAYERS, n_phys, PAGE_SIZE, KV_HEADS, HEAD_DIM), f32),
        "cache_idx_k": nrm(ks[5], (N_ATTN_LAYERS, n_phys, PAGE_SIZE, IDX_DIM), f32),
        "state_conv": 0.5 * nrm(ks[6], (N_CONV_LAYERS, DEC_BATCH, CONV_WIDTH - 1, CONV_CH), f32),
        "page_table": page_table,
        "c_prompt": nrm(ks[7], (BATCH, D), f32),
        "c_sample": nrm(ks[8], (DEC_BATCH, D), f32),
        "w_ada": 0.5 * D ** -0.5 * nrm(ks[9], (DEPTH, D, 6 * D), f32),
        "b_ada": 0.01 * nrm(ks[10], (DEPTH, 6 * D), f32),
        "norm_g": 1.0 + 0.01 * nrm(ks[11], (DEPTH, 2, D), f32),
        "attn_w_in": D ** -0.5 * nrm(ks[12], (N_ATTN_LAYERS, D, PROJ_COLS), f32),
        "attn_q_gain": 1.0 + 0.01 * nrm(ks[13], (N_ATTN_LAYERS, HEAD_DIM), f32),
        "attn_k_gain": 1.0 + 0.01 * nrm(ks[14], (N_ATTN_LAYERS, HEAD_DIM), f32),
        "attn_w_out": D ** -0.5 * nrm(ks[15], (N_ATTN_LAYERS, N_HEADS * HEAD_DIM, D), f32),
        "conv_w_pw1": D ** -0.5 * nrm(ks[16], (N_CONV_LAYERS, D, 2 * CONV_CH), f32),
        "conv_b_pw1": 0.01 * nrm(ks[17], (N_CONV_LAYERS, 2 * CONV_CH), f32),
        "conv_w_dw": CONV_WIDTH ** -0.5 * nrm(ks[18], (N_CONV_LAYERS, CONV_WIDTH, CONV_CH), f32),
        "conv_b_dw": 0.01 * nrm(ks[19], (N_CONV_LAYERS, CONV_CH), f32),
        "conv_ln_g": 1.0 + 0.01 * nrm(ks[20], (N_CONV_LAYERS, CONV_CH), f32),
        "conv_ln_b": 0.01 * nrm(ks[21], (N_CONV_LAYERS, CONV_CH), f32),
        "conv_w_pw2": CONV_CH ** -0.5 * nrm(ks[22], (N_CONV_LAYERS, CONV_CH, D), f32),
        "peer_w_q": D ** -0.5 * nrm(ks[23], (DEPTH, D, PEER_HEADS * PEER_QDIM), f32),
        "peer_sub_keys": PEER_HALF ** -0.5 * nrm(ks[24], (DEPTH, PEER_HEADS, 2, PEER_NKEYS, PEER_HALF), f32),
        "peer_u": D ** -0.5 * nrm(ks[25], (DEPTH, PEER_EXPERTS, D), f32),
        "peer_v": PEER_HEADS ** -0.5 * nrm(ks[26], (DEPTH, PEER_EXPERTS, D), f32),
    }


def reference(x_prompt, x_sample, cache_k, cache_v, cache_idx_k, state_conv, page_table,
              c_prompt, c_sample, w_ada, b_ada, norm_g, attn_w_in, attn_q_gain, attn_k_gain,
              attn_w_out, conv_w_pw1, conv_b_pw1, conv_w_dw, conv_b_dw, conv_ln_g, conv_ln_b,
              conv_w_pw2, peer_w_q, peer_sub_keys, peer_u, peer_v):
    yp, ys = x_prompt, x_sample
    kp_l, vp_l, ikp_l, ks_l, vs_l, iks_l, cp_l, cs_l = [], [], [], [], [], [], [], []
    for li in range(DEPTH):
        sh1p, sc1p, g1p, sh2p, sc2p, g2p = ada_modulation(c_prompt, w_ada[li], b_ada[li])
        sh1s, sc1s, g1s, sh2s, sc2s, g2s = ada_modulation(c_sample, w_ada[li], b_ada[li])
        hp = modulate(rms_norm(yp, norm_g[li, 0]), sh1p, sc1p)
        hs = modulate(rms_norm(ys, norm_g[li, 0]), sh1s, sc1s)
        if li % 2 == 0:
            ai = li // 2
            aw = (attn_w_in[ai], attn_q_gain[ai], attn_k_gain[ai], attn_w_out[ai])
            op, kp, vp, ikp = attention_prompt(hp, *aw)
            os_, k_s, v_s, ik_s = attention_sample(hs, cache_k[ai], cache_v[ai], cache_idx_k[ai], page_table, *aw)
            kp_l.append(kp); vp_l.append(vp); ikp_l.append(ikp)
            ks_l.append(k_s); vs_l.append(v_s); iks_l.append(ik_s)
        else:
            ci = li // 2
            cw = (conv_w_pw1[ci], conv_b_pw1[ci], conv_w_dw[ci], conv_b_dw[ci],
                  conv_ln_g[ci], conv_ln_b[ci], conv_w_pw2[ci])
            hist0 = jnp.zeros((hp.shape[0], CONV_WIDTH - 1, CONV_CH), hp.dtype)
            op, stp = conformer_conv(hp, hist0, *cw)
            os_, sts = conformer_conv(hs, state_conv[ci], *cw)
            cp_l.append(stp); cs_l.append(sts)
        yp = yp + g1p * op
        ys = ys + g1s * os_
        pw = (peer_w_q[li], peer_sub_keys[li], peer_u[li], peer_v[li])
        hp = modulate(rms_norm(yp, norm_g[li, 1]), sh2p, sc2p)
        hs = modulate(rms_norm(ys, norm_g[li, 1]), sh2s, sc2s)
        yp = yp + g2p * peer(hp, *pw)
        ys = ys + g2s * peer(hs, *pw)
    return (yp, ys, jnp.stack(kp_l), jnp.stack(vp_l), jnp.stack(ikp_l),
            jnp.stack(ks_l), jnp.stack(vs_l), jnp.stack(iks_l),
            jnp.stack(cp_l), jnp.stack(cs_l))
```

```python
import functools
import math

import numpy as np
import jax
import jax.numpy as jnp
from jax import lax
from jax.experimental import pallas as pl
from jax.experimental.pallas import tpu as pltpu

f32 = jnp.float32
bf16 = jnp.bfloat16
i32 = jnp.int32

D_MODEL = 2048
N_HEADS = 16
HEAD_DIM = 128
KV_HEADS = 4
Q_PER_KV = N_HEADS // KV_HEADS
KV_COLS = KV_HEADS * HEAD_DIM
IDX_HEADS = 16
IDX_DIM = 64
TOPK_MAX = 256
ROPE_THETA = 10000.0
PAGE_SIZE = 128
CONV_WIDTH = 31
PEER_HEADS = 8
PEER_NKEYS = 128
PEER_TOPK = 16
RMS_EPS = 1e-6
LN_EPS = 1e-5
NEG = -1e30
MASKED = -1e30
INT_MIN = -(2 ** 31)
VMEM_LIMIT = 56 * 1024 * 1024
ADA_ROWS = 40
ADA_PROMPT_ROW = 32

NT_DIMS = (((1,), (1,)), ((), ()))


def _params(*sem):
    return pltpu.CompilerParams(dimension_semantics=sem, vmem_limit_bytes=VMEM_LIMIT)


def _ada_kernel(c_ref, w_ref, b_ref, o_ref):
    c = c_ref[...]
    a = (c * jax.nn.sigmoid(c)).astype(bf16)
    o_ref[...] = jnp.dot(a, w_ref[...].astype(bf16), preferred_element_type=f32) + b_ref[...]


def ada_modulation(c_all, w_ada, b_ada):
    depth, d, _ = w_ada.shape
    tn = 1024
    nj = d // tn
    return pl.pallas_call(
        _ada_kernel,
        grid=(depth, 6, nj),
        in_specs=[
            pl.BlockSpec((ADA_ROWS, d), lambda l, k, j: (0, 0)),
            pl.BlockSpec((None, d, tn), lambda l, k, j: (l, 0, k * nj + j)),
            pl.BlockSpec((None, 1, tn), lambda l, k, j: (l, 0, k * nj + j)),
        ],
        out_specs=pl.BlockSpec((None, None, ADA_ROWS, tn), lambda l, k, j: (l, k, 0, j)),
        out_shape=jax.ShapeDtypeStruct((depth, 6, ADA_ROWS, d), f32),
        compiler_params=_params("parallel", "parallel", "parallel"),
        name="ada",
    )(c_all, w_ada, b_ada.reshape(depth, 1, 6 * d))


def _mod_operand(mod, sample):
    return mod if sample else mod.reshape(mod.shape[:3] + (1, mod.shape[3]))


def _mod_spec(sample, layer, k):
    if sample:
        return pl.BlockSpec((None, None, 32, D_MODEL), lambda g, t, *_: (layer, k, 0, 0))
    return pl.BlockSpec((None, None, None, 1, D_MODEL), lambda g, t, *_: (layer, k, ADA_PROMPT_ROW + g, 0, 0))


def _normmod_kernel(*refs, has_delta, want_h):
    refs = list(refs)
    y = refs.pop(0)[...]
    if has_delta:
        d_ref, gate_ref = refs.pop(0), refs.pop(0)
        y = y + gate_ref[...] * d_ref[...]
    if want_h:
        g_ref, sh_ref, sc_ref = refs.pop(0), refs.pop(0), refs.pop(0)
    if has_delta:
        refs.pop(0)[...] = y
    if want_h:
        ms = jnp.mean(y * y, axis=-1, keepdims=True)
        hn = y * lax.rsqrt(ms + RMS_EPS) * g_ref[...]
        refs.pop(0)[...] = (hn * (1.0 + sc_ref[...]) + sh_ref[...]).astype(bf16)


def normmod(y, mod, sample, *, delta=None, gate=None, norm_g=None, shift=None, scale=None):
    G, T, Dm = y.shape
    tm = min(T, 256)
    tok = pl.BlockSpec((None, tm, Dm), lambda g, t: (g, t, 0))
    modop = _mod_operand(mod, sample)
    ins, specs = [y], [tok]
    if delta is not None:
        ins += [delta, modop]
        specs += [tok, _mod_spec(sample, *gate)]
    if norm_g is not None:
        ins += [norm_g, modop, modop]
        specs += [pl.BlockSpec((1, Dm), lambda g, t: (0, 0)), _mod_spec(sample, *shift), _mod_spec(sample, *scale)]
    outs, ospecs = [], []
    if delta is not None:
        outs.append(jax.ShapeDtypeStruct((G, T, Dm), f32)); ospecs.append(tok)
    if norm_g is not None:
        outs.append(jax.ShapeDtypeStruct((G, T, Dm), bf16)); ospecs.append(tok)
    res = pl.pallas_call(
        functools.partial(_normmod_kernel, has_delta=delta is not None, want_h=norm_g is not None),
        grid=(G, T // tm), in_specs=specs, out_specs=ospecs, out_shape=outs,
        compiler_params=_params("parallel", "parallel"), name="normmod",
    )(*ins)
    return res if len(res) > 1 else res[0]


def _matmul_kernel(x_ref, w_ref, o_ref):
    o_ref[...] = jnp.dot(x_ref[...], w_ref[...].astype(bf16), preferred_element_type=f32).astype(o_ref.dtype)


def token_matmul(x, w, out_dtype):
    G, T, K = x.shape
    N = w.shape[1]
    tm, tn = min(T, 1024), 512
    return pl.pallas_call(
        _matmul_kernel,
        grid=(G, T // tm, N // tn),
        in_specs=[pl.BlockSpec((None, tm, K), lambda g, t, j: (g, t, 0)),
                  pl.BlockSpec((K, tn), lambda g, t, j: (0, j))],
        out_specs=pl.BlockSpec((None, tm, tn), lambda g, t, j: (g, t, j)),
        out_shape=jax.ShapeDtypeStruct((G, T, N), out_dtype),
        compiler_params=_params("parallel", "parallel", "parallel"), name="token_matmul",
    )(x, w)


def _rope_tables(pos, dim):
    half = dim // 2
    inv = ROPE_THETA ** (-jnp.arange(half, dtype=f32) / half)
    ang = pos.astype(f32)[:, None] * inv
    cos, sin = jnp.cos(ang), jnp.sin(ang)
    reps = 128 // dim
    return jnp.tile(jnp.concatenate([cos, cos], -1), (1, reps)), jnp.tile(jnp.concatenate([-sin, sin], -1), (1, reps))


def _head_rmsnorm(x, gain):
    ms = jnp.mean(x * x, axis=-1, keepdims=True)
    return x * lax.rsqrt(ms + RMS_EPS) * gain


def _qproj_kernel(x_ref, w_ref, gain_ref, cos_ref, sin_ref, o_ref):
    acc = jnp.dot(x_ref[...], w_ref[...].astype(bf16), preferred_element_type=f32)
    cos, sin, gain = cos_ref[...], sin_ref[...], gain_ref[...]
    for h in range(4):
        xn = _head_rmsnorm(acc[:, HEAD_DIM * h:HEAD_DIM * (h + 1)], gain)
        r = xn * cos + pltpu.roll(xn, HEAD_DIM // 2, 1) * sin
        o_ref[h] = (r * (HEAD_DIM ** -0.5)).astype(bf16)


def q_projection(h, w_in, q_gain, cos, sin):
    G, T, K = h.shape
    tm = min(T, 1024)
    return pl.pallas_call(
        _qproj_kernel,
        grid=(G, T // tm, 4),
        in_specs=[pl.BlockSpec((None, tm, K), lambda g, t, j: (g, t, 0)),
                  pl.BlockSpec((K, 512), lambda g, t, j: (0, j)),
                  pl.BlockSpec((1, HEAD_DIM), lambda g, t, j: (0, 0)),
                  pl.BlockSpec((tm, 128), lambda g, t, j: (t, 0)),
                  pl.BlockSpec((tm, 128), lambda g, t, j: (t, 0))],
        out_specs=pl.BlockSpec((None, 4, tm, HEAD_DIM), lambda g, t, j: (g, j, t, 0)),
        out_shape=jax.ShapeDtypeStruct((G, N_HEADS, T, HEAD_DIM), bf16),
        compiler_params=_params("parallel", "parallel", "parallel"), name="q_proj",
    )(h, w_in, q_gain.reshape(1, HEAD_DIM), cos, sin)


def _kvproj_kernel(x_ref, wk_ref, wv_ref, gain_ref, cos_ref, sin_ref, k_ref, kb_ref, v_ref, vb_ref):
    x = x_ref[...]
    kacc = jnp.dot(x, wk_ref[...].astype(bf16), preferred_element_type=f32)
    cos, sin, gain = cos_ref[...], sin_ref[...], gain_ref[...]
    for h in range(KV_HEADS):
        sl = slice(HEAD_DIM * h, HEAD_DIM * (h + 1))
        xn = _head_rmsnorm(kacc[:, sl], gain)
        r = xn * cos + pltpu.roll(xn, HEAD_DIM // 2, 1) * sin
        k_ref[:, sl] = r
        kb_ref[:, sl] = r.astype(bf16)
    v = jnp.dot(x, wv_ref[...].astype(bf16), preferred_element_type=f32)
    v_ref[...] = v
    vb_ref[...] = v.astype(bf16)


def kv_projection(h, w_in, k_gain, cos, sin):
    G, T, K = h.shape
    tm = min(T, 1024)
    kcol = (N_HEADS * HEAD_DIM) // KV_COLS
    tok = pl.BlockSpec((None, tm, KV_COLS), lambda g, t: (g, t, 0))
    return pl.pallas_call(
        _kvproj_kernel,
        grid=(G, T // tm),
        in_specs=[pl.BlockSpec((None, tm, K), lambda g, t: (g, t, 0)),
                  pl.BlockSpec((K, KV_COLS), lambda g, t: (0, kcol)),
                  pl.BlockSpec((K, KV_COLS), lambda g, t: (0, kcol + 1)),
                  pl.BlockSpec((1, HEAD_DIM), lambda g, t: (0, 0)),
                  pl.BlockSpec((tm, 128), lambda g, t: (t, 0)),
                  pl.BlockSpec((tm, 128), lambda g, t: (t, 0))],
        out_specs=[tok, tok, tok, tok],
        out_shape=[jax.ShapeDtypeStruct((G, T, KV_COLS), f32), jax.ShapeDtypeStruct((G, T, KV_COLS), bf16),
                   jax.ShapeDtypeStruct((G, T, KV_COLS), f32), jax.ShapeDtypeStruct((G, T, KV_COLS), bf16)],
        compiler_params=_params("parallel", "parallel"), name="kv_proj",
    )(h, w_in, w_in, k_gain.reshape(1, HEAD_DIM), cos, sin)


IDX_W_COLS = 1152


def _idxproj_kernel(x_ref, w_ref, cos_ref, sin_ref, iq_ref, ik_ref, ikb_ref, iw_ref):
    acc = jnp.dot(x_ref[...], w_ref[...].astype(bf16), preferred_element_type=f32)
    cos, sin = cos_ref[...], sin_ref[...]
    lane = lax.broadcasted_iota(i32, cos.shape, 1)
    first = (lane % IDX_DIM) < (IDX_DIM // 2)

    def rope64(x):
        partner = jnp.where(first, pltpu.roll(x, 128 - IDX_DIM // 2, 1), pltpu.roll(x, IDX_DIM // 2, 1))
        return x * cos + partner * sin

    for p in range(IDX_HEADS // 2):
        r = rope64(acc[:, 128 * p:128 * (p + 1)])
        iq_ref[2 * p] = r[:, :IDX_DIM].astype(bf16)
        iq_ref[2 * p + 1] = r[:, IDX_DIM:].astype(bf16)
    slab = acc[:, IDX_HEADS * IDX_DIM:]
    r = rope64(slab)[:, :IDX_DIM]
    ik_ref[...] = r
    ikb_ref[...] = r.astype(bf16)
    iw_ref[...] = slab[:, IDX_DIM:IDX_DIM + IDX_HEADS] * (IDX_HEADS ** -0.5)


def idx_projection(h, w_idx, cos, sin):
    G, T, K = h.shape
    tm = min(T, 512)
    return pl.pallas_call(
        _idxproj_kernel,
        grid=(G, T // tm),
        in_specs=[pl.BlockSpec((None, tm, K), lambda g, t: (g, t, 0)),
                  pl.BlockSpec((K, IDX_W_COLS), lambda g, t: (0, 0)),
                  pl.BlockSpec((tm, 128), lambda g, t: (t, 0)),
                  pl.BlockSpec((tm, 128), lambda g, t: (t, 0))],
        out_specs=[pl.BlockSpec((None, IDX_HEADS, tm, IDX_DIM), lambda g, t: (g, 0, t, 0)),
                   pl.BlockSpec((None, tm, IDX_DIM), lambda g, t: (g, t, 0)),
                   pl.BlockSpec((None, tm, IDX_DIM), lambda g, t: (g, t, 0)),
                   pl.BlockSpec((None, tm, IDX_HEADS), lambda g, t: (g, t, 0))],
        out_shape=[jax.ShapeDtypeStruct((G, IDX_HEADS, T, IDX_DIM), bf16),
                   jax.ShapeDtypeStruct((G, T, IDX_DIM), f32),
                   jax.ShapeDtypeStruct((G, T, IDX_DIM), bf16),
                   jax.ShapeDtypeStruct((G, T, IDX_HEADS), f32)],
        compiler_params=_params("parallel", "parallel"), name="idx_proj",
    )(h, w_idx, cos, sin)


def _sortable(x):
    bits = pltpu.bitcast(jnp.where(x == 0.0, 0.0, x), i32)
    return bits ^ ((bits >> 31) & 0x7FFFFFFF)


def _kth_largest_key(count_ge, shape, k):
    t0 = jnp.where(count_ge(jnp.zeros(shape, i32)) >= k, 0, INT_MIN).astype(i32)

    def body(i, t):
        cand = t + jnp.left_shift(jnp.int32(1), 30 - i)
        return jnp.where(count_ge(cand) >= k, cand, t)

    return lax.fori_loop(0, 31, body, t0)


def _tie_cutoff(count_tied_below, shape, need, nbits):
    def body(i, j):
        cand = j + jnp.left_shift(jnp.int32(1), nbits - 1 - i)
        return jnp.where(count_tied_below(cand) < need, cand, j)

    return lax.fori_loop(0, nbits, body, jnp.zeros(shape, i32))


TQ = 128
TK = 512


def _attn_prompt_kernel(q_ref, k_ref, v_ref, iq_ref, iw_ref, ik_ref, o_ref, keys_ref, m_ref, l_ref, acc_ref, *, topk):
    qi = pl.program_id(1)
    nkc = qi // (TK // TQ) + 1
    iw = iw_ref[...]
    qpos = lax.broadcasted_iota(i32, (TQ, TK), 0) + qi * TQ
    lane = lax.broadcasted_iota(i32, (TQ, TK), 1)

    def score_chunk(c, carry):
        ikc = ik_ref[pl.ds(pl.multiple_of(c * TK, TK), TK), :]
        sc = jnp.zeros((TQ, TK), f32)
        for h in range(IDX_HEADS):
            s = lax.dot_general(iq_ref[h], ikc, NT_DIMS, preferred_element_type=f32)
            sc = sc + jnp.maximum(s, 0.0) * iw[:, h:h + 1]
        sc = jnp.where(lane + c * TK <= qpos, sc, NEG)
        keys_ref[c] = _sortable(sc)
        return carry

    lax.fori_loop(0, nkc, score_chunk, 0)

    def fold(x):
        return x[:, 0:128] + x[:, 128:256] + x[:, 256:384] + x[:, 384:512]

    def count(pred):
        def body(c, acc):
            return acc + fold(jnp.where(pred(c, keys_ref[c]), 1, 0).astype(i32))
        acc = lax.fori_loop(0, nkc, body, jnp.zeros((TQ, 128), i32))
        return jnp.sum(acc, axis=1, keepdims=True)

    thr = _kth_largest_key(lambda cand: count(lambda c, key: key >= cand), (TQ, 1), topk)

    n_ge = count(lambda c, key: key >= thr)

    @pl.when(jnp.max(jnp.where(n_ge != topk, 1, 0)) > 0)
    def _():
        n_gt = count(lambda c, key: key > thr)
        need = topk - n_gt
        cut = _tie_cutoff(lambda j: count(lambda c, key: (key == thr) & (lane + c * TK < j)), (TQ, 1), need, 12)

        def demote(c, carry):
            key = keys_ref[c]
            keys_ref[c] = jnp.where((key == thr) & (lane + c * TK > cut), thr - 1, key)
            return carry

        lax.fori_loop(0, nkc, demote, 0)

    m_ref[...] = jnp.full(m_ref.shape, MASKED, f32)
    l_ref[...] = jnp.zeros(l_ref.shape, f32)
    acc_ref[...] = jnp.zeros(acc_ref.shape, f32)

    def attn_chunk(c, carry):
        off = pl.multiple_of(c * TK, TK)
        sel = (keys_ref[c] >= thr) & (lane + c * TK <= qpos)
        for g in range(KV_HEADS):
            kc = k_ref[pl.ds(off, TK), HEAD_DIM * g:HEAD_DIM * (g + 1)]
            vc = v_ref[pl.ds(off, TK), HEAD_DIM * g:HEAD_DIM * (g + 1)]
            qg = q_ref[Q_PER_KV * g:Q_PER_KV * (g + 1)].reshape(Q_PER_KV * TQ, HEAD_DIM)
            s = lax.dot_general(qg, kc, NT_DIMS, preferred_element_type=f32).reshape(Q_PER_KV, TQ, TK)
            s = jnp.where(sel[None], s, MASKED)
            m_old = m_ref[g]
            m_new = jnp.maximum(m_old, jnp.max(s, axis=-1, keepdims=True))
            p = jnp.exp(s - m_new)
            alpha = jnp.exp(m_old - m_new)
            l_ref[g] = alpha * l_ref[g] + jnp.sum(p, axis=-1, keepdims=True)
            pv = jnp.dot(p.reshape(Q_PER_KV * TQ, TK).astype(bf16), vc, preferred_element_type=f32)
            acc_ref[g] = alpha * acc_ref[g] + pv.reshape(Q_PER_KV, TQ, HEAD_DIM)
            m_ref[g] = m_new
        return carry

    lax.fori_loop(0, nkc, attn_chunk, 0)

    for g in range(KV_HEADS):
        o = acc_ref[g] / l_ref[g]
        for r in range(Q_PER_KV):
            hd = Q_PER_KV * g + r
            o_ref[:, HEAD_DIM * hd:HEAD_DIM * (hd + 1)] = o[r].astype(bf16)


def attention_prompt(q, kb, vb, iq, iw, ikb):
    G, _, T, _ = q.shape
    topk = min(TOPK_MAX, T // 4)
    return pl.pallas_call(
        functools.partial(_attn_prompt_kernel, topk=topk),
        grid=(G, T // TQ),
        in_specs=[pl.BlockSpec((None, N_HEADS, TQ, HEAD_DIM), lambda g, t: (g, 0, t, 0)),
                  pl.BlockSpec((None, T, KV_COLS), lambda g, t: (g, 0, 0)),
                  pl.BlockSpec((None, T, KV_COLS), lambda g, t: (g, 0, 0)),
                  pl.BlockSpec((None, IDX_HEADS, TQ, IDX_DIM), lambda g, t: (g, 0, t, 0)),
                  pl.BlockSpec((None, TQ, IDX_HEADS), lambda g, t: (g, t, 0)),
                  pl.BlockSpec((None, T, IDX_DIM), lambda g, t: (g, 0, 0))],
        out_specs=pl.BlockSpec((None, TQ, N_HEADS * HEAD_DIM), lambda g, t: (g, t, 0)),
        out_shape=jax.ShapeDtypeStruct((G, T, N_HEADS * HEAD_DIM), bf16),
        scratch_shapes=[pltpu.VMEM((T // TK, TQ, TK), i32),
                        pltpu.VMEM((KV_HEADS, Q_PER_KV, TQ, 1), f32),
                        pltpu.VMEM((KV_HEADS, Q_PER_KV, TQ, 1), f32),
                        pltpu.VMEM((KV_HEADS, Q_PER_KV, TQ, HEAD_DIM), f32)],
        compiler_params=_params("parallel", "arbitrary"), name="attn_prompt",
    )(q, kb, vb, iq, iw, ikb)


PAGES_PER_STEP = 8


def _sample_select_kernel(pt_ref, iq_ref, iw_ref, ikn_ref, *refs, topk, n_pages):
    page_refs = refs[:PAGES_PER_STEP]
    sel_ref, seln_ref, sc_ref = refs[PAGES_PER_STEP:]
    j = pl.program_id(1)
    iq = iq_ref[...]
    iw = iw_ref[...]
    iqb = iq.astype(bf16)
    for p in range(PAGES_PER_STEP):
        s = lax.dot_general(iqb, page_refs[p][...].astype(bf16), NT_DIMS, preferred_element_type=f32)
        sc_ref[pl.ds(j * PAGES_PER_STEP + p, 1), :] = jnp.sum(jnp.maximum(s, 0.0) * iw, axis=0, keepdims=True)

    @pl.when(j == pl.num_programs(1) - 1)
    def _():
        past = n_pages * PAGE_SIZE
        s_new = jnp.sum(iq * ikn_ref[...], axis=1, keepdims=True)
        s_new = jnp.sum(jnp.maximum(s_new, 0.0) * iw, axis=0, keepdims=True)
        keys = _sortable(sc_ref[...])
        key_new = _sortable(s_new)
        idx = lax.broadcasted_iota(i32, keys.shape, 0) * PAGE_SIZE + lax.broadcasted_iota(i32, keys.shape, 1)

        def total(x):
            return jnp.sum(jnp.sum(x, axis=0, keepdims=True), axis=1, keepdims=True)

        def count(pred):
            new = jnp.where(pred(key_new, jnp.full((1, 1), past, i32)), 1, 0).astype(i32)
            return total(jnp.where(pred(keys, idx), 1, 0).astype(i32)) + new

        thr = _kth_largest_key(lambda cand: count(lambda k, i: k >= cand), (1, 1), topk)
        n_gt = count(lambda k, i: k > thr)
        need = topk - n_gt
        nbits = int(past).bit_length()
        cut = _tie_cutoff(lambda jj: count(lambda k, i: (k == thr) & (i < jj)), (1, 1), need, nbits)
        chosen = lambda k, i: (k > thr) | ((k == thr) & (i <= cut))
        sel_ref[...] = jnp.where(chosen(keys, idx), 0.0, MASKED)
        seln_ref[...] = jnp.broadcast_to(jnp.where(chosen(key_new, jnp.full((1, 1), past, i32)), 0.0, MASKED), seln_ref.shape)


def sample_select(page_table, iq_s, iw_s, ik_new, cache_ik):
    B, n_pages = page_table.shape
    L = n_pages * PAGE_SIZE + 1
    topk = min(TOPK_MAX, L // 4)
    steps = n_pages // PAGES_PER_STEP

    def page_spec(p):
        return pl.BlockSpec((None, PAGE_SIZE, IDX_DIM), lambda b, j, pt: (pt[b, j * PAGES_PER_STEP + p], 0, 0))

    grid_spec = pltpu.PrefetchScalarGridSpec(
        num_scalar_prefetch=1,
        grid=(B, steps),
        in_specs=[pl.BlockSpec((None, IDX_HEADS, IDX_DIM), lambda b, j, pt: (b, 0, 0)),
                  pl.BlockSpec((None, IDX_HEADS, 1), lambda b, j, pt: (b, 0, 0)),
                  pl.BlockSpec((None, 1, IDX_DIM), lambda b, j, pt: (b, 0, 0))]
                 + [page_spec(p) for p in range(PAGES_PER_STEP)],
        out_specs=[pl.BlockSpec((None, n_pages, PAGE_SIZE), lambda b, j, pt: (b, 0, 0)),
                   pl.BlockSpec((None, 1, PAGE_SIZE), lambda b, j, pt: (b, 0, 0))],
        scratch_shapes=[pltpu.VMEM((n_pages, PAGE_SIZE), f32)],
    )
    return pl.pallas_call(
        functools.partial(_sample_select_kernel, topk=topk, n_pages=n_pages),
        grid_spec=grid_spec,
        out_shape=[jax.ShapeDtypeStruct((B, n_pages, PAGE_SIZE), f32), jax.ShapeDtypeStruct((B, 1, PAGE_SIZE), f32)],
        compiler_params=_params("parallel", "arbitrary"), name="sample_select",
    )(page_table, iq_s, iw_s, ik_new, *([cache_ik] * PAGES_PER_STEP))


def _sample_attn_kernel(pt_ref, q_ref, kn_ref, vn_ref, sel_ref, seln_ref, *refs):
    k_refs = refs[:PAGES_PER_STEP]
    v_refs = refs[PAGES_PER_STEP:2 * PAGES_PER_STEP]
    o_ref, m_ref, l_ref, acc_ref = refs[2 * PAGES_PER_STEP:]
    j = pl.program_id(1)
    head_group = lax.broadcasted_iota(i32, (N_HEADS, KV_COLS), 0) // Q_PER_KV
    col_group = lax.broadcasted_iota(i32, (N_HEADS, KV_COLS), 1) // HEAD_DIM
    own = head_group == col_group
    q = q_ref[...].astype(f32)
    qbd = jnp.where(own, jnp.concatenate([q] * KV_HEADS, axis=1), 0.0)

    @pl.when(j == 0)
    def _():
        s_new = jnp.sum(qbd * kn_ref[...], axis=1, keepdims=True) + seln_ref[:, 0:1]
        m_ref[...] = s_new
        l_ref[...] = jnp.ones(l_ref.shape, f32)
        acc_ref[...] = jnp.broadcast_to(vn_ref[...], acc_ref.shape)

    qb = qbd.astype(bf16)
    s = jnp.concatenate(
        [lax.dot_general(qb, k_refs[p][...].astype(bf16), NT_DIMS, preferred_element_type=f32) + sel_ref[p:p + 1, :]
         for p in range(PAGES_PER_STEP)], axis=1)
    m_old = m_ref[...]
    m_new = jnp.maximum(m_old, jnp.max(s, axis=1, keepdims=True))
    p_ = jnp.exp(s - m_new)
    alpha = jnp.exp(m_old - m_new)
    l_ref[...] = alpha * l_ref[...] + jnp.sum(p_, axis=1, keepdims=True)
    v = jnp.concatenate([v_refs[p][...].astype(bf16) for p in range(PAGES_PER_STEP)], axis=0)
    acc_ref[...] = alpha * acc_ref[...] + jnp.dot(p_.astype(bf16), v, preferred_element_type=f32)
    m_ref[...] = m_new

    @pl.when(j == pl.num_programs(1) - 1)
    def _():
        o = jnp.where(own, acc_ref[...] / l_ref[...], 0.0)
        o_ref[...] = (o[:, 0:128] + o[:, 128:256] + o[:, 256:384] + o[:, 384:512]).astype(bf16)


def sample_attention(page_table, q_s, k_new, v_new, sel, sel_new, cache_k, cache_v):
    B, n_pages = page_table.shape
    steps = n_pages // PAGES_PER_STEP

    def page_spec(p):
        return pl.BlockSpec((None, PAGE_SIZE, KV_COLS), lambda b, j, pt: (pt[b, j * PAGES_PER_STEP + p], 0, 0))

    grid_spec = pltpu.PrefetchScalarGridSpec(
        num_scalar_prefetch=1,
        grid=(B, steps),
        in_specs=[pl.BlockSpec((None, N_HEADS, HEAD_DIM), lambda b, j, pt: (b, 0, 0)),
                  pl.BlockSpec((None, 1, KV_COLS), lambda b, j, pt: (b, 0, 0)),
                  pl.BlockSpec((None, 1, KV_COLS), lambda b, j, pt: (b, 0, 0)),
                  pl.BlockSpec((None, PAGES_PER_STEP, PAGE_SIZE), lambda b, j, pt: (b, j, 0)),
                  pl.BlockSpec((None, 1, PAGE_SIZE), lambda b, j, pt: (b, 0, 0))]
                 + [page_spec(p) for p in range(PAGES_PER_STEP)] * 2,
        out_specs=pl.BlockSpec((None, N_HEADS, HEAD_DIM), lambda b, j, pt: (b, 0, 0)),
        scratch_shapes=[pltpu.VMEM((N_HEADS, 1), f32), pltpu.VMEM((N_HEADS, 1), f32),
                        pltpu.VMEM((N_HEADS, KV_COLS), f32)],
    )
    return pl.pallas_call(
        _sample_attn_kernel, grid_spec=grid_spec,
        out_shape=jax.ShapeDtypeStruct((B, N_HEADS, HEAD_DIM), bf16),
        compiler_params=_params("parallel", "arbitrary"), name="sample_attn",
    )(page_table, q_s, k_new, v_new, sel, sel_new, *([cache_k] * PAGES_PER_STEP), *([cache_v] * PAGES_PER_STEP))


def _glu_kernel(x_ref, wa_ref, wg_ref, ba_ref, bg_ref, o_ref):
    x = x_ref[...]
    a = jnp.dot(x, wa_ref[...].astype(bf16), preferred_element_type=f32) + ba_ref[...]
    g = jnp.dot(x, wg_ref[...].astype(bf16), preferred_element_type=f32) + bg_ref[...]
    o_ref[...] = a * jax.nn.sigmoid(g)


def glu_projection(h, w_pw1, b_pw1):
    G, T, K = h.shape
    C = w_pw1.shape[1] // 2
    tm, tn = min(T, 1024), 512
    nj = C // tn
    b2 = b_pw1.reshape(1, 2 * C)
    return pl.pallas_call(
        _glu_kernel,
        grid=(G, T // tm, nj),
        in_specs=[pl.BlockSpec((None, tm, K), lambda g, t, j: (g, t, 0)),
                  pl.BlockSpec((K, tn), lambda g, t, j: (0, j)),
                  pl.BlockSpec((K, tn), lambda g, t, j: (0, nj + j)),
                  pl.BlockSpec((1, tn), lambda g, t, j: (0, j)),
                  pl.BlockSpec((1, tn), lambda g, t, j: (0, nj + j))],
        out_specs=pl.BlockSpec((None, tm, tn), lambda g, t, j: (g, t, j)),
        out_shape=jax.ShapeDtypeStruct((G, T, C), f32),
        compiler_params=_params("parallel", "parallel", "parallel"), name="glu_proj",
    )(h, w_pw1, w_pw1, b2, b2)


def _ln_silu(y, g, b):
    mu = jnp.mean(y, axis=-1, keepdims=True)
    var = jnp.mean(jnp.square(y - mu), axis=-1, keepdims=True)
    z = (y - mu) * lax.rsqrt(var + LN_EPS) * g + b
    return z * jax.nn.sigmoid(z)


CONV_HALO = 32


CONV_COLS = 256


def _conv_prompt_kernel(u_ref, halo_ref, w_ref, b_ref, g_ref, be_ref, o_ref, ext_ref, y_ref):
    t = pl.program_id(1)
    tt, C = u_ref.shape
    halo = halo_ref[...]
    ext_ref[0:CONV_HALO, :] = jnp.where(t == 0, jnp.zeros_like(halo), halo)
    ext_ref[CONV_HALO:, :] = u_ref[...]
    base = CONV_HALO - (CONV_WIDTH - 1)
    for c in range(C // CONV_COLS):
        cols = slice(CONV_COLS * c, CONV_COLS * (c + 1))
        y = jnp.zeros((tt, CONV_COLS), f32) + b_ref[:, cols]
        for k in range(CONV_WIDTH):
            y = y + ext_ref[base + k:base + k + tt, cols] * w_ref[k:k + 1, cols]
        y_ref[:, cols] = y
    o_ref[...] = _ln_silu(y_ref[...], g_ref[...], be_ref[...]).astype(bf16)


def conv_prompt(u, w_dw, b_dw, ln_g, ln_b):
    G, T, C = u.shape
    tt = 128
    hb = tt // CONV_HALO
    row = pl.BlockSpec((1, C), lambda g, t: (0, 0))
    return pl.pallas_call(
        _conv_prompt_kernel,
        grid=(G, T // tt),
        in_specs=[pl.BlockSpec((None, tt, C), lambda g, t: (g, t, 0)),
                  pl.BlockSpec((None, CONV_HALO, C), lambda g, t: (g, jnp.maximum(t * hb - 1, 0), 0)),
                  pl.BlockSpec((CONV_WIDTH, C), lambda g, t: (0, 0)), row, row, row],
        out_specs=pl.BlockSpec((None, tt, C), lambda g, t: (g, t, 0)),
        out_shape=jax.ShapeDtypeStruct((G, T, C), bf16),
        scratch_shapes=[pltpu.VMEM((CONV_HALO + tt, C), f32), pltpu.VMEM((tt, C), f32)],
        compiler_params=_params("parallel", "parallel"), name="conv_prompt",
    )(u, u, w_dw, b_dw.reshape(1, C), ln_g.reshape(1, C), ln_b.reshape(1, C))


def _conv_sample_kernel(u_ref, hist_ref, w_ref, b_ref, g_ref, be_ref, o_ref):
    w = w_ref[...]
    y = jnp.sum(hist_ref[...] * w[None, :CONV_WIDTH - 1, :], axis=1) + u_ref[...] * w[CONV_WIDTH - 1:, :] + b_ref[...]
    o_ref[...] = _ln_silu(y, g_ref[...], be_ref[...]).astype(bf16)


def conv_sample(u, hist, w_dw, b_dw, ln_g, ln_b):
    B, C = u.shape
    tc = 512
    col = lambda r: pl.BlockSpec((r, tc), lambda j: (0, j))
    return pl.pallas_call(
        _conv_sample_kernel,
        grid=(C // tc,),
        in_specs=[col(B), pl.BlockSpec((B, CONV_WIDTH - 1, tc), lambda j: (0, 0, j)), col(CONV_WIDTH),
                  col(1), col(1), col(1)],
        out_specs=col(B),
        out_shape=jax.ShapeDtypeStruct((B, C), bf16),
        compiler_params=_params("arbitrary"), name="conv_sample",
    )(u, hist, w_dw, b_dw.reshape(1, C), ln_g.reshape(1, C), ln_b.reshape(1, C))


def _top_values(x, n):
    vals = []
    for _ in range(n):
        m = jnp.max(x, axis=0, keepdims=True)
        vals.append(m)
        x = jnp.where(x == m, -jnp.inf, x)
    return vals


def _peer_gate_kernel(q_ref, sk_ref, a_ref, ea_ref, b_ref, eb_ref, thr_ref):
    for h in range(PEER_HEADS):
        st = []
        for p in range(2):
            col = (2 * h + p) * PEER_NKEYS
            st.append(lax.dot_general(sk_ref[h, p].astype(bf16), q_ref[:, col:col + PEER_NKEYS], NT_DIMS,
                                      preferred_element_type=f32))
        a, b = st
        va = _top_values(a, PEER_TOPK)
        vbl = _top_values(b, PEER_TOPK)
        rank = lax.broadcasted_iota(i32, (PEER_TOPK, a.shape[1]), 0)
        vb = jnp.zeros((PEER_TOPK, a.shape[1]), f32)
        for r in range(PEER_TOPK):
            vb = jnp.where(rank == r, vbl[r], vb)
        cand = jnp.concatenate([va[r] + vb for r in range(PEER_TOPK)], axis=0)
        thr = _top_values(cand, PEER_TOPK)[-1]
        top = va[0] + vbl[0]
        z = jnp.sum(jnp.where(cand >= thr, jnp.exp(cand - top), 0.0), axis=0, keepdims=True)
        ea = jnp.exp(a - va[0]) / z
        for grp in range(PEER_NKEYS // 8):
            a_ref[grp, h] = a[8 * grp:8 * (grp + 1), :]
            ea_ref[grp, h] = ea[8 * grp:8 * (grp + 1), :]
        b_ref[h] = b
        eb_ref[h] = jnp.exp(b - vbl[0])
        thr_ref[h:h + 1, :] = thr


def peer_gates(q, sub_keys):
    G, T, _ = q.shape
    tm = min(T, 256)
    a_spec = pl.BlockSpec((None, PEER_NKEYS // 8, PEER_HEADS, 8, tm), lambda g, t: (g, 0, 0, 0, t))
    b_spec = pl.BlockSpec((None, PEER_HEADS, PEER_NKEYS, tm), lambda g, t: (g, 0, 0, t))
    a_shape = jax.ShapeDtypeStruct((G, PEER_NKEYS // 8, PEER_HEADS, 8, T), f32)
    b_shape = jax.ShapeDtypeStruct((G, PEER_HEADS, PEER_NKEYS, T), f32)
    return pl.pallas_call(
        _peer_gate_kernel,
        grid=(G, T // tm),
        in_specs=[pl.BlockSpec((None, tm, q.shape[2]), lambda g, t: (g, t, 0)),
                  pl.BlockSpec(sub_keys.shape, lambda g, t: (0, 0, 0, 0))],
        out_specs=[a_spec, a_spec, b_spec, b_spec, pl.BlockSpec((None, PEER_HEADS, tm), lambda g, t: (g, 0, t))],
        out_shape=[a_shape, a_shape, b_shape, b_shape, jax.ShapeDtypeStruct((G, PEER_HEADS, T), f32)],
        compiler_params=_params("parallel", "parallel"), name="peer_gates",
    )(q, sub_keys)


PEER_TE = 256


def _gelu(x):
    return 0.5 * x * (1.0 + lax.erf(x * (2.0 ** -0.5)))


def _peer_expert_kernel(x_ref, u_ref, v_ref, a_ref, ea_ref, b_ref, eb_ref, thr_ref, o_ref):
    e = pl.program_id(2)
    tm = x_ref.shape[0]
    rows = PEER_TE // PEER_NKEYS
    hid = lax.dot_general(x_ref[...], u_ref[...].astype(bf16), NT_DIMS, preferred_element_type=f32)
    act = _gelu(hid)
    isub0 = (e % (8 // rows)) * rows
    sub = lax.broadcasted_iota(i32, (8, 128), 0)

    def pick_row(tile, i):
        return jnp.sum(jnp.where(sub == i, tile, 0.0), axis=0, keepdims=True)

    blocks = []
    for r in range(rows):
        cols = []
        for c in range(tm // 128):
            tok = slice(128 * c, 128 * (c + 1))
            gt = jnp.zeros((PEER_NKEYS, 128), f32)
            for h in range(PEER_HEADS):
                ai = pick_row(a_ref[h, :, tok], isub0 + r)
                eai = pick_row(ea_ref[h, :, tok], isub0 + r)
                active = (ai + b_ref[h, :, tok]) >= thr_ref[h:h + 1, tok]
                gt = gt + jnp.where(active, eai * eb_ref[h, :, tok], 0.0)
            cols.append(gt.T)
        blocks.append(jnp.concatenate(cols, axis=0) * act[:, PEER_NKEYS * r:PEER_NKEYS * (r + 1)])
    w = jnp.concatenate(blocks, axis=1).astype(bf16)
    contrib = jnp.dot(w, v_ref[...].astype(bf16), preferred_element_type=f32)

    @pl.when(e == 0)
    def _():
        o_ref[...] = contrib

    @pl.when(e > 0)
    def _():
        o_ref[...] += contrib


def peer_experts(x, gates, u_tab, v_tab, layer):
    G, T, Dm = x.shape
    a, ea, b, eb, thr = gates
    tm = min(T, 512)
    n_e = u_tab.shape[1] // PEER_TE
    per_group = 8 * PEER_NKEYS // PEER_TE
    a_spec = pl.BlockSpec((None, None, PEER_HEADS, 8, tm), lambda g, t, e: (g, e // per_group, 0, 0, t))
    b_spec = pl.BlockSpec((None, PEER_HEADS, PEER_NKEYS, tm), lambda g, t, e: (g, 0, 0, t))
    tab_spec = pl.BlockSpec((None, PEER_TE, Dm), lambda g, t, e: (layer, e, 0))
    return pl.pallas_call(
        _peer_expert_kernel,
        grid=(G, T // tm, n_e),
        in_specs=[pl.BlockSpec((None, tm, Dm), lambda g, t, e: (g, t, 0)), tab_spec, tab_spec,
                  a_spec, a_spec, b_spec, b_spec,
                  pl.BlockSpec((None, PEER_HEADS, tm), lambda g, t, e: (g, 0, t))],
        out_specs=pl.BlockSpec((None, tm, Dm), lambda g, t, e: (g, t, 0)),
        out_shape=jax.ShapeDtypeStruct((G, T, Dm), f32),
        compiler_params=_params("parallel", "parallel", "arbitrary"), name="peer_experts",
    )(x, u_tab, v_tab, a, ea, b, eb, thr)


def peer(h, w_q, sub_keys, u_tab, v_tab, layer):
    G, T, Dm = h.shape
    Tp = -(-T // 128) * 128
    if Tp != T:
        h = jnp.pad(h, ((0, 0), (0, Tp - T), (0, 0)))
    q = token_matmul(h, w_q, bf16)
    out = peer_experts(h, peer_gates(q, sub_keys), u_tab, v_tab, layer)
    return out[:, :T] if Tp != T else out


def _trunk(y, sample, mod, norm_g, attend, conv, weights):
    (w_in, w_idx, q_gain, k_gain, w_out, rope128, rope64, w_pw1, b_pw1, w_pw2, peer_w_q, peer_sub_keys, peer_u,
     peer_v) = weights
    nm = functools.partial(normmod, mod=mod, sample=sample)
    h = nm(y, norm_g=norm_g[0, 0], shift=(0, 0), scale=(0, 1))
    q = q_projection(h, w_in, q_gain, *rope128)
    k, kb, v, vb = kv_projection(h, w_in, k_gain, *rope128)
    iq, ik, ikb, iw = idx_projection(h, w_idx, *rope64)
    o = attend(q, k, kb, v, vb, iq, ik, ikb, iw)
    d = token_matmul(o, w_out, f32)
    y, h = nm(y, delta=d, gate=(0, 2), norm_g=norm_g[0, 1], shift=(0, 3), scale=(0, 4))
    d = peer(h, peer_w_q[0], peer_sub_keys[0], peer_u, peer_v, 0)
    y, h = nm(y, delta=d, gate=(0, 5), norm_g=norm_g[1, 0], shift=(1, 0), scale=(1, 1))
    u = glu_projection(h, w_pw1, b_pw1)
    c, state = conv(u)
    d = token_matmul(c, w_pw2, f32)
    y, h = nm(y, delta=d, gate=(1, 2), norm_g=norm_g[1, 1], shift=(1, 3), scale=(1, 4))
    d = peer(h, peer_w_q[1], peer_sub_keys[1], peer_u, peer_v, 1)
    y = nm(y, delta=d, gate=(1, 5))
    return y, k, v, ik, state


def kernel(x_prompt, x_sample, cache_k, cache_v, cache_idx_k, state_conv, page_table, c_prompt, c_sample, w_ada, b_ada, norm_g, attn_w_in, attn_q_gain, attn_k_gain, attn_w_out, conv_w_pw1, conv_b_pw1, conv_w_dw, conv_b_dw, conv_ln_g, conv_ln_b, conv_w_pw2, peer_w_q, peer_sub_keys, peer_u, peer_v):
    B, T, Dm = x_prompt.shape
    Bd = x_sample.shape[0]
    n_pages = page_table.shape[1]
    past = n_pages * PAGE_SIZE
    n_phys = cache_k.shape[1]

    c_all = jnp.concatenate([c_sample, c_prompt, jnp.zeros((ADA_ROWS - Bd - B, Dm), f32)], axis=0)
    mod = ada_modulation(c_all, w_ada, b_ada)
    ng = norm_g.reshape(norm_g.shape[0], 2, 1, Dm)

    w_in = attn_w_in[0]
    idx0 = N_HEADS * HEAD_DIM + 2 * KV_COLS
    w_idx = jnp.concatenate([w_in[:, idx0:], jnp.zeros((Dm, IDX_W_COLS - (w_in.shape[1] - idx0)), f32)], axis=1)

    def weights(pos):
        return (w_in, w_idx, attn_q_gain[0], attn_k_gain[0], attn_w_out[0], _rope_tables(pos, HEAD_DIM),
                _rope_tables(pos, IDX_DIM), conv_w_pw1[0], conv_b_pw1[0], conv_w_pw2[0], peer_w_q, peer_sub_keys,
                peer_u, peer_v)

    conv_tail = (conv_w_dw[0], conv_b_dw[0], conv_ln_g[0], conv_ln_b[0])

    def attend_prompt(q, k, kb, v, vb, iq, ik, ikb, iw):
        return attention_prompt(q, kb, vb, iq, iw, ikb)

    def conv_p(u):
        return conv_prompt(u, *conv_tail), u[:, T - (CONV_WIDTH - 1):]

    def attend_sample(q, k, kb, v, vb, iq, ik, ikb, iw):
        iq_s = jnp.transpose(iq[0], (1, 0, 2)).astype(f32)
        sel, sel_new = sample_select(page_table, iq_s, iw.reshape(Bd, IDX_HEADS, 1), ik.reshape(Bd, 1, IDX_DIM),
                                     cache_idx_k[0])
        o = sample_attention(page_table, jnp.transpose(q[0], (1, 0, 2)), k.reshape(Bd, 1, KV_COLS),
                             v.reshape(Bd, 1, KV_COLS), sel, sel_new,
                             cache_k[0].reshape(n_phys, PAGE_SIZE, KV_COLS), cache_v[0].reshape(n_phys, PAGE_SIZE, KV_COLS))
        return o.reshape(1, Bd, N_HEADS * HEAD_DIM)

    def conv_s(u):
        hist = state_conv[0]
        c = conv_sample(u[0], hist, *conv_tail)
        return c[None], jnp.concatenate([hist[:, 1:], u[0][:, None, :]], axis=1)

    yp, kp, vp, ikp, stp = _trunk(x_prompt, False, mod, ng, attend_prompt, conv_p,
                                  weights(jnp.arange(T, dtype=i32)))
    ys, ks, vs, iks, sts = _trunk(x_sample.reshape(1, Bd, Dm), True, mod, ng, attend_sample, conv_s,
                                  weights(jnp.full((Bd,), past, i32)))
    return (yp, ys.reshape(Bd, 1, Dm),
            kp.reshape(1, B, T, KV_HEADS, HEAD_DIM), vp.reshape(1, B, T, KV_HEADS, HEAD_DIM),
            ikp.reshape(1, B, T, IDX_DIM),
            ks.reshape(1, Bd, 1, KV_HEADS, HEAD_DIM), vs.reshape(1, Bd, 1, KV_HEADS, HEAD_DIM),
            iks.reshape(1, Bd, 1, IDX_DIM),
            stp[None], sts[None])
```

```python
import functools
import math

import numpy as np
import jax
import jax.numpy as jnp
from jax import lax
from jax.experimental import pallas as pl
from jax.experimental.pallas import tpu as pltpu

f32 = jnp.float32
bf16 = jnp.bfloat16
i32 = jnp.int32

D_MODEL = 2048
N_HEADS = 16
HEAD_DIM = 128
KV_HEADS = 4
Q_PER_KV = N_HEADS // KV_HEADS
KV_COLS = KV_HEADS * HEAD_DIM
IDX_HEADS = 16
IDX_DIM = 64
TOPK_MAX = 256
ROPE_THETA = 10000.0
PAGE_SIZE = 128
CONV_WIDTH = 31
PEER_HEADS = 8
PEER_NKEYS = 128
PEER_TOPK = 16
RMS_EPS = 1e-6
LN_EPS = 1e-5
NEG = -1e30
MASKED = -1e30
INT_MIN = -(2 ** 31)
VMEM_LIMIT = 56 * 1024 * 1024
BF16_ROWS = 16
ADA_ROWS = 40
ADA_PROMPT_ROW = 32

NT_DIMS = (((1,), (1,)), ((), ()))


def _params(*sem):
    return pltpu.CompilerParams(dimension_semantics=sem, vmem_limit_bytes=VMEM_LIMIT)


def _ada_kernel(c_ref, w_ref, b_ref, o_ref):
    c = c_ref[...]
    a = (c * jax.nn.sigmoid(c)).astype(bf16)
    o_ref[...] = jnp.dot(a, w_ref[...].astype(bf16), preferred_element_type=f32) + b_ref[...]


def ada_modulation(c_all, w_ada, b_ada):
    depth, d, _ = w_ada.shape
    tn = 1024
    nj = d // tn
    return pl.pallas_call(
        _ada_kernel,
        grid=(depth, 6, nj),
        in_specs=[
            pl.BlockSpec((ADA_ROWS, d), lambda l, k, j: (0, 0)),
            pl.BlockSpec((None, d, tn), lambda l, k, j: (l, 0, k * nj + j)),
            pl.BlockSpec((None, 1, tn), lambda l, k, j: (l, 0, k * nj + j)),
        ],
        out_specs=pl.BlockSpec((None, None, ADA_ROWS, tn), lambda l, k, j: (l, k, 0, j)),
        out_shape=jax.ShapeDtypeStruct((depth, 6, ADA_ROWS, d), f32),
        compiler_params=_params("parallel", "parallel", "parallel"),
        name="ada",
    )(c_all, w_ada, b_ada.reshape(depth, 1, 6 * d))


def _mod_operand(mod, sample):
    return mod if sample else mod.reshape(mod.shape[:3] + (1, mod.shape[3]))


def _mod_spec(sample, layer, k):
    if sample:
        return pl.BlockSpec((None, None, 32, D_MODEL), lambda g, t, *_: (layer, k, 0, 0))
    return pl.BlockSpec((None, None, None, 1, D_MODEL), lambda g, t, *_: (layer, k, ADA_PROMPT_ROW + g, 0, 0))


def _normmod_kernel(*refs, has_delta, want_h):
    refs = list(refs)
    y = refs.pop(0)[...]
    if has_delta:
        d_ref, gate_ref = refs.pop(0), refs.pop(0)
        y = y + gate_ref[...] * d_ref[...]
    if want_h:
        g_ref, sh_ref, sc_ref = refs.pop(0), refs.pop(0), refs.pop(0)
    if has_delta:
        refs.pop(0)[...] = y
    if want_h:
        ms = jnp.mean(y * y, axis=-1, keepdims=True)
        hn = y * lax.rsqrt(ms + RMS_EPS) * g_ref[...]
        refs.pop(0)[...] = (hn * (1.0 + sc_ref[...]) + sh_ref[...]).astype(bf16)


def normmod(y, mod, sample, *, delta=None, gate=None, norm_g=None, shift=None, scale=None):
    G, T, Dm = y.shape
    tm = min(T, 256)
    tok = pl.BlockSpec((None, tm, Dm), lambda g, t: (g, t, 0))
    modop = _mod_operand(mod, sample)
    ins, specs = [y], [tok]
    if delta is not None:
        ins += [delta, modop]
        specs += [tok, _mod_spec(sample, *gate)]
    if norm_g is not None:
        ins += [norm_g, modop, modop]
        specs += [pl.BlockSpec((1, Dm), lambda g, t: (0, 0)), _mod_spec(sample, *shift), _mod_spec(sample, *scale)]
    outs, ospecs = [], []
    if delta is not None:
        outs.append(jax.ShapeDtypeStruct((G, T, Dm), f32)); ospecs.append(tok)
    if norm_g is not None:
        outs.append(jax.ShapeDtypeStruct((G, T, Dm), bf16)); ospecs.append(tok)
    res = pl.pallas_call(
        functools.partial(_normmod_kernel, has_delta=delta is not None, want_h=norm_g is not None),
        grid=(G, T // tm), in_specs=specs, out_specs=ospecs, out_shape=outs,
        compiler_params=_params("parallel", "parallel"), name="normmod",
    )(*ins)
    return res if len(res) > 1 else res[0]


def _matmul_kernel(x_ref, w_ref, o_ref):
    o_ref[...] = jnp.dot(x_ref[...], w_ref[...].astype(bf16), preferred_element_type=f32).astype(o_ref.dtype)


def token_matmul(x, w, out_dtype):
    G, T, K = x.shape
    N = w.shape[1]
    tm, tn = min(T, 1024), 512
    return pl.pallas_call(
        _matmul_kernel,
        grid=(G, T // tm, N // tn),
        in_specs=[pl.BlockSpec((None, tm, K), lambda g, t, j: (g, t, 0)),
                  pl.BlockSpec((K, tn), lambda g, t, j: (0, j))],
        out_specs=pl.BlockSpec((None, tm, tn), lambda g, t, j: (g, t, j)),
        out_shape=jax.ShapeDtypeStruct((G, T, N), out_dtype),
        compiler_params=_params("parallel", "parallel", "parallel"), name="token_matmul",
    )(x, w)


def _rope_tables(pos, dim):
    half = dim // 2
    inv = ROPE_THETA ** (-jnp.arange(half, dtype=f32) / half)
    ang = pos.astype(f32)[:, None] * inv
    cos, sin = jnp.cos(ang), jnp.sin(ang)
    reps = 128 // dim
    return jnp.tile(jnp.concatenate([cos, cos], -1), (1, reps)), jnp.tile(jnp.concatenate([-sin, sin], -1), (1, reps))


def _head_rmsnorm(x, gain):
    ms = jnp.mean(x * x, axis=-1, keepdims=True)
    return x * lax.rsqrt(ms + RMS_EPS) * gain


def _qproj_kernel(x_ref, w_ref, gain_ref, cos_ref, sin_ref, o_ref):
    acc = jnp.dot(x_ref[...], w_ref[...].astype(bf16), preferred_element_type=f32)
    cos, sin, gain = cos_ref[...], sin_ref[...], gain_ref[...]
    for h in range(4):
        xn = _head_rmsnorm(acc[:, HEAD_DIM * h:HEAD_DIM * (h + 1)], gain)
        r = xn * cos + pltpu.roll(xn, HEAD_DIM // 2, 1) * sin
        o_ref[h] = (r * (HEAD_DIM ** -0.5)).astype(bf16)


def q_projection(h, w_in, q_gain, cos, sin):
    G, T, K = h.shape
    tm = min(T, 1024)
    return pl.pallas_call(
        _qproj_kernel,
        grid=(G, T // tm, 4),
        in_specs=[pl.BlockSpec((None, tm, K), lambda g, t, j: (g, t, 0)),
                  pl.BlockSpec((K, 512), lambda g, t, j: (0, j)),
                  pl.BlockSpec((1, HEAD_DIM), lambda g, t, j: (0, 0)),
                  pl.BlockSpec((tm, 128), lambda g, t, j: (t, 0)),
                  pl.BlockSpec((tm, 128), lambda g, t, j: (t, 0))],
        out_specs=pl.BlockSpec((None, 4, tm, HEAD_DIM), lambda g, t, j: (g, j, t, 0)),
        out_shape=jax.ShapeDtypeStruct((G, N_HEADS, T, HEAD_DIM), bf16),
        compiler_params=_params("parallel", "parallel", "parallel"), name="q_proj",
    )(h, w_in, q_gain.reshape(1, HEAD_DIM), cos, sin)


def _kvproj_kernel(x_ref, wk_ref, wv_ref, gain_ref, cos_ref, sin_ref, k_ref, kb_ref, v_ref, vb_ref):
    x = x_ref[...]
    kacc = jnp.dot(x, wk_ref[...].astype(bf16), preferred_element_type=f32)
    cos, sin, gain = cos_ref[...], sin_ref[...], gain_ref[...]
    for h in range(KV_HEADS):
        sl = slice(HEAD_DIM * h, HEAD_DIM * (h + 1))
        xn = _head_rmsnorm(kacc[:, sl], gain)
        r = xn * cos + pltpu.roll(xn, HEAD_DIM // 2, 1) * sin
        k_ref[:, sl] = r
        kb_ref[:, sl] = r.astype(bf16)
    v = jnp.dot(x, wv_ref[...].astype(bf16), preferred_element_type=f32)
    v_ref[...] = v
    vb_ref[...] = v.astype(bf16)


def kv_projection(h, w_in, k_gain, cos, sin):
    G, T, K = h.shape
    tm = min(T, 1024)
    kcol = (N_HEADS * HEAD_DIM) // KV_COLS
    tok = pl.BlockSpec((None, tm, KV_COLS), lambda g, t: (g, t, 0))
    return pl.pallas_call(
        _kvproj_kernel,
        grid=(G, T // tm),
        in_specs=[pl.BlockSpec((None, tm, K), lambda g, t: (g, t, 0)),
                  pl.BlockSpec((K, KV_COLS), lambda g, t: (0, kcol)),
                  pl.BlockSpec((K, KV_COLS), lambda g, t: (0, kcol + 1)),
                  pl.BlockSpec((1, HEAD_DIM), lambda g, t: (0, 0)),
                  pl.BlockSpec((tm, 128), lambda g, t: (t, 0)),
                  pl.BlockSpec((tm, 128), lambda g, t: (t, 0))],
        out_specs=[tok, tok, tok, tok],
        out_shape=[jax.ShapeDtypeStruct((G, T, KV_COLS), f32), jax.ShapeDtypeStruct((G, T, KV_COLS), bf16),
                   jax.ShapeDtypeStruct((G, T, KV_COLS), f32), jax.ShapeDtypeStruct((G, T, KV_COLS), bf16)],
        compiler_params=_params("parallel", "parallel"), name="kv_proj",
    )(h, w_in, w_in, k_gain.reshape(1, HEAD_DIM), cos, sin)


IDX_W_COLS = 1152


def _idxproj_kernel(x_ref, w_ref, cos_ref, sin_ref, iq_ref, ik_ref, ikb_ref, iw_ref):
    acc = jnp.dot(x_ref[...], w_ref[...].astype(bf16), preferred_element_type=f32)
    cos, sin = cos_ref[...], sin_ref[...]
    lane = lax.broadcasted_iota(i32, cos.shape, 1)
    first = (lane % IDX_DIM) < (IDX_DIM // 2)

    def rope64(x):
        partner = jnp.where(first, pltpu.roll(x, 128 - IDX_DIM // 2, 1), pltpu.roll(x, IDX_DIM // 2, 1))
        return x * cos + partner * sin

    for p in range(IDX_HEADS // 2):
        r = rope64(acc[:, 128 * p:128 * (p + 1)])
        iq_ref[2 * p] = r[:, :IDX_DIM].astype(bf16)
        iq_ref[2 * p + 1] = r[:, IDX_DIM:].astype(bf16)
    slab = acc[:, IDX_HEADS * IDX_DIM:]
    r = rope64(slab)[:, :IDX_DIM]
    ik_ref[...] = r
    ikb_ref[...] = r.astype(bf16)
    iw_ref[...] = slab[:, IDX_DIM:IDX_DIM + IDX_HEADS] * (IDX_HEADS ** -0.5)


def idx_projection(h, w_idx, cos, sin):
    G, T, K = h.shape
    tm = min(T, 512)
    return pl.pallas_call(
        _idxproj_kernel,
        grid=(G, T // tm),
        in_specs=[pl.BlockSpec((None, tm, K), lambda g, t: (g, t, 0)),
                  pl.BlockSpec((K, IDX_W_COLS), lambda g, t: (0, 0)),
                  pl.BlockSpec((tm, 128), lambda g, t: (t, 0)),
                  pl.BlockSpec((tm, 128), lambda g, t: (t, 0))],
        out_specs=[pl.BlockSpec((None, IDX_HEADS, tm, IDX_DIM), lambda g, t: (g, 0, t, 0)),
                   pl.BlockSpec((None, tm, IDX_DIM), lambda g, t: (g, t, 0)),
                   pl.BlockSpec((None, tm, IDX_DIM), lambda g, t: (g, t, 0)),
                   pl.BlockSpec((None, tm, IDX_HEADS), lambda g, t: (g, t, 0))],
        out_shape=[jax.ShapeDtypeStruct((G, IDX_HEADS, T, IDX_DIM), bf16),
                   jax.ShapeDtypeStruct((G, T, IDX_DIM), f32),
                   jax.ShapeDtypeStruct((G, T, IDX_DIM), bf16),
                   jax.ShapeDtypeStruct((G, T, IDX_HEADS), f32)],
        compiler_params=_params("parallel", "parallel"), name="idx_proj",
    )(h, w_idx, cos, sin)


def _sortable(x):
    bits = pltpu.bitcast(jnp.where(x == 0.0, 0.0, x), i32)
    return bits ^ ((bits >> 31) & 0x7FFFFFFF)


def _kth_largest_key(count_ge, shape, k):
    t0 = jnp.where(count_ge(jnp.zeros(shape, i32)) >= k, 0, INT_MIN).astype(i32)

    def body(i, t):
        cand = t + jnp.left_shift(jnp.int32(1), 30 - i)
        return jnp.where(count_ge(cand) >= k, cand, t)

    return lax.fori_loop(0, 31, body, t0)


def _tie_cutoff(count_tied_below, shape, need, nbits):
    def body(i, j):
        cand = j + jnp.left_shift(jnp.int32(1), nbits - 1 - i)
        return jnp.where(count_tied_below(cand) < need, cand, j)

    return lax.fori_loop(0, nbits, body, jnp.zeros(shape, i32))


TQ = 128
TK = 512


def _attn_prompt_kernel(q_ref, k_ref, v_ref, iq_ref, iw_ref, ik_ref, o_ref, keys_ref, m_ref, l_ref, acc_ref, *, topk):
    qi = pl.program_id(1)
    nkc = qi // (TK // TQ) + 1
    iw = iw_ref[...]
    qpos = lax.broadcasted_iota(i32, (TQ, TK), 0) + qi * TQ
    lane = lax.broadcasted_iota(i32, (TQ, TK), 1)

    def score_chunk(c, carry):
        ikc = ik_ref[pl.ds(pl.multiple_of(c * TK, TK), TK), :]
        sc = jnp.zeros((TQ, TK), f32)
        for h in range(IDX_HEADS):
            s = lax.dot_general(iq_ref[h], ikc, NT_DIMS, preferred_element_type=f32)
            sc = sc + jnp.maximum(s, 0.0) * iw[:, h:h + 1]
        sc = jnp.where(lane + c * TK <= qpos, sc, NEG)
        keys_ref[c] = _sortable(sc)
        return carry

    lax.fori_loop(0, nkc, score_chunk, 0)

    def fold(x):
        return x[:, 0:128] + x[:, 128:256] + x[:, 256:384] + x[:, 384:512]

    def count(pred):
        def body(c, acc):
            return acc + fold(jnp.where(pred(c, keys_ref[c]), 1, 0).astype(i32))
        acc = lax.fori_loop(0, nkc, body, jnp.zeros((TQ, 128), i32))
        return jnp.sum(acc, axis=1, keepdims=True)

    thr = _kth_largest_key(lambda cand: count(lambda c, key: key >= cand), (TQ, 1), topk)

    n_ge = count(lambda c, key: key >= thr)

    @pl.when(jnp.max(jnp.where(n_ge != topk, 1, 0)) > 0)
    def _():
        n_gt = count(lambda c, key: key > thr)
        need = topk - n_gt
        cut = _tie_cutoff(lambda j: count(lambda c, key: (key == thr) & (lane + c * TK < j)), (TQ, 1), need, 12)

        def demote(c, carry):
            key = keys_ref[c]
            keys_ref[c] = jnp.where((key == thr) & (lane + c * TK > cut), thr - 1, key)
            return carry

        lax.fori_loop(0, nkc, demote, 0)

    m_ref[...] = jnp.full(m_ref.shape, MASKED, f32)
    l_ref[...] = jnp.zeros(l_ref.shape, f32)
    acc_ref[...] = jnp.zeros(acc_ref.shape, f32)

    def attn_chunk(c, carry):
        off = pl.multiple_of(c * TK, TK)
        sel = (keys_ref[c] >= thr) & (lane + c * TK <= qpos)
        for g in range(KV_HEADS):
            kc = k_ref[pl.ds(off, TK), HEAD_DIM * g:HEAD_DIM * (g + 1)]
            vc = v_ref[pl.ds(off, TK), HEAD_DIM * g:HEAD_DIM * (g + 1)]
            qg = q_ref[Q_PER_KV * g:Q_PER_KV * (g + 1)].reshape(Q_PER_KV * TQ, HEAD_DIM)
            s = lax.dot_general(qg, kc, NT_DIMS, preferred_element_type=f32).reshape(Q_PER_KV, TQ, TK)
            s = jnp.where(sel[None], s, MASKED)
            m_old = m_ref[g]
            m_new = jnp.maximum(m_old, jnp.max(s, axis=-1, keepdims=True))
            p = jnp.exp(s - m_new)
            alpha = jnp.exp(m_old - m_new)
            l_ref[g] = alpha * l_ref[g] + jnp.sum(p, axis=-1, keepdims=True)
            pv = jnp.dot(p.reshape(Q_PER_KV * TQ, TK).astype(bf16), vc, preferred_element_type=f32)
            acc_ref[g] = alpha * acc_ref[g] + pv.reshape(Q_PER_KV, TQ, HEAD_DIM)
            m_ref[g] = m_new
        return carry

    lax.fori_loop(0, nkc, attn_chunk, 0)

    for g in range(KV_HEADS):
        o = acc_ref[g] / l_ref[g]
        for r in range(Q_PER_KV):
            hd = Q_PER_KV * g + r
            o_ref[:, HEAD_DIM * hd:HEAD_DIM * (hd + 1)] = o[r].astype(bf16)


def attention_prompt(q, kb, vb, iq, iw, ikb):
    G, _, T, _ = q.shape
    topk = min(TOPK_MAX, T // 4)
    return pl.pallas_call(
        functools.partial(_attn_prompt_kernel, topk=topk),
        grid=(G, T // TQ),
        in_specs=[pl.BlockSpec((None, N_HEADS, TQ, HEAD_DIM), lambda g, t: (g, 0, t, 0)),
                  pl.BlockSpec((None, T, KV_COLS), lambda g, t: (g, 0, 0)),
                  pl.BlockSpec((None, T, KV_COLS), lambda g, t: (g, 0, 0)),
                  pl.BlockSpec((None, IDX_HEADS, TQ, IDX_DIM), lambda g, t: (g, 0, t, 0)),
                  pl.BlockSpec((None, TQ, IDX_HEADS), lambda g, t: (g, t, 0)),
                  pl.BlockSpec((None, T, IDX_DIM), lambda g, t: (g, 0, 0))],
        out_specs=pl.BlockSpec((None, TQ, N_HEADS * HEAD_DIM), lambda g, t: (g, t, 0)),
        out_shape=jax.ShapeDtypeStruct((G, T, N_HEADS * HEAD_DIM), bf16),
        scratch_shapes=[pltpu.VMEM((T // TK, TQ, TK), i32),
                        pltpu.VMEM((KV_HEADS, Q_PER_KV, TQ, 1), f32),
                        pltpu.VMEM((KV_HEADS, Q_PER_KV, TQ, 1), f32),
                        pltpu.VMEM((KV_HEADS, Q_PER_KV, TQ, HEAD_DIM), f32)],
        compiler_params=_params("parallel", "arbitrary"), name="attn_prompt",
    )(q, kb, vb, iq, iw, ikb)


SCORE_PAGES = 16
PAGES_PER_STEP = 8
PAGE_ROWS = PAGE_SIZE * KV_HEADS


def _sample_score_kernel(pt_ref, iq_ref, iw_ref, ikn_ref, *refs):
    page_refs = refs[:SCORE_PAGES]
    sc_ref, snew_ref = refs[SCORE_PAGES:]
    iq = iq_ref[...]
    iw = iw_ref[...]
    iqb = iq.astype(bf16)
    for p in range(SCORE_PAGES):
        s = lax.dot_general(iqb, page_refs[p][...].astype(bf16), NT_DIMS, preferred_element_type=f32)
        sc_ref[p:p + 1, :] = jnp.sum(jnp.maximum(s, 0.0) * iw, axis=0, keepdims=True)
    s_new = jnp.sum(iq * ikn_ref[...], axis=1, keepdims=True)
    s_new = jnp.sum(jnp.maximum(s_new, 0.0) * iw, axis=0, keepdims=True)
    snew_ref[...] = jnp.broadcast_to(s_new, snew_ref.shape)


def sample_scores(page_table, iq_s, iw_s, ik_new, cache_ik):
    B, n_pages = page_table.shape

    def page_spec(p):
        return pl.BlockSpec((None, PAGE_SIZE, IDX_DIM), lambda b, j, pt: (pt[b, j * SCORE_PAGES + p], 0, 0))

    grid_spec = pltpu.PrefetchScalarGridSpec(
        num_scalar_prefetch=1,
        grid=(B, n_pages // SCORE_PAGES),
        in_specs=[pl.BlockSpec((None, IDX_HEADS, IDX_DIM), lambda b, j, pt: (b, 0, 0)),
                  pl.BlockSpec((None, IDX_HEADS, 1), lambda b, j, pt: (b, 0, 0)),
                  pl.BlockSpec((None, 1, IDX_DIM), lambda b, j, pt: (b, 0, 0))]
                 + [page_spec(p) for p in range(SCORE_PAGES)],
        out_specs=[pl.BlockSpec((None, SCORE_PAGES, PAGE_SIZE), lambda b, j, pt: (b, j, 0)),
                   pl.BlockSpec((None, 1, PAGE_SIZE), lambda b, j, pt: (b, 0, 0))],
    )
    return pl.pallas_call(
        _sample_score_kernel, grid_spec=grid_spec,
        out_shape=[jax.ShapeDtypeStruct((B, n_pages, PAGE_SIZE), f32), jax.ShapeDtypeStruct((B, 1, PAGE_SIZE), f32)],
        compiler_params=_params("parallel", "arbitrary"), name="sample_scores",
    )(page_table, iq_s, iw_s, ik_new, *([cache_ik] * SCORE_PAGES))


def _sample_select_kernel(sc_ref, snew_ref, sel_ref, seln_ref, *, topk):
    B, past = sc_ref.shape
    keys = _sortable(sc_ref[...])
    key_new = _sortable(snew_ref[:, 0:1])
    idx = lax.broadcasted_iota(i32, keys.shape, 1)
    idx_new = jnp.full((B, 1), past, i32)

    def count(pred):
        new = jnp.where(pred(key_new, idx_new), 1, 0).astype(i32)
        return jnp.sum(jnp.where(pred(keys, idx), 1, 0).astype(i32), axis=1, keepdims=True) + new

    thr = _kth_largest_key(lambda cand: count(lambda k, i: k >= cand), (B, 1), topk)
    need = topk - count(lambda k, i: k > thr)
    cut = _tie_cutoff(lambda jj: count(lambda k, i: (k == thr) & (i < jj)), (B, 1), need, int(past).bit_length())
    chosen = lambda k, i: (k > thr) | ((k == thr) & (i <= cut))
    seln_ref[...] = jnp.broadcast_to(jnp.where(chosen(key_new, idx_new), 0.0, MASKED), seln_ref.shape)
    spread = (lax.broadcasted_iota(i32, (PAGE_SIZE, PAGE_ROWS), 1) // KV_HEADS
              == lax.broadcasted_iota(i32, (PAGE_SIZE, PAGE_ROWS), 0))
    spread = jnp.where(spread, 1.0, 0.0).astype(bf16)
    flags = jnp.where(chosen(keys, idx), 1.0, 0.0).astype(bf16)
    for pg in range(past // PAGE_SIZE):
        rep = jnp.dot(flags[:, PAGE_SIZE * pg:PAGE_SIZE * (pg + 1)], spread, preferred_element_type=f32)
        sel_ref[:, PAGE_ROWS * pg:PAGE_ROWS * (pg + 1)] = jnp.where(rep > 0.5, 0.0, MASKED)


def sample_select(scores, score_new):
    B, n_pages, _ = scores.shape
    past = n_pages * PAGE_SIZE
    topk = min(TOPK_MAX, (past + 1) // 4)
    sel, sel_new = pl.pallas_call(
        functools.partial(_sample_select_kernel, topk=topk),
        out_shape=[jax.ShapeDtypeStruct((B, n_pages * PAGE_ROWS), f32), jax.ShapeDtypeStruct((B, PAGE_SIZE), f32)],
        compiler_params=pltpu.CompilerParams(vmem_limit_bytes=VMEM_LIMIT), name="sample_select",
    )(scores.reshape(B, past), score_new.reshape(B, PAGE_SIZE))
    return sel.reshape(B, n_pages, PAGE_ROWS), sel_new.reshape(B, 1, PAGE_SIZE)


def _sample_attn_kernel(pt_ref, q_ref, kn_ref, vn_ref, sel_ref, seln_ref, *refs):
    k_refs = refs[:PAGES_PER_STEP]
    v_refs = refs[PAGES_PER_STEP:2 * PAGES_PER_STEP]
    o_ref, m_ref, l_ref, acc_ref = refs[2 * PAGES_PER_STEP:]
    j = pl.program_id(1)
    qb = q_ref[...]
    head_kv = lax.broadcasted_iota(i32, (N_HEADS, PAGE_ROWS), 0) // Q_PER_KV
    row_kv = lax.broadcasted_iota(i32, (N_HEADS, PAGE_ROWS), 1) % KV_HEADS
    own_rows = jnp.where(head_kv == row_kv, 0.0, MASKED)

    @pl.when(j == 0)
    def _():
        head_kv_col = lax.broadcasted_iota(i32, (N_HEADS, HEAD_DIM), 0) // Q_PER_KV
        kn = jnp.zeros((N_HEADS, HEAD_DIM), f32)
        vn = jnp.zeros((N_HEADS, HEAD_DIM), f32)
        for g in range(KV_HEADS):
            cols = slice(HEAD_DIM * g, HEAD_DIM * (g + 1))
            kn = jnp.where(head_kv_col == g, kn_ref[:, cols], kn)
            vn = jnp.where(head_kv_col == g, vn_ref[:, cols], vn)
        m_ref[...] = jnp.sum(qb.astype(f32) * kn, axis=1, keepdims=True) + seln_ref[:, 0:1]
        l_ref[...] = jnp.ones(l_ref.shape, f32)
        acc_ref[...] = vn

    s = jnp.concatenate(
        [lax.dot_general(qb, k_refs[p][...].astype(bf16), NT_DIMS, preferred_element_type=f32)
         + sel_ref[p:p + 1, :] + own_rows for p in range(PAGES_PER_STEP)], axis=1)
    m_old = m_ref[...]
    m_new = jnp.maximum(m_old, jnp.max(s, axis=1, keepdims=True))
    p_ = jnp.exp(s - m_new)
    alpha = jnp.exp(m_old - m_new)
    l_ref[...] = alpha * l_ref[...] + jnp.sum(p_, axis=1, keepdims=True)
    v = jnp.concatenate([v_refs[p][...].astype(bf16) for p in range(PAGES_PER_STEP)], axis=0)
    acc_ref[...] = alpha * acc_ref[...] + jnp.dot(p_.astype(bf16), v, preferred_element_type=f32)
    m_ref[...] = m_new

    @pl.when(j == pl.num_programs(1) - 1)
    def _():
        o_ref[...] = (acc_ref[...] / l_ref[...]).astype(bf16)


def sample_attention(page_table, q_s, k_new, v_new, sel, sel_new, cache_k, cache_v):
    B, n_pages = page_table.shape
    steps = n_pages // PAGES_PER_STEP

    def page_spec(p):
        return pl.BlockSpec((None, PAGE_ROWS, HEAD_DIM), lambda b, j, pt: (pt[b, j * PAGES_PER_STEP + p], 0, 0))

    grid_spec = pltpu.PrefetchScalarGridSpec(
        num_scalar_prefetch=1,
        grid=(B, steps),
        in_specs=[pl.BlockSpec((None, N_HEADS, HEAD_DIM), lambda b, j, pt: (b, 0, 0)),
                  pl.BlockSpec((None, 1, KV_COLS), lambda b, j, pt: (b, 0, 0)),
                  pl.BlockSpec((None, 1, KV_COLS), lambda b, j, pt: (b, 0, 0)),
                  pl.BlockSpec((None, PAGES_PER_STEP, PAGE_ROWS), lambda b, j, pt: (b, j, 0)),
                  pl.BlockSpec((None, 1, PAGE_SIZE), lambda b, j, pt: (b, 0, 0))]
                 + [page_spec(p) for p in range(PAGES_PER_STEP)] * 2,
        out_specs=pl.BlockSpec((None, N_HEADS, HEAD_DIM), lambda b, j, pt: (b, 0, 0)),
        scratch_shapes=[pltpu.VMEM((N_HEADS, 1), f32), pltpu.VMEM((N_HEADS, 1), f32),
                        pltpu.VMEM((N_HEADS, HEAD_DIM), f32)],
    )
    return pl.pallas_call(
        _sample_attn_kernel, grid_spec=grid_spec,
        out_shape=jax.ShapeDtypeStruct((B, N_HEADS, HEAD_DIM), bf16),
        compiler_params=_params("parallel", "arbitrary"), name="sample_attn",
    )(page_table, q_s, k_new, v_new, sel, sel_new, *([cache_k] * PAGES_PER_STEP), *([cache_v] * PAGES_PER_STEP))


def _glu_kernel(x_ref, wa_ref, wg_ref, ba_ref, bg_ref, o_ref):
    x = x_ref[...]
    a = jnp.dot(x, wa_ref[...].astype(bf16), preferred_element_type=f32) + ba_ref[...]
    g = jnp.dot(x, wg_ref[...].astype(bf16), preferred_element_type=f32) + bg_ref[...]
    o_ref[...] = a * jax.nn.sigmoid(g)


def glu_projection(h, w_pw1, b_pw1):
    G, T, K = h.shape
    C = w_pw1.shape[1] // 2
    tm, tn = min(T, 1024), 512
    nj = C // tn
    b2 = b_pw1.reshape(1, 2 * C)
    return pl.pallas_call(
        _glu_kernel,
        grid=(G, T // tm, nj),
        in_specs=[pl.BlockSpec((None, tm, K), lambda g, t, j: (g, t, 0)),
                  pl.BlockSpec((K, tn), lambda g, t, j: (0, j)),
                  pl.BlockSpec((K, tn), lambda g, t, j: (0, nj + j)),
                  pl.BlockSpec((1, tn), lambda g, t, j: (0, j)),
                  pl.BlockSpec((1, tn), lambda g, t, j: (0, nj + j))],
        out_specs=pl.BlockSpec((None, tm, tn), lambda g, t, j: (g, t, j)),
        out_shape=jax.ShapeDtypeStruct((G, T, C), f32),
        compiler_params=_params("parallel", "parallel", "parallel"), name="glu_proj",
    )(h, w_pw1, w_pw1, b2, b2)


def _ln_silu(y, g, b):
    mu = jnp.mean(y, axis=-1, keepdims=True)
    var = jnp.mean(jnp.square(y - mu), axis=-1, keepdims=True)
    z = (y - mu) * lax.rsqrt(var + LN_EPS) * g + b
    return z * jax.nn.sigmoid(z)


CONV_HALO = 32


CONV_COLS = 256


def _conv_prompt_kernel(u_ref, halo_ref, w_ref, b_ref, g_ref, be_ref, o_ref, ext_ref, y_ref):
    t = pl.program_id(1)
    tt, C = u_ref.shape
    halo = halo_ref[...]
    ext_ref[0:CONV_HALO, :] = jnp.where(t == 0, jnp.zeros_like(halo), halo)
    ext_ref[CONV_HALO:, :] = u_ref[...]
    base = CONV_HALO - (CONV_WIDTH - 1)
    for c in range(C // CONV_COLS):
        cols = slice(CONV_COLS * c, CONV_COLS * (c + 1))
        y = jnp.zeros((tt, CONV_COLS), f32) + b_ref[:, cols]
        for k in range(CONV_WIDTH):
            y = y + ext_ref[base + k:base + k + tt, cols] * w_ref[k:k + 1, cols]
        y_ref[:, cols] = y
    o_ref[...] = _ln_silu(y_ref[...], g_ref[...], be_ref[...]).astype(bf16)


def conv_prompt(u, w_dw, b_dw, ln_g, ln_b):
    G, T, C = u.shape
    tt = 128
    hb = tt // CONV_HALO
    row = pl.BlockSpec((1, C), lambda g, t: (0, 0))
    return pl.pallas_call(
        _conv_prompt_kernel,
        grid=(G, T // tt),
        in_specs=[pl.BlockSpec((None, tt, C), lambda g, t: (g, t, 0)),
                  pl.BlockSpec((None, CONV_HALO, C), lambda g, t: (g, jnp.maximum(t * hb - 1, 0), 0)),
                  pl.BlockSpec((CONV_WIDTH, C), lambda g, t: (0, 0)), row, row, row],
        out_specs=pl.BlockSpec((None, tt, C), lambda g, t: (g, t, 0)),
        out_shape=jax.ShapeDtypeStruct((G, T, C), bf16),
        scratch_shapes=[pltpu.VMEM((CONV_HALO + tt, C), f32), pltpu.VMEM((tt, C), f32)],
        compiler_params=_params("parallel", "parallel"), name="conv_prompt",
    )(u, u, w_dw, b_dw.reshape(1, C), ln_g.reshape(1, C), ln_b.reshape(1, C))


def _conv_sample_kernel(u_ref, hist_ref, w_ref, b_ref, g_ref, be_ref, o_ref):
    w = w_ref[...]
    y = jnp.sum(hist_ref[...] * w[None, :CONV_WIDTH - 1, :], axis=1) + u_ref[...] * w[CONV_WIDTH - 1:, :] + b_ref[...]
    o_ref[...] = _ln_silu(y, g_ref[...], be_ref[...]).astype(bf16)


def conv_sample(u, hist, w_dw, b_dw, ln_g, ln_b):
    B, C = u.shape
    tc = 512
    col = lambda r: pl.BlockSpec((r, tc), lambda j: (0, j))
    return pl.pallas_call(
        _conv_sample_kernel,
        grid=(C // tc,),
        in_specs=[col(B), pl.BlockSpec((B, CONV_WIDTH - 1, tc), lambda j: (0, 0, j)), col(CONV_WIDTH),
                  col(1), col(1), col(1)],
        out_specs=col(B),
        out_shape=jax.ShapeDtypeStruct((B, C), bf16),
        compiler_params=_params("arbitrary"), name="conv_sample",
    )(u, hist, w_dw, b_dw.reshape(1, C), ln_g.reshape(1, C), ln_b.reshape(1, C))


def _top_values(x, n):
    vals = []
    for _ in range(n):
        m = jnp.max(x, axis=0, keepdims=True)
        vals.append(m)
        x = jnp.where(x == m, -jnp.inf, x)
    return vals


NOT_TOP = float(PEER_TOPK)


def _top_ranked(x, n):
    vals = []
    rank = jnp.full(x.shape, NOT_TOP, f32)
    for r in range(n):
        m = jnp.max(x, axis=0, keepdims=True)
        hit = x == m
        vals.append(m)
        rank = jnp.where(hit, float(r), rank)
        x = jnp.where(hit, -jnp.inf, x)
    return vals, rank


def _peer_gate_kernel(q_ref, sk_ref, cnt_ref, ea_ref, rb_ref, eb_ref):
    for h in range(PEER_HEADS):
        st = []
        for p in range(2):
            col = (2 * h + p) * PEER_NKEYS
            st.append(lax.dot_general(sk_ref[h, p].astype(bf16), q_ref[:, col:col + PEER_NKEYS], NT_DIMS,
                                      preferred_element_type=f32))
        a, b = st
        va, rank_a = _top_ranked(a, PEER_TOPK)
        vbl, rank_b = _top_ranked(b, PEER_TOPK)
        row = lax.broadcasted_iota(i32, (PEER_TOPK, a.shape[1]), 0)
        vb = jnp.zeros((PEER_TOPK, a.shape[1]), f32)
        for r in range(PEER_TOPK):
            vb = jnp.where(row == r, vbl[r], vb)
        cands = [va[r] + vb for r in range(PEER_TOPK)]
        thr = _top_values(jnp.concatenate(cands, axis=0), PEER_TOPK)[-1]
        top = va[0] + vbl[0]
        z = jnp.zeros_like(top)
        cnt = jnp.zeros(a.shape, f32)
        for r in range(PEER_TOPK):
            keep = cands[r] >= thr
            z = z + jnp.sum(jnp.where(keep, jnp.exp(cands[r] - top), 0.0), axis=0, keepdims=True)
            kept = jnp.sum(jnp.where(keep, 1.0, 0.0), axis=0, keepdims=True)
            cnt = jnp.where(rank_a == float(r), kept, cnt)
        ea = jnp.exp(a - va[0]) / z
        for grp in range(PEER_NKEYS // 8):
            cnt_ref[grp, h] = cnt[8 * grp:8 * (grp + 1), :]
            ea_ref[grp, h] = ea[8 * grp:8 * (grp + 1), :]
        eb = jnp.exp(b - vbl[0])
        for s in range(PEER_NKEYS // BF16_ROWS):
            rows = slice(BF16_ROWS * s, BF16_ROWS * (s + 1))
            rb_ref[h, s] = rank_b[rows, :].astype(bf16)
            eb_ref[h, s] = eb[rows, :].astype(bf16)


def peer_gates(q, sub_keys):
    G, T, _ = q.shape
    tm = min(T, 256)
    jt = PEER_NKEYS // BF16_ROWS
    a_spec = pl.BlockSpec((None, PEER_NKEYS // 8, PEER_HEADS, 8, tm), lambda g, t: (g, 0, 0, 0, t))
    b_spec = pl.BlockSpec((None, PEER_HEADS, jt, BF16_ROWS, tm), lambda g, t: (g, 0, 0, 0, t))
    a_shape = jax.ShapeDtypeStruct((G, PEER_NKEYS // 8, PEER_HEADS, 8, T), f32)
    b_shape = jax.ShapeDtypeStruct((G, PEER_HEADS, jt, BF16_ROWS, T), bf16)
    return pl.pallas_call(
        _peer_gate_kernel,
        grid=(G, T // tm),
        in_specs=[pl.BlockSpec((None, tm, q.shape[2]), lambda g, t: (g, t, 0)),
                  pl.BlockSpec(sub_keys.shape, lambda g, t: (0, 0, 0, 0))],
        out_specs=[a_spec, a_spec, b_spec, b_spec],
        out_shape=[a_shape, a_shape, b_shape, b_shape],
        compiler_params=_params("parallel", "parallel"), name="peer_gates",
    )(q, sub_keys)


PEER_ROWS = 8
PEER_TE = PEER_ROWS * PEER_NKEYS
PEER_PASS = 4


def _gelu(x):
    return 0.5 * x * (1.0 + lax.erf(x * (2.0 ** -0.5)))


def _peer_expert_kernel(x_ref, u_ref, v_ref, cnt_ref, ea_ref, rb_ref, eb_ref, o_ref):
    e = pl.program_id(2)
    tm = x_ref.shape[0]
    jt = PEER_NKEYS // BF16_ROWS
    hid = lax.dot_general(x_ref[...], u_ref[...], NT_DIMS, preferred_element_type=f32)
    act = _gelu(hid)
    gates = [[None] * (tm // 128) for _ in range(PEER_ROWS)]
    for c in range(tm // 128):
        tok = slice(128 * c, 128 * (c + 1))
        for r0 in range(0, PEER_ROWS, PEER_PASS):
            gt = [jnp.zeros((jt, BF16_ROWS, 128), bf16) for _ in range(PEER_PASS)]
            for h in range(PEER_HEADS):
                rb = rb_ref[h, :, :, tok]
                eb = eb_ref[h, :, :, tok]
                for k in range(PEER_PASS):
                    r = r0 + k
                    cnt = jnp.broadcast_to(cnt_ref[h, r:r + 1, tok], (BF16_ROWS, 128)).astype(bf16)
                    eai = jnp.broadcast_to(ea_ref[h, r:r + 1, tok], (BF16_ROWS, 128)).astype(bf16)
                    active = jnp.minimum(jnp.maximum(cnt[None] - rb, 0.0), 1.0)
                    gt[k] = gt[k] + active * (eai[None] * eb)
            for k in range(PEER_PASS):
                gates[r0 + k][c] = gt[k].astype(f32).reshape(PEER_NKEYS, 128).T
    blocks = [jnp.concatenate(gates[r], axis=0) * act[:, PEER_NKEYS * r:PEER_NKEYS * (r + 1)]
              for r in range(PEER_ROWS)]
    w = jnp.concatenate(blocks, axis=1).astype(bf16)
    contrib = jnp.dot(w, v_ref[...], preferred_element_type=f32)

    @pl.when(e == 0)
    def _():
        o_ref[...] = contrib

    @pl.when(e > 0)
    def _():
        o_ref[...] += contrib


def peer_experts(x, gates, u_tab, v_tab, layer):
    G, T, Dm = x.shape
    cnt, ea, rb, eb = gates
    tm = min(T, 512)
    a_spec = pl.BlockSpec((None, None, PEER_HEADS, 8, tm), lambda g, t, e: (g, e, 0, 0, t))
    b_spec = pl.BlockSpec((None, PEER_HEADS, PEER_NKEYS // BF16_ROWS, BF16_ROWS, tm), lambda g, t, e: (g, 0, 0, 0, t))
    tab_spec =pl.BlockSpec((None, PEER_TE, Dm), lambda g, t, e: (layer, e, 0))
    return pl.pallas_call(
        _peer_expert_kernel,
        grid=(G, T // tm, u_tab.shape[1] // PEER_TE),
        in_specs=[pl.BlockSpec((None, tm, Dm), lambda g, t, e: (g, t, 0)), tab_spec, tab_spec,
                  a_spec, a_spec, b_spec, b_spec],
        out_specs=pl.BlockSpec((None, tm, Dm), lambda g, t, e: (g, t, 0)),
        out_shape=jax.ShapeDtypeStruct((G, T, Dm), f32),
        compiler_params=_params("parallel", "parallel", "arbitrary"), name="peer_experts",
    )(x, u_tab, v_tab, cnt, ea, rb, eb)


def peer(h, w_q, sub_keys, u_tab, v_tab, layer):
    G, T, Dm = h.shape
    Tp = -(-T // 128) * 128
    if Tp != T:
        h = jnp.pad(h, ((0, 0), (0, Tp - T), (0, 0)))
    q = token_matmul(h, w_q, bf16)
    out = peer_experts(h, peer_gates(q, sub_keys), u_tab, v_tab, layer)
    return out[:, :T] if Tp != T else out


def _trunk(y, sample, mod, norm_g, attend, conv, weights):
    (w_in, w_idx, q_gain, k_gain, w_out, rope128, rope64, w_pw1, b_pw1, w_pw2, peer_w_q, peer_sub_keys, peer_u,
     peer_v) = weights
    nm = functools.partial(normmod, mod=mod, sample=sample)
    h = nm(y, norm_g=norm_g[0, 0], shift=(0, 0), scale=(0, 1))
    q = q_projection(h, w_in, q_gain, *rope128)
    k, kb, v, vb = kv_projection(h, w_in, k_gain, *rope128)
    iq, ik, ikb, iw = idx_projection(h, w_idx, *rope64)
    o = attend(q, k, kb, v, vb, iq, ik, ikb, iw)
    d = token_matmul(o, w_out, f32)
    y, h = nm(y, delta=d, gate=(0, 2), norm_g=norm_g[0, 1], shift=(0, 3), scale=(0, 4))
    d = peer(h, peer_w_q[0], peer_sub_keys[0], peer_u, peer_v, 0)
    y, h = nm(y, delta=d, gate=(0, 5), norm_g=norm_g[1, 0], shift=(1, 0), scale=(1, 1))
    u = glu_projection(h, w_pw1, b_pw1)
    c, state = conv(u)
    d = token_matmul(c, w_pw2, f32)
    y, h = nm(y, delta=d, gate=(1, 2), norm_g=norm_g[1, 1], shift=(1, 3), scale=(1, 4))
    d = peer(h, peer_w_q[1], peer_sub_keys[1], peer_u, peer_v, 1)
    y = nm(y, delta=d, gate=(1, 5))
    return y, k, v, ik, state


def kernel(x_prompt, x_sample, cache_k, cache_v, cache_idx_k, state_conv, page_table, c_prompt, c_sample, w_ada, b_ada, norm_g, attn_w_in, attn_q_gain, attn_k_gain, attn_w_out, conv_w_pw1, conv_b_pw1, conv_w_dw, conv_b_dw, conv_ln_g, conv_ln_b, conv_w_pw2, peer_w_q, peer_sub_keys, peer_u, peer_v):
    B, T, Dm = x_prompt.shape
    Bd = x_sample.shape[0]
    n_pages = page_table.shape[1]
    past = n_pages * PAGE_SIZE
    n_phys = cache_k.shape[1]

    c_all = jnp.concatenate([c_sample, c_prompt, jnp.zeros((ADA_ROWS - Bd - B, Dm), f32)], axis=0)
    mod = ada_modulation(c_all, w_ada, b_ada)
    ng = norm_g.reshape(norm_g.shape[0], 2, 1, Dm)

    peer_u_bf, peer_v_bf = peer_u.astype(bf16), peer_v.astype(bf16)
    w_in = attn_w_in[0]
    idx0 = N_HEADS * HEAD_DIM + 2 * KV_COLS
    w_idx = jnp.concatenate([w_in[:, idx0:], jnp.zeros((Dm, IDX_W_COLS - (w_in.shape[1] - idx0)), f32)], axis=1)

    def weights(pos):
        return (w_in, w_idx, attn_q_gain[0], attn_k_gain[0], attn_w_out[0], _rope_tables(pos, HEAD_DIM),
                _rope_tables(pos, IDX_DIM), conv_w_pw1[0], conv_b_pw1[0], conv_w_pw2[0], peer_w_q, peer_sub_keys,
                peer_u_bf, peer_v_bf)

    conv_tail = (conv_w_dw[0], conv_b_dw[0], conv_ln_g[0], conv_ln_b[0])

    def attend_prompt(q, k, kb, v, vb, iq, ik, ikb, iw):
        return attention_prompt(q, kb, vb, iq, iw, ikb)

    def conv_p(u):
        return conv_prompt(u, *conv_tail), u[:, T - (CONV_WIDTH - 1):]

    def attend_sample(q, k, kb, v, vb, iq, ik, ikb, iw):
        iq_s = jnp.transpose(iq[0], (1, 0, 2)).astype(f32)
        scores, score_new = sample_scores(page_table, iq_s, iw.reshape(Bd, IDX_HEADS, 1),
                                          ik.reshape(Bd, 1, IDX_DIM), cache_idx_k[0])
        sel, sel_new = sample_select(scores, score_new)
        o = sample_attention(page_table, jnp.transpose(q[0], (1, 0, 2)), k.reshape(Bd, 1, KV_COLS),
                             v.reshape(Bd, 1, KV_COLS), sel, sel_new,
                             cache_k[0].reshape(n_phys, PAGE_ROWS, HEAD_DIM), cache_v[0].reshape(n_phys, PAGE_ROWS, HEAD_DIM))
        return o.reshape(1, Bd, N_HEADS * HEAD_DIM)

    def conv_s(u):
        hist = state_conv[0]
        c = conv_sample(u[0], hist, *conv_tail)
        return c[None], jnp.concatenate([hist[:, 1:], u[0][:, None, :]], axis=1)

    yp, kp, vp, ikp, stp = _trunk(x_prompt, False, mod, ng, attend_prompt, conv_p,
                                  weights(jnp.arange(T, dtype=i32)))
    ys, ks, vs, iks, sts = _trunk(x_sample.reshape(1, Bd, Dm), True, mod, ng, attend_sample, conv_s,
                                  weights(jnp.full((Bd,), past, i32)))
    return (yp, ys.reshape(Bd, 1, Dm),
            kp.reshape(1, B, T, KV_HEADS, HEAD_DIM), vp.reshape(1, B, T, KV_HEADS, HEAD_DIM),
            ikp.reshape(1, B, T, IDX_DIM),
            ks.reshape(1, Bd, 1, KV_HEADS, HEAD_DIM), vs.reshape(1, Bd, 1, KV_HEADS, HEAD_DIM),
            iks.reshape(1, Bd, 1, IDX_DIM),
            stp[None], sts[None])
```

```python
import functools
import math

import numpy as np
import jax
import jax.numpy as jnp
from jax import lax
from jax.experimental import pallas as pl
from jax.experimental.pallas import tpu as pltpu

f32 = jnp.float32
bf16 = jnp.bfloat16
i32 = jnp.int32

D_MODEL = 2048
N_HEADS = 16
HEAD_DIM = 128
KV_HEADS = 4
Q_PER_KV = N_HEADS // KV_HEADS
KV_COLS = KV_HEADS * HEAD_DIM
IDX_HEADS = 16
IDX_DIM = 64
TOPK_MAX = 256
ROPE_THETA = 10000.0
PAGE_SIZE = 128
CONV_WIDTH = 31
PEER_HEADS = 8
PEER_NKEYS = 128
PEER_TOPK = 16
RMS_EPS = 1e-6
LN_EPS = 1e-5
NEG = -1e30
MASKED = -1e30
INT_MIN = -(2 ** 31)
Q_SCALE = HEAD_DIM ** -0.5 * math.log2(math.e)
VMEM_LIMIT = 56 * 1024 * 1024
BF16_ROWS = 16
ADA_ROWS = 40
ADA_PROMPT_ROW = 32

NT_DIMS = (((1,), (1,)), ((), ()))


def _params(*sem):
    return pltpu.CompilerParams(dimension_semantics=sem, vmem_limit_bytes=VMEM_LIMIT)


def _ada_kernel(c_ref, w_ref, b_ref, o_ref):
    c = c_ref[...]
    a = (c * jax.nn.sigmoid(c)).astype(bf16)
    o_ref[...] = jnp.dot(a, w_ref[...].astype(bf16), preferred_element_type=f32) + b_ref[...]


def ada_modulation(c_all, w_ada, b_ada):
    depth, d, _ = w_ada.shape
    tn = 1024
    nj = d // tn
    return pl.pallas_call(
        _ada_kernel,
        grid=(depth, 6, nj),
        in_specs=[
            pl.BlockSpec((ADA_ROWS, d), lambda l, k, j: (0, 0)),
            pl.BlockSpec((None, d, tn), lambda l, k, j: (l, 0, k * nj + j)),
            pl.BlockSpec((None, 1, tn), lambda l, k, j: (l, 0, k * nj + j)),
        ],
        out_specs=pl.BlockSpec((None, None, ADA_ROWS, tn), lambda l, k, j: (l, k, 0, j)),
        out_shape=jax.ShapeDtypeStruct((depth, 6, ADA_ROWS, d), f32),
        compiler_params=_params("parallel", "parallel", "parallel"),
        name="ada",
    )(c_all, w_ada, b_ada.reshape(depth, 1, 6 * d))


def _mod_operand(mod, sample):
    return mod if sample else mod.reshape(mod.shape[:3] + (1, mod.shape[3]))


def _mod_spec(sample, layer, k):
    if sample:
        return pl.BlockSpec((None, None, 32, D_MODEL), lambda g, t, *_: (layer, k, 0, 0))
    return pl.BlockSpec((None, None, None, 1, D_MODEL), lambda g, t, *_: (layer, k, ADA_PROMPT_ROW + g, 0, 0))


def _normmod_kernel(*refs, has_delta, want_h):
    refs = list(refs)
    y = refs.pop(0)[...]
    if has_delta:
        d_ref, gate_ref = refs.pop(0), refs.pop(0)
        y = y + gate_ref[...] * d_ref[...]
    if want_h:
        g_ref, sh_ref, sc_ref = refs.pop(0), refs.pop(0), refs.pop(0)
    if has_delta:
        refs.pop(0)[...] = y
    if want_h:
        ms = jnp.mean(y * y, axis=-1, keepdims=True)
        hn = y * lax.rsqrt(ms + RMS_EPS) * g_ref[...]
        refs.pop(0)[...] = (hn * (1.0 + sc_ref[...]) + sh_ref[...]).astype(bf16)


def normmod(y, mod, sample, *, delta=None, gate=None, norm_g=None, shift=None, scale=None):
    G, T, Dm = y.shape
    tm = min(T, 256)
    tok = pl.BlockSpec((None, tm, Dm), lambda g, t: (g, t, 0))
    modop = _mod_operand(mod, sample)
    ins, specs = [y], [tok]
    if delta is not None:
        ins += [delta, modop]
        specs += [tok, _mod_spec(sample, *gate)]
    if norm_g is not None:
        ins += [norm_g, modop, modop]
        specs += [pl.BlockSpec((1, Dm), lambda g, t: (0, 0)), _mod_spec(sample, *shift), _mod_spec(sample, *scale)]
    outs, ospecs = [], []
    if delta is not None:
        outs.append(jax.ShapeDtypeStruct((G, T, Dm), f32)); ospecs.append(tok)
    if norm_g is not None:
        outs.append(jax.ShapeDtypeStruct((G, T, Dm), bf16)); ospecs.append(tok)
    res = pl.pallas_call(
        functools.partial(_normmod_kernel, has_delta=delta is not None, want_h=norm_g is not None),
        grid=(G, T // tm), in_specs=specs, out_specs=ospecs, out_shape=outs,
        compiler_params=_params("parallel", "parallel"), name="normmod",
    )(*ins)
    return res if len(res) > 1 else res[0]


def _matmul_kernel(x_ref, w_ref, o_ref):
    o_ref[...] = jnp.dot(x_ref[...], w_ref[...].astype(bf16), preferred_element_type=f32).astype(o_ref.dtype)


def token_matmul(x, w, out_dtype):
    G, T, K = x.shape
    N = w.shape[1]
    tm, tn = min(T, 1024), 512
    return pl.pallas_call(
        _matmul_kernel,
        grid=(G, T // tm, N // tn),
        in_specs=[pl.BlockSpec((None, tm, K), lambda g, t, j: (g, t, 0)),
                  pl.BlockSpec((K, tn), lambda g, t, j: (0, j))],
        out_specs=pl.BlockSpec((None, tm, tn), lambda g, t, j: (g, t, j)),
        out_shape=jax.ShapeDtypeStruct((G, T, N), out_dtype),
        compiler_params=_params("parallel", "parallel", "parallel"), name="token_matmul",
    )(x, w)


def _rope_tables(pos, dim):
    half = dim // 2
    inv = ROPE_THETA ** (-jnp.arange(half, dtype=f32) / half)
    ang = pos.astype(f32)[:, None] * inv
    cos, sin = jnp.cos(ang), jnp.sin(ang)
    reps = 128 // dim
    return jnp.tile(jnp.concatenate([cos, cos], -1), (1, reps)), jnp.tile(jnp.concatenate([-sin, sin], -1), (1, reps))


def _head_rmsnorm(x, gain):
    ms = jnp.mean(x * x, axis=-1, keepdims=True)
    return x * lax.rsqrt(ms + RMS_EPS) * gain


def _qproj_kernel(x_ref, w_ref, gain_ref, cos_ref, sin_ref, o_ref):
    acc = jnp.dot(x_ref[...], w_ref[...].astype(bf16), preferred_element_type=f32)
    cos, sin, gain = cos_ref[...], sin_ref[...], gain_ref[...]
    for h in range(4):
        xn = _head_rmsnorm(acc[:, HEAD_DIM * h:HEAD_DIM * (h + 1)], gain)
        r = xn * cos + pltpu.roll(xn, HEAD_DIM // 2, 1) * sin
        o_ref[h] = (r * Q_SCALE).astype(bf16)


def q_projection(h, w_in, q_gain, cos, sin):
    G, T, K = h.shape
    tm = min(T, 1024)
    return pl.pallas_call(
        _qproj_kernel,
        grid=(G, T // tm, 4),
        in_specs=[pl.BlockSpec((None, tm, K), lambda g, t, j: (g, t, 0)),
                  pl.BlockSpec((K, 512), lambda g, t, j: (0, j)),
                  pl.BlockSpec((1, HEAD_DIM), lambda g, t, j: (0, 0)),
                  pl.BlockSpec((tm, 128), lambda g, t, j: (t, 0)),
                  pl.BlockSpec((tm, 128), lambda g, t, j: (t, 0))],
        out_specs=pl.BlockSpec((None, 4, tm, HEAD_DIM), lambda g, t, j: (g, j, t, 0)),
        out_shape=jax.ShapeDtypeStruct((G, N_HEADS, T, HEAD_DIM), bf16),
        compiler_params=_params("parallel", "parallel", "parallel"), name="q_proj",
    )(h, w_in, q_gain.reshape(1, HEAD_DIM), cos, sin)


def _kvproj_kernel(x_ref, wk_ref, wv_ref, gain_ref, cos_ref, sin_ref, k_ref, kb_ref, v_ref, vb_ref):
    x = x_ref[...]
    kacc = jnp.dot(x, wk_ref[...].astype(bf16), preferred_element_type=f32)
    cos, sin, gain = cos_ref[...], sin_ref[...], gain_ref[...]
    for h in range(KV_HEADS):
        sl = slice(HEAD_DIM * h, HEAD_DIM * (h + 1))
        xn = _head_rmsnorm(kacc[:, sl], gain)
        r = xn * cos + pltpu.roll(xn, HEAD_DIM // 2, 1) * sin
        k_ref[:, sl] = r
        kb_ref[:, sl] = r.astype(bf16)
    v = jnp.dot(x, wv_ref[...].astype(bf16), preferred_element_type=f32)
    v_ref[...] = v
    vb_ref[...] = v.astype(bf16)


def kv_projection(h, w_in, k_gain, cos, sin):
    G, T, K = h.shape
    tm = min(T, 1024)
    kcol = (N_HEADS * HEAD_DIM) // KV_COLS
    tok = pl.BlockSpec((None, tm, KV_COLS), lambda g, t: (g, t, 0))
    return pl.pallas_call(
        _kvproj_kernel,
        grid=(G, T // tm),
        in_specs=[pl.BlockSpec((None, tm, K), lambda g, t: (g, t, 0)),
                  pl.BlockSpec((K, KV_COLS), lambda g, t: (0, kcol)),
                  pl.BlockSpec((K, KV_COLS), lambda g, t: (0, kcol + 1)),
                  pl.BlockSpec((1, HEAD_DIM), lambda g, t: (0, 0)),
                  pl.BlockSpec((tm, 128), lambda g, t: (t, 0)),
                  pl.BlockSpec((tm, 128), lambda g, t: (t, 0))],
        out_specs=[tok, tok, tok, tok],
        out_shape=[jax.ShapeDtypeStruct((G, T, KV_COLS), f32), jax.ShapeDtypeStruct((G, T, KV_COLS), bf16),
                   jax.ShapeDtypeStruct((G, T, KV_COLS), f32), jax.ShapeDtypeStruct((G, T, KV_COLS), bf16)],
        compiler_params=_params("parallel", "parallel"), name="kv_proj",
    )(h, w_in, w_in, k_gain.reshape(1, HEAD_DIM), cos, sin)


IDX_W_COLS = 1152


def _idxproj_kernel(x_ref, w_ref, cos_ref, sin_ref, iq_ref, ik_ref, ikb_ref, iw_ref):
    acc = jnp.dot(x_ref[...], w_ref[...].astype(bf16), preferred_element_type=f32)
    cos, sin = cos_ref[...], sin_ref[...]
    lane = lax.broadcasted_iota(i32, cos.shape, 1)
    first = (lane % IDX_DIM) < (IDX_DIM // 2)

    def rope64(x):
        partner = jnp.where(first, pltpu.roll(x, 128 - IDX_DIM // 2, 1), pltpu.roll(x, IDX_DIM // 2, 1))
        return x * cos + partner * sin

    for p in range(IDX_HEADS // 2):
        r = rope64(acc[:, 128 * p:128 * (p + 1)])
        iq_ref[2 * p] = r[:, :IDX_DIM].astype(bf16)
        iq_ref[2 * p + 1] = r[:, IDX_DIM:].astype(bf16)
    slab = acc[:, IDX_HEADS * IDX_DIM:]
    r = rope64(slab)[:, :IDX_DIM]
    ik_ref[...] = r
    ikb_ref[...] = r.astype(bf16)
    iw_ref[...] = slab[:, IDX_DIM:IDX_DIM + IDX_HEADS] * (IDX_HEADS ** -0.5)


def idx_projection(h, w_idx, cos, sin):
    G, T, K = h.shape
    tm = min(T, 512)
    return pl.pallas_call(
        _idxproj_kernel,
        grid=(G, T // tm),
        in_specs=[pl.BlockSpec((None, tm, K), lambda g, t: (g, t, 0)),
                  pl.BlockSpec((K, IDX_W_COLS), lambda g, t: (0, 0)),
                  pl.BlockSpec((tm, 128), lambda g, t: (t, 0)),
                  pl.BlockSpec((tm, 128), lambda g, t: (t, 0))],
        out_specs=[pl.BlockSpec((None, IDX_HEADS, tm, IDX_DIM), lambda g, t: (g, 0, t, 0)),
                   pl.BlockSpec((None, tm, IDX_DIM), lambda g, t: (g, t, 0)),
                   pl.BlockSpec((None, tm, IDX_DIM), lambda g, t: (g, t, 0)),
                   pl.BlockSpec((None, tm, IDX_HEADS), lambda g, t: (g, t, 0))],
        out_shape=[jax.ShapeDtypeStruct((G, IDX_HEADS, T, IDX_DIM), bf16),
                   jax.ShapeDtypeStruct((G, T, IDX_DIM), f32),
                   jax.ShapeDtypeStruct((G, T, IDX_DIM), bf16),
                   jax.ShapeDtypeStruct((G, T, IDX_HEADS), f32)],
        compiler_params=_params("parallel", "parallel"), name="idx_proj",
    )(h, w_idx, cos, sin)


def _sortable(x):
    bits = pltpu.bitcast(jnp.where(x == 0.0, 0.0, x), i32)
    return bits ^ ((bits >> 31) & 0x7FFFFFFF)


def _kth_largest_key(count_ge, shape, k):
    t0 = jnp.where(count_ge(jnp.zeros(shape, i32)) >= k, 0, INT_MIN).astype(i32)

    def body(i, t):
        cand = t + jnp.left_shift(jnp.int32(1), 30 - i)
        return jnp.where(count_ge(cand) >= k, cand, t)

    return lax.fori_loop(0, 31, body, t0)


def _tie_cutoff(count_tied_below, shape, need, nbits):
    def body(i, j):
        cand = j + jnp.left_shift(jnp.int32(1), nbits - 1 - i)
        return jnp.where(count_tied_below(cand) < need, cand, j)

    return lax.fori_loop(0, nbits, body, jnp.zeros(shape, i32))


TQ = 256
TK = 512


def _attn_prompt_kernel(q_ref, k_ref, v_ref, iq_ref, iw_ref, ik_ref, o_ref, keys_ref, m_ref, acc_ref, *, topk):
    qi = pl.program_id(1)
    nkc = (qi * TQ + TQ - 1) // TK + 1
    iw = iw_ref[...]
    qpos = lax.broadcasted_iota(i32, (TQ, TK), 0) + qi * TQ
    lane = lax.broadcasted_iota(i32, (TQ, TK), 1)

    def score_chunk(c, carry):
        ikc = ik_ref[pl.ds(pl.multiple_of(c * TK, TK), TK), :]
        sc = jnp.zeros((TQ, TK), f32)
        for h in range(IDX_HEADS):
            s = lax.dot_general(iq_ref[h], ikc, NT_DIMS, preferred_element_type=f32)
            sc = sc + jnp.maximum(s, 0.0) * iw[:, h:h + 1]
        sc = jnp.where(lane + c * TK <= qpos, sc, NEG)
        keys_ref[c] = _sortable(sc)
        return carry

    lax.fori_loop(0, nkc, score_chunk, 0)

    def fold(x):
        return sum(x[:, 128 * j:128 * (j + 1)] for j in range(1, TK // 128)) + x[:, 0:128]

    def count(pred):
        def body(c, acc):
            return acc + fold(jnp.where(pred(c, keys_ref[c]), 1, 0).astype(i32))
        acc = lax.fori_loop(0, nkc, body, jnp.zeros((TQ, 128), i32))
        return jnp.sum(acc, axis=1, keepdims=True)

    thr = _kth_largest_key(lambda cand: count(lambda c, key: key >= cand), (TQ, 1), topk)

    n_ge = count(lambda c, key: key >= thr)

    @pl.when(jnp.max(jnp.where(n_ge != topk, 1, 0)) > 0)
    def _():
        n_gt = count(lambda c, key: key > thr)
        need = topk - n_gt
        cut = _tie_cutoff(lambda j: count(lambda c, key: (key == thr) & (lane + c * TK < j)), (TQ, 1), need, 12)

        def demote(c, carry):
            key = keys_ref[c]
            keys_ref[c] = jnp.where((key == thr) & (lane + c * TK > cut), thr - 1, key)
            return carry

        lax.fori_loop(0, nkc, demote, 0)

    m_ref[...] = jnp.full(m_ref.shape, MASKED, f32)
    acc_ref[...] = jnp.zeros(acc_ref.shape, f32)
    ones = jnp.ones((TK, HEAD_DIM), bf16)

    def attn_chunk(c, carry):
        off = pl.multiple_of(c * TK, TK)
        sel = (keys_ref[c] >= thr) & (lane + c * TK <= qpos)
        for g in range(KV_HEADS):
            kc = k_ref[pl.ds(off, TK), HEAD_DIM * g:HEAD_DIM * (g + 1)]
            vc = jnp.concatenate([v_ref[pl.ds(off, TK), HEAD_DIM * g:HEAD_DIM * (g + 1)], ones], axis=1)
            qg = q_ref[Q_PER_KV * g:Q_PER_KV * (g + 1)].reshape(Q_PER_KV * TQ, HEAD_DIM)
            s_all = lax.dot_general(qg, kc, NT_DIMS, preferred_element_type=f32)
            ps, alphas = [], []
            for r in range(Q_PER_KV):
                hd = Q_PER_KV * g + r
                s = jnp.where(sel, s_all[TQ * r:TQ * (r + 1)], MASKED)
                m_old = m_ref[hd]
                m_new = jnp.maximum(m_old, jnp.max(s, axis=-1, keepdims=True))
                m_ref[hd] = m_new
                alphas.append(jnp.exp2(m_old - m_new))
                ps.append(jnp.exp2(s - jnp.concatenate([m_new] * (TK // 128), axis=1)).astype(bf16))
            pv = jnp.dot(jnp.concatenate(ps, axis=0), vc, preferred_element_type=f32)
            for r in range(Q_PER_KV):
                hd = Q_PER_KV * g + r
                acc_ref[hd] = jnp.concatenate([alphas[r]] * 2, axis=1) * acc_ref[hd] + pv[TQ * r:TQ * (r + 1)]
        return carry

    lax.fori_loop(0, nkc, attn_chunk, 0)

    for hd in range(N_HEADS):
        a = acc_ref[hd]
        o_ref[:, HEAD_DIM * hd:HEAD_DIM * (hd + 1)] = (a[:, :HEAD_DIM] / a[:, HEAD_DIM:]).astype(bf16)


def attention_prompt(q, kb, vb, iq, iw, ikb):
    G, _, T, _ = q.shape
    topk = min(TOPK_MAX, T // 4)
    return pl.pallas_call(
        functools.partial(_attn_prompt_kernel, topk=topk),
        grid=(G, T // TQ),
        in_specs=[pl.BlockSpec((None, N_HEADS, TQ, HEAD_DIM), lambda g, t: (g, 0, t, 0)),
                  pl.BlockSpec((None, T, KV_COLS), lambda g, t: (g, 0, 0)),
                  pl.BlockSpec((None, T, KV_COLS), lambda g, t: (g, 0, 0)),
                  pl.BlockSpec((None, IDX_HEADS, TQ, IDX_DIM), lambda g, t: (g, 0, t, 0)),
                  pl.BlockSpec((None, TQ, IDX_HEADS), lambda g, t: (g, t, 0)),
                  pl.BlockSpec((None, T, IDX_DIM), lambda g, t: (g, 0, 0))],
        out_specs=pl.BlockSpec((None, TQ, N_HEADS * HEAD_DIM), lambda g, t: (g, t, 0)),
        out_shape=jax.ShapeDtypeStruct((G, T, N_HEADS * HEAD_DIM), bf16),
        scratch_shapes=[pltpu.VMEM((T // TK, TQ, TK), i32),
                        pltpu.VMEM((N_HEADS, TQ, 128), f32),
                        pltpu.VMEM((N_HEADS, TQ, 2 * HEAD_DIM), f32)],
        compiler_params=_params("parallel", "arbitrary"), name="attn_prompt",
    )(q, kb, vb, iq, iw, ikb)


SCORE_PAGES = 16
PAGES_PER_STEP = 16
PAGE_ROWS = PAGE_SIZE * KV_HEADS


def _sample_score_kernel(pt_ref, iq_ref, iw_ref, ikn_ref, *refs):
    page_refs = refs[:SCORE_PAGES]
    sc_ref, snew_ref = refs[SCORE_PAGES:]
    iq = iq_ref[...]
    iw = iw_ref[...]
    iqb = iq.astype(bf16)
    for p in range(SCORE_PAGES):
        s = lax.dot_general(iqb, page_refs[p][...].astype(bf16), NT_DIMS, preferred_element_type=f32)
        sc_ref[p:p + 1, :] = jnp.sum(jnp.maximum(s, 0.0) * iw, axis=0, keepdims=True)
    s_new = jnp.sum(iq * ikn_ref[...], axis=1, keepdims=True)
    s_new = jnp.sum(jnp.maximum(s_new, 0.0) * iw, axis=0, keepdims=True)
    snew_ref[...] = jnp.broadcast_to(s_new, snew_ref.shape)


def sample_scores(page_table, iq_s, iw_s, ik_new, cache_ik):
    B, n_pages = page_table.shape

    def page_spec(p):
        return pl.BlockSpec((None, PAGE_SIZE, IDX_DIM), lambda b, j, pt: (pt[b, j * SCORE_PAGES + p], 0, 0))

    grid_spec = pltpu.PrefetchScalarGridSpec(
        num_scalar_prefetch=1,
        grid=(B, n_pages // SCORE_PAGES),
        in_specs=[pl.BlockSpec((None, IDX_HEADS, IDX_DIM), lambda b, j, pt: (b, 0, 0)),
                  pl.BlockSpec((None, IDX_HEADS, 1), lambda b, j, pt: (b, 0, 0)),
                  pl.BlockSpec((None, 1, IDX_DIM), lambda b, j, pt: (b, 0, 0))]
                 + [page_spec(p) for p in range(SCORE_PAGES)],
        out_specs=[pl.BlockSpec((None, SCORE_PAGES, PAGE_SIZE), lambda b, j, pt: (b, j, 0)),
                   pl.BlockSpec((None, 1, PAGE_SIZE), lambda b, j, pt: (b, 0, 0))],
    )
    return pl.pallas_call(
        _sample_score_kernel, grid_spec=grid_spec,
        out_shape=[jax.ShapeDtypeStruct((B, n_pages, PAGE_SIZE), f32), jax.ShapeDtypeStruct((B, 1, PAGE_SIZE), f32)],
        compiler_params=_params("parallel", "arbitrary"), name="sample_scores",
    )(page_table, iq_s, iw_s, ik_new, *([cache_ik] * SCORE_PAGES))


def _sample_select_kernel(sc_ref, snew_ref, sel_ref, seln_ref, *, topk):
    B, past = sc_ref.shape
    keys = _sortable(sc_ref[...])
    key_new = _sortable(snew_ref[:, 0:1])
    idx = lax.broadcasted_iota(i32, keys.shape, 1)
    idx_new = jnp.full((B, 1), past, i32)

    def count(pred):
        new = jnp.where(pred(key_new, idx_new), 1, 0).astype(i32)
        return jnp.sum(jnp.where(pred(keys, idx), 1, 0).astype(i32), axis=1, keepdims=True) + new

    thr = _kth_largest_key(lambda cand: count(lambda k, i: k >= cand), (B, 1), topk)
    need = topk - count(lambda k, i: k > thr)
    cut = _tie_cutoff(lambda jj: count(lambda k, i: (k == thr) & (i < jj)), (B, 1), need, int(past).bit_length())
    chosen = lambda k, i: (k > thr) | ((k == thr) & (i <= cut))
    seln_ref[...] = jnp.broadcast_to(jnp.where(chosen(key_new, idx_new), 0.0, MASKED), seln_ref.shape)
    spread = (lax.broadcasted_iota(i32, (PAGE_SIZE, PAGE_ROWS), 1) // KV_HEADS
              == lax.broadcasted_iota(i32, (PAGE_SIZE, PAGE_ROWS), 0))
    spread = jnp.where(spread, 1.0, 0.0).astype(bf16)
    flags = jnp.where(chosen(keys, idx), 1.0, 0.0).astype(bf16)
    for pg in range(past // PAGE_SIZE):
        rep = jnp.dot(flags[:, PAGE_SIZE * pg:PAGE_SIZE * (pg + 1)], spread, preferred_element_type=f32)
        sel_ref[:, PAGE_ROWS * pg:PAGE_ROWS * (pg + 1)] = jnp.where(rep > 0.5, 0.0, MASKED)


def sample_select(scores, score_new):
    B, n_pages, _ = scores.shape
    past = n_pages * PAGE_SIZE
    topk = min(TOPK_MAX, (past + 1) // 4)
    sel, sel_new = pl.pallas_call(
        functools.partial(_sample_select_kernel, topk=topk),
        out_shape=[jax.ShapeDtypeStruct((B, n_pages * PAGE_ROWS), f32), jax.ShapeDtypeStruct((B, PAGE_SIZE), f32)],
        compiler_params=pltpu.CompilerParams(vmem_limit_bytes=VMEM_LIMIT), name="sample_select",
    )(scores.reshape(B, past), score_new.reshape(B, PAGE_SIZE))
    return sel.reshape(B, n_pages, PAGE_ROWS), sel_new.reshape(B, 1, PAGE_SIZE)


def _sample_attn_kernel(pt_ref, q_ref, kn_ref, vn_ref, sel_ref, seln_ref, *refs):
    k_refs = refs[:PAGES_PER_STEP]
    v_refs = refs[PAGES_PER_STEP:2 * PAGES_PER_STEP]
    o_ref, m_ref, l_ref, acc_ref = refs[2 * PAGES_PER_STEP:]
    j = pl.program_id(1)
    qb = q_ref[...]
    head_kv = lax.broadcasted_iota(i32, (N_HEADS, PAGE_ROWS), 0) // Q_PER_KV
    row_kv = lax.broadcasted_iota(i32, (N_HEADS, PAGE_ROWS), 1) % KV_HEADS
    own_rows = jnp.where(head_kv == row_kv, 0.0, MASKED)

    @pl.when(j == 0)
    def _():
        head_kv_col = lax.broadcasted_iota(i32, (N_HEADS, HEAD_DIM), 0) // Q_PER_KV
        kn = jnp.zeros((N_HEADS, HEAD_DIM), f32)
        vn = jnp.zeros((N_HEADS, HEAD_DIM), f32)
        for g in range(KV_HEADS):
            cols = slice(HEAD_DIM * g, HEAD_DIM * (g + 1))
            kn = jnp.where(head_kv_col == g, kn_ref[:, cols], kn)
            vn = jnp.where(head_kv_col == g, vn_ref[:, cols], vn)
        m_ref[...] = jnp.sum(qb.astype(f32) * kn, axis=1, keepdims=True) + seln_ref[:, 0:1]
        l_ref[...] = jnp.ones(l_ref.shape, f32)
        acc_ref[...] = vn

    s = jnp.concatenate(
        [lax.dot_general(qb, k_refs[p][...].astype(bf16), NT_DIMS, preferred_element_type=f32)
         + sel_ref[p:p + 1, :] + own_rows for p in range(PAGES_PER_STEP)], axis=1)
    m_old = m_ref[...]
    m_new = jnp.maximum(m_old, jnp.max(s, axis=1, keepdims=True))
    p_ = jnp.exp2(s - m_new)
    alpha = jnp.exp2(m_old - m_new)
    l_ref[...] = alpha * l_ref[...] + jnp.sum(p_, axis=1, keepdims=True)
    v = jnp.concatenate([v_refs[p][...].astype(bf16) for p in range(PAGES_PER_STEP)], axis=0)
    acc_ref[...] = alpha * acc_ref[...] + jnp.dot(p_.astype(bf16), v, preferred_element_type=f32)
    m_ref[...] = m_new

    @pl.when(j == pl.num_programs(1) - 1)
    def _():
        o_ref[...] = (acc_ref[...] / l_ref[...]).astype(bf16)


def sample_attention(page_table, q_s, k_new, v_new, sel, sel_new, cache_k, cache_v):
    B, n_pages = page_table.shape
    steps = n_pages // PAGES_PER_STEP

    def page_spec(p):
        return pl.BlockSpec((None, PAGE_ROWS, HEAD_DIM), lambda b, j, pt: (pt[b, j * PAGES_PER_STEP + p], 0, 0))

    grid_spec = pltpu.PrefetchScalarGridSpec(
        num_scalar_prefetch=1,
        grid=(B, steps),
        in_specs=[pl.BlockSpec((None, N_HEADS, HEAD_DIM), lambda b, j, pt: (b, 0, 0)),
                  pl.BlockSpec((None, 1, KV_COLS), lambda b, j, pt: (b, 0, 0)),
                  pl.BlockSpec((None, 1, KV_COLS), lambda b, j, pt: (b, 0, 0)),
                  pl.BlockSpec((None, PAGES_PER_STEP, PAGE_ROWS), lambda b, j, pt: (b, j, 0)),
                  pl.BlockSpec((None, 1, PAGE_SIZE), lambda b, j, pt: (b, 0, 0))]
                 + [page_spec(p) for p in range(PAGES_PER_STEP)] * 2,
        out_specs=pl.BlockSpec((None, N_HEADS, HEAD_DIM), lambda b, j, pt: (b, 0, 0)),
        scratch_shapes=[pltpu.VMEM((N_HEADS, 1), f32), pltpu.VMEM((N_HEADS, 1), f32),
                        pltpu.VMEM((N_HEADS, HEAD_DIM), f32)],
    )
    return pl.pallas_call(
        _sample_attn_kernel, grid_spec=grid_spec,
        out_shape=jax.ShapeDtypeStruct((B, N_HEADS, HEAD_DIM), bf16),
        compiler_params=_params("parallel", "arbitrary"), name="sample_attn",
    )(page_table, q_s, k_new, v_new, sel, sel_new, *([cache_k] * PAGES_PER_STEP), *([cache_v] * PAGES_PER_STEP))


def _glu_kernel(x_ref, wa_ref, wg_ref, ba_ref, bg_ref, o_ref):
    x = x_ref[...]
    a = jnp.dot(x, wa_ref[...].astype(bf16), preferred_element_type=f32) + ba_ref[...]
    g = jnp.dot(x, wg_ref[...].astype(bf16), preferred_element_type=f32) + bg_ref[...]
    o_ref[...] = a * jax.nn.sigmoid(g)


def glu_projection(h, w_pw1, b_pw1):
    G, T, K = h.shape
    C = w_pw1.shape[1] // 2
    tm, tn = min(T, 1024), 512
    nj = C // tn
    b2 = b_pw1.reshape(1, 2 * C)
    return pl.pallas_call(
        _glu_kernel,
        grid=(G, T // tm, nj),
        in_specs=[pl.BlockSpec((None, tm, K), lambda g, t, j: (g, t, 0)),
                  pl.BlockSpec((K, tn), lambda g, t, j: (0, j)),
                  pl.BlockSpec((K, tn), lambda g, t, j: (0, nj + j)),
                  pl.BlockSpec((1, tn), lambda g, t, j: (0, j)),
                  pl.BlockSpec((1, tn), lambda g, t, j: (0, nj + j))],
        out_specs=pl.BlockSpec((None, tm, tn), lambda g, t, j: (g, t, j)),
        out_shape=jax.ShapeDtypeStruct((G, T, C), f32),
        compiler_params=_params("parallel", "parallel", "parallel"), name="glu_proj",
    )(h, w_pw1, w_pw1, b2, b2)


def _ln_silu(y, g, b):
    mu = jnp.mean(y, axis=-1, keepdims=True)
    var = jnp.mean(jnp.square(y - mu), axis=-1, keepdims=True)
    z = (y - mu) * lax.rsqrt(var + LN_EPS) * g + b
    return z * jax.nn.sigmoid(z)


CONV_HALO = 32


CONV_COLS = 256


def _conv_prompt_kernel(u_ref, halo_ref, w_ref, b_ref, g_ref, be_ref, o_ref, ext_ref, y_ref):
    t = pl.program_id(1)
    tt, C = u_ref.shape
    halo = halo_ref[...]
    ext_ref[0:CONV_HALO, :] = jnp.where(t == 0, jnp.zeros_like(halo), halo)
    ext_ref[CONV_HALO:, :] = u_ref[...]
    base = CONV_HALO - (CONV_WIDTH - 1)
    for c in range(C // CONV_COLS):
        cols = slice(CONV_COLS * c, CONV_COLS * (c + 1))
        y = jnp.zeros((tt, CONV_COLS), f32) + b_ref[:, cols]
        for k in range(CONV_WIDTH):
            y = y + ext_ref[base + k:base + k + tt, cols] * w_ref[k:k + 1, cols]
        y_ref[:, cols] = y
    o_ref[...] = _ln_silu(y_ref[...], g_ref[...], be_ref[...]).astype(bf16)


def conv_prompt(u, w_dw, b_dw, ln_g, ln_b):
    G, T, C = u.shape
    tt = 128
    hb = tt // CONV_HALO
    row = pl.BlockSpec((1, C), lambda g, t: (0, 0))
    return pl.pallas_call(
        _conv_prompt_kernel,
        grid=(G, T // tt),
        in_specs=[pl.BlockSpec((None, tt, C), lambda g, t: (g, t, 0)),
                  pl.BlockSpec((None, CONV_HALO, C), lambda g, t: (g, jnp.maximum(t * hb - 1, 0), 0)),
                  pl.BlockSpec((CONV_WIDTH, C), lambda g, t: (0, 0)), row, row, row],
        out_specs=pl.BlockSpec((None, tt, C), lambda g, t: (g, t, 0)),
        out_shape=jax.ShapeDtypeStruct((G, T, C), bf16),
        scratch_shapes=[pltpu.VMEM((CONV_HALO + tt, C), f32), pltpu.VMEM((tt, C), f32)],
        compiler_params=_params("parallel", "parallel"), name="conv_prompt",
    )(u, u, w_dw, b_dw.reshape(1, C), ln_g.reshape(1, C), ln_b.reshape(1, C))


def _conv_sample_kernel(u_ref, hist_ref, w_ref, b_ref, g_ref, be_ref, o_ref):
    w = w_ref[...]
    y = jnp.sum(hist_ref[...] * w[None, :CONV_WIDTH - 1, :], axis=1) + u_ref[...] * w[CONV_WIDTH - 1:, :] + b_ref[...]
    o_ref[...] = _ln_silu(y, g_ref[...], be_ref[...]).astype(bf16)


def conv_sample(u, hist, w_dw, b_dw, ln_g, ln_b):
    B, C = u.shape
    tc = 512
    col = lambda r: pl.BlockSpec((r, tc), lambda j: (0, j))
    return pl.pallas_call(
        _conv_sample_kernel,
        grid=(C // tc,),
        in_specs=[col(B), pl.BlockSpec((B, CONV_WIDTH - 1, tc), lambda j: (0, 0, j)), col(CONV_WIDTH),
                  col(1), col(1), col(1)],
        out_specs=col(B),
        out_shape=jax.ShapeDtypeStruct((B, C), bf16),
        compiler_params=_params("arbitrary"), name="conv_sample",
    )(u, hist, w_dw, b_dw.reshape(1, C), ln_g.reshape(1, C), ln_b.reshape(1, C))


def _top_values(x, n):
    vals = []
    for _ in range(n):
        m = jnp.max(x, axis=0, keepdims=True)
        vals.append(m)
        x = jnp.where(x == m, -jnp.inf, x)
    return vals


NOT_TOP = float(PEER_TOPK)


def _top_ranked(x, n):
    vals = []
    rank = jnp.full(x.shape, NOT_TOP, f32)
    for r in range(n):
        m = jnp.max(x, axis=0, keepdims=True)
        hit = x == m
        vals.append(m)
        rank = jnp.where(hit, float(r), rank)
        x = jnp.where(hit, -jnp.inf, x)
    return vals, rank


def _peer_gate_kernel(q_ref, sk_ref, cnt_ref, ea_ref, rb_ref, eb_ref):
    for h in range(PEER_HEADS):
        st = []
        for p in range(2):
            col = (2 * h + p) * PEER_NKEYS
            st.append(lax.dot_general(sk_ref[h, p].astype(bf16), q_ref[:, col:col + PEER_NKEYS], NT_DIMS,
                                      preferred_element_type=f32))
        a, b = st
        va, rank_a = _top_ranked(a, PEER_TOPK)
        vbl, rank_b = _top_ranked(b, PEER_TOPK)
        row = lax.broadcasted_iota(i32, (PEER_TOPK, a.shape[1]), 0)
        vb = jnp.zeros((PEER_TOPK, a.shape[1]), f32)
        for r in range(PEER_TOPK):
            vb = jnp.where(row == r, vbl[r], vb)
        cands = [va[r] + vb for r in range(PEER_TOPK)]
        thr = _top_values(jnp.concatenate(cands, axis=0), PEER_TOPK)[-1]
        top = va[0] + vbl[0]
        z = jnp.zeros_like(top)
        cnt = jnp.zeros(a.shape, f32)
        for r in range(PEER_TOPK):
            keep = cands[r] >= thr
            z = z + jnp.sum(jnp.where(keep, jnp.exp(cands[r] - top), 0.0), axis=0, keepdims=True)
            kept = jnp.sum(jnp.where(keep, 1.0, 0.0), axis=0, keepdims=True)
            cnt = jnp.where(rank_a == float(r), kept, cnt)
        ea = jnp.exp(a - va[0]) / z
        for grp in range(PEER_NKEYS // 8):
            cnt_ref[grp, h] = cnt[8 * grp:8 * (grp + 1), :]
            ea_ref[grp, h] = ea[8 * grp:8 * (grp + 1), :]
        eb = jnp.exp(b - vbl[0])
        for s in range(PEER_NKEYS // BF16_ROWS):
            rows = slice(BF16_ROWS * s, BF16_ROWS * (s + 1))
            rb_ref[h, s] = rank_b[rows, :].astype(bf16)
            eb_ref[h, s] = eb[rows, :].astype(bf16)


def peer_gates(q, sub_keys):
    G, T, _ = q.shape
    tm = min(T, 256)
    jt = PEER_NKEYS // BF16_ROWS
    a_spec = pl.BlockSpec((None, PEER_NKEYS // 8, PEER_HEADS, 8, tm), lambda g, t: (g, 0, 0, 0, t))
    b_spec = pl.BlockSpec((None, PEER_HEADS, jt, BF16_ROWS, tm), lambda g, t: (g, 0, 0, 0, t))
    a_shape = jax.ShapeDtypeStruct((G, PEER_NKEYS // 8, PEER_HEADS, 8, T), f32)
    b_shape = jax.ShapeDtypeStruct((G, PEER_HEADS, jt, BF16_ROWS, T), bf16)
    return pl.pallas_call(
        _peer_gate_kernel,
        grid=(G, T // tm),
        in_specs=[pl.BlockSpec((None, tm, q.shape[2]), lambda g, t: (g, t, 0)),
                  pl.BlockSpec(sub_keys.shape, lambda g, t: (0, 0, 0, 0))],
        out_specs=[a_spec, a_spec, b_spec, b_spec],
        out_shape=[a_shape, a_shape, b_shape, b_shape],
        compiler_params=_params("parallel", "parallel"), name="peer_gates",
    )(q, sub_keys)


PEER_ROWS = 8
PEER_TE = PEER_ROWS * PEER_NKEYS
PEER_PASS = 4


def _gelu(x):
    return 0.5 * x * (1.0 + lax.erf(x * (2.0 ** -0.5)))


def _peer_expert_kernel(x_ref, u_ref, v_ref, cnt_ref, ea_ref, rb_ref, eb_ref, o_ref):
    e = pl.program_id(2)
    tm = x_ref.shape[0]
    jt = PEER_NKEYS // BF16_ROWS
    hid = lax.dot_general(x_ref[...], u_ref[...], NT_DIMS, preferred_element_type=f32)
    act = _gelu(hid)
    gates = [[None] * (tm // 128) for _ in range(PEER_ROWS)]
    for c in range(tm // 128):
        tok = slice(128 * c, 128 * (c + 1))
        for r0 in range(0, PEER_ROWS, PEER_PASS):
            gt = [jnp.zeros((jt, BF16_ROWS, 128), bf16) for _ in range(PEER_PASS)]
            for h in range(PEER_HEADS):
                rb = rb_ref[h, :, :, tok]
                eb = eb_ref[h, :, :, tok]
                for k in range(PEER_PASS):
                    r = r0 + k
                    cnt = jnp.broadcast_to(cnt_ref[h, r:r + 1, tok], (BF16_ROWS, 128)).astype(bf16)
                    eai = jnp.broadcast_to(ea_ref[h, r:r + 1, tok], (BF16_ROWS, 128)).astype(bf16)
                    active = jnp.minimum(jnp.maximum(cnt[None] - rb, 0.0), 1.0)
                    gt[k] = gt[k] + active * (eai[None] * eb)
            for k in range(PEER_PASS):
                gates[r0 + k][c] = gt[k].astype(f32).reshape(PEER_NKEYS, 128).T
    blocks = [jnp.concatenate(gates[r], axis=0) * act[:, PEER_NKEYS * r:PEER_NKEYS * (r + 1)]
              for r in range(PEER_ROWS)]
    w = jnp.concatenate(blocks, axis=1).astype(bf16)
    contrib = jnp.dot(w, v_ref[...], preferred_element_type=f32)

    @pl.when(e == 0)
    def _():
        o_ref[...] = contrib

    @pl.when(e > 0)
    def _():
        o_ref[...] += contrib


def peer_experts(x, gates, u_tab, v_tab, layer):
    G, T, Dm = x.shape
    cnt, ea, rb, eb = gates
    tm = min(T, 512)
    a_spec = pl.BlockSpec((None, None, PEER_HEADS, 8, tm), lambda g, t, e: (g, e, 0, 0, t))
    b_spec = pl.BlockSpec((None, PEER_HEADS, PEER_NKEYS // BF16_ROWS, BF16_ROWS, tm), lambda g, t, e: (g, 0, 0, 0, t))
    tab_spec =pl.BlockSpec((None, PEER_TE, Dm), lambda g, t, e: (layer, e, 0))
    return pl.pallas_call(
        _peer_expert_kernel,
        grid=(G, T // tm, u_tab.shape[1] // PEER_TE),
        in_specs=[pl.BlockSpec((None, tm, Dm), lambda g, t, e: (g, t, 0)), tab_spec, tab_spec,
                  a_spec, a_spec, b_spec, b_spec],
        out_specs=pl.BlockSpec((None, tm, Dm), lambda g, t, e: (g, t, 0)),
        out_shape=jax.ShapeDtypeStruct((G, T, Dm), f32),
        compiler_params=_params("parallel", "parallel", "arbitrary"), name="peer_experts",
    )(x, u_tab, v_tab, cnt, ea, rb, eb)


def peer(h, w_q, sub_keys, u_tab, v_tab, layer):
    G, T, Dm = h.shape
    Tp = -(-T // 128) * 128
    if Tp != T:
        h = jnp.pad(h, ((0, 0), (0, Tp - T), (0, 0)))
    q = token_matmul(h, w_q, bf16)
    out = peer_experts(h, peer_gates(q, sub_keys), u_tab, v_tab, layer)
    return out[:, :T] if Tp != T else out


def _trunk(y, sample, mod, norm_g, attend, conv, weights):
    (w_in, w_idx, q_gain, k_gain, w_out, rope128, rope64, w_pw1, b_pw1, w_pw2, peer_w_q, peer_sub_keys, peer_u,
     peer_v) = weights
    nm = functools.partial(normmod, mod=mod, sample=sample)
    h = nm(y, norm_g=norm_g[0, 0], shift=(0, 0), scale=(0, 1))
    q = q_projection(h, w_in, q_gain, *rope128)
    k, kb, v, vb = kv_projection(h, w_in, k_gain, *rope128)
    iq, ik, ikb, iw = idx_projection(h, w_idx, *rope64)
    o = attend(q, k, kb, v, vb, iq, ik, ikb, iw)
    d = token_matmul(o, w_out, f32)
    y, h = nm(y, delta=d, gate=(0, 2), norm_g=norm_g[0, 1], shift=(0, 3), scale=(0, 4))
    d = peer(h, peer_w_q[0], peer_sub_keys[0], peer_u, peer_v, 0)
    y, h = nm(y, delta=d, gate=(0, 5), norm_g=norm_g[1, 0], shift=(1, 0), scale=(1, 1))
    u = glu_projection(h, w_pw1, b_pw1)
    c, state = conv(u)
    d = token_matmul(c, w_pw2, f32)
    y, h = nm(y, delta=d, gate=(1, 2), norm_g=norm_g[1, 1], shift=(1, 3), scale=(1, 4))
    d = peer(h, peer_w_q[1], peer_sub_keys[1], peer_u, peer_v, 1)
    y = nm(y, delta=d, gate=(1, 5))
    return y, k, v, ik, state


def kernel(x_prompt, x_sample, cache_k, cache_v, cache_idx_k, state_conv, page_table, c_prompt, c_sample, w_ada, b_ada, norm_g, attn_w_in, attn_q_gain, attn_k_gain, attn_w_out, conv_w_pw1, conv_b_pw1, conv_w_dw, conv_b_dw, conv_ln_g, conv_ln_b, conv_w_pw2, peer_w_q, peer_sub_keys, peer_u, peer_v):
    B, T, Dm = x_prompt.shape
    Bd = x_sample.shape[0]
    n_pages = page_table.shape[1]
    past = n_pages * PAGE_SIZE
    n_phys = cache_k.shape[1]

    c_all = jnp.concatenate([c_sample, c_prompt, jnp.zeros((ADA_ROWS - Bd - B, Dm), f32)], axis=0)
    mod = ada_modulation(c_all, w_ada, b_ada)
    ng = norm_g.reshape(norm_g.shape[0], 2, 1, Dm)

    peer_u_bf, peer_v_bf = peer_u.astype(bf16), peer_v.astype(bf16)
    w_in = attn_w_in[0]
    idx0 = N_HEADS * HEAD_DIM + 2 * KV_COLS
    w_idx = jnp.concatenate([w_in[:, idx0:], jnp.zeros((Dm, IDX_W_COLS - (w_in.shape[1] - idx0)), f32)], axis=1)

    def weights(pos):
        return (w_in, w_idx, attn_q_gain[0], attn_k_gain[0], attn_w_out[0], _rope_tables(pos, HEAD_DIM),
                _rope_tables(pos, IDX_DIM), conv_w_pw1[0], conv_b_pw1[0], conv_w_pw2[0], peer_w_q, peer_sub_keys,
                peer_u_bf, peer_v_bf)

    conv_tail = (conv_w_dw[0], conv_b_dw[0], conv_ln_g[0], conv_ln_b[0])

    def attend_prompt(q, k, kb, v, vb, iq, ik, ikb, iw):
        return attention_prompt(q, kb, vb, iq, iw, ikb)

    def conv_p(u):
        return conv_prompt(u, *conv_tail), u[:, T - (CONV_WIDTH - 1):]

    def attend_sample(q, k, kb, v, vb, iq, ik, ikb, iw):
        iq_s = jnp.transpose(iq[0], (1, 0, 2)).astype(f32)
        scores, score_new = sample_scores(page_table, iq_s, iw.reshape(Bd, IDX_HEADS, 1),
                                          ik.reshape(Bd, 1, IDX_DIM), cache_idx_k[0])
        sel, sel_new = sample_select(scores, score_new)
        o = sample_attention(page_table, jnp.transpose(q[0], (1, 0, 2)), k.reshape(Bd, 1, KV_COLS),
                             v.reshape(Bd, 1, KV_COLS), sel, sel_new,
                             cache_k[0].reshape(n_phys, PAGE_ROWS, HEAD_DIM), cache_v[0].reshape(n_phys, PAGE_ROWS, HEAD_DIM))
        return o.reshape(1, Bd, N_HEADS * HEAD_DIM)

    def conv_s(u):
        hist = state_conv[0]
        c = conv_sample(u[0], hist, *conv_tail)
        return c[None], jnp.concatenate([hist[:, 1:], u[0][:, None, :]], axis=1)

    yp, kp, vp, ikp, stp = _trunk(x_prompt, False, mod, ng, attend_prompt, conv_p,
                                  weights(jnp.arange(T, dtype=i32)))
    ys, ks, vs, iks, sts = _trunk(x_sample.reshape(1, Bd, Dm), True, mod, ng, attend_sample, conv_s,
                                  weights(jnp.full((Bd,), past, i32)))
    return (yp, ys.reshape(Bd, 1, Dm),
            kp.reshape(1, B, T, KV_HEADS, HEAD_DIM), vp.reshape(1, B, T, KV_HEADS, HEAD_DIM),
            ikp.reshape(1, B, T, IDX_DIM),
            ks.reshape(1, Bd, 1, KV_HEADS, HEAD_DIM), vs.reshape(1, Bd, 1, KV_HEADS, HEAD_DIM),
            iks.reshape(1, Bd, 1, IDX_DIM),
            stp[None], sts[None])
```

```python
import functools
import math

import numpy as np
import jax
import jax.numpy as jnp
from jax import lax
from jax.experimental import pallas as pl
from jax.experimental.pallas import tpu as pltpu

f32 = jnp.float32
bf16 = jnp.bfloat16
i32 = jnp.int32

D_MODEL = 2048
N_HEADS = 16
HEAD_DIM = 128
KV_HEADS = 4
Q_PER_KV = N_HEADS // KV_HEADS
KV_COLS = KV_HEADS * HEAD_DIM
IDX_HEADS = 16
IDX_DIM = 64
TOPK_MAX = 256
ROPE_THETA = 10000.0
PAGE_SIZE = 128
CONV_WIDTH = 31
PEER_HEADS = 8
PEER_NKEYS = 128
PEER_TOPK = 16
RMS_EPS = 1e-6
LN_EPS = 1e-5
NEG = -1e30
MASKED = -1e30
INT_MIN = -(2 ** 31)
Q_SCALE = HEAD_DIM ** -0.5 * math.log2(math.e)
VMEM_LIMIT = 56 * 1024 * 1024
BF16_ROWS = 16
ADA_ROWS = 40
ADA_PROMPT_ROW = 32

NT_DIMS = (((1,), (1,)), ((), ()))


def _params(*sem):
    return pltpu.CompilerParams(dimension_semantics=sem, vmem_limit_bytes=VMEM_LIMIT)


def _ada_kernel(c_ref, w_ref, b_ref, o_ref):
    c = c_ref[...]
    a = (c * jax.nn.sigmoid(c)).astype(bf16)
    o_ref[...] = jnp.dot(a, w_ref[...].astype(bf16), preferred_element_type=f32) + b_ref[...]


def ada_modulation(c_all, w_ada, b_ada):
    depth, d, _ = w_ada.shape
    tn = 1024
    nj = d // tn
    return pl.pallas_call(
        _ada_kernel,
        grid=(depth, 6, nj),
        in_specs=[
            pl.BlockSpec((ADA_ROWS, d), lambda l, k, j: (0, 0)),
            pl.BlockSpec((None, d, tn), lambda l, k, j: (l, 0, k * nj + j)),
            pl.BlockSpec((None, 1, tn), lambda l, k, j: (l, 0, k * nj + j)),
        ],
        out_specs=pl.BlockSpec((None, None, ADA_ROWS, tn), lambda l, k, j: (l, k, 0, j)),
        out_shape=jax.ShapeDtypeStruct((depth, 6, ADA_ROWS, d), f32),
        compiler_params=_params("parallel", "parallel", "parallel"),
        name="ada",
    )(c_all, w_ada, b_ada.reshape(depth, 1, 6 * d))


def _mod_operand(mod, sample):
    return mod if sample else mod.reshape(mod.shape[:3] + (1, mod.shape[3]))


def _mod_spec(sample, layer, k):
    if sample:
        return pl.BlockSpec((None, None, 32, D_MODEL), lambda g, t, *_: (layer, k, 0, 0))
    return pl.BlockSpec((None, None, None, 1, D_MODEL), lambda g, t, *_: (layer, k, ADA_PROMPT_ROW + g, 0, 0))


def _normmod_kernel(*refs, has_delta, want_h):
    refs = list(refs)
    y = refs.pop(0)[...]
    if has_delta:
        d_ref, gate_ref = refs.pop(0), refs.pop(0)
        y = y + gate_ref[...] * d_ref[...]
    if want_h:
        g_ref, sh_ref, sc_ref = refs.pop(0), refs.pop(0), refs.pop(0)
    if has_delta:
        refs.pop(0)[...] = y
    if want_h:
        ms = jnp.mean(y * y, axis=-1, keepdims=True)
        hn = y * lax.rsqrt(ms + RMS_EPS) * g_ref[...]
        refs.pop(0)[...] = (hn * (1.0 + sc_ref[...]) + sh_ref[...]).astype(bf16)


def normmod(y, mod, sample, *, delta=None, gate=None, norm_g=None, shift=None, scale=None):
    G, T, Dm = y.shape
    tm = min(T, 256)
    tok = pl.BlockSpec((None, tm, Dm), lambda g, t: (g, t, 0))
    modop = _mod_operand(mod, sample)
    ins, specs = [y], [tok]
    if delta is not None:
        ins += [delta, modop]
        specs += [tok, _mod_spec(sample, *gate)]
    if norm_g is not None:
        ins += [norm_g, modop, modop]
        specs += [pl.BlockSpec((1, Dm), lambda g, t: (0, 0)), _mod_spec(sample, *shift), _mod_spec(sample, *scale)]
    outs, ospecs = [], []
    if delta is not None:
        outs.append(jax.ShapeDtypeStruct((G, T, Dm), f32)); ospecs.append(tok)
    if norm_g is not None:
        outs.append(jax.ShapeDtypeStruct((G, T, Dm), bf16)); ospecs.append(tok)
    res = pl.pallas_call(
        functools.partial(_normmod_kernel, has_delta=delta is not None, want_h=norm_g is not None),
        grid=(G, T // tm), in_specs=specs, out_specs=ospecs, out_shape=outs,
        compiler_params=_params("parallel", "parallel"), name="normmod",
    )(*ins)
    return res if len(res) > 1 else res[0]


def _matmul_kernel(x_ref, w_ref, o_ref):
    o_ref[...] = jnp.dot(x_ref[...], w_ref[...].astype(bf16), preferred_element_type=f32).astype(o_ref.dtype)


def token_matmul(x, w, out_dtype):
    G, T, K = x.shape
    N = w.shape[1]
    tm, tn = min(T, 1024), 512
    return pl.pallas_call(
        _matmul_kernel,
        grid=(G, T // tm, N // tn),
        in_specs=[pl.BlockSpec((None, tm, K), lambda g, t, j: (g, t, 0)),
                  pl.BlockSpec((K, tn), lambda g, t, j: (0, j))],
        out_specs=pl.BlockSpec((None, tm, tn), lambda g, t, j: (g, t, j)),
        out_shape=jax.ShapeDtypeStruct((G, T, N), out_dtype),
        compiler_params=_params("parallel", "parallel", "parallel"), name="token_matmul",
    )(x, w)


def _rope_tables(pos, dim):
    half = dim // 2
    inv = ROPE_THETA ** (-jnp.arange(half, dtype=f32) / half)
    ang = pos.astype(f32)[:, None] * inv
    cos, sin = jnp.cos(ang), jnp.sin(ang)
    reps = 128 // dim
    return jnp.tile(jnp.concatenate([cos, cos], -1), (1, reps)), jnp.tile(jnp.concatenate([-sin, sin], -1), (1, reps))


def _head_rmsnorm(x, gain):
    ms = jnp.mean(x * x, axis=-1, keepdims=True)
    return x * lax.rsqrt(ms + RMS_EPS) * gain


def _qproj_kernel(x_ref, w_ref, gain_ref, cos_ref, sin_ref, o_ref):
    acc = jnp.dot(x_ref[...], w_ref[...].astype(bf16), preferred_element_type=f32)
    cos, sin, gain = cos_ref[...], sin_ref[...], gain_ref[...]
    for h in range(4):
        xn = _head_rmsnorm(acc[:, HEAD_DIM * h:HEAD_DIM * (h + 1)], gain)
        r = xn * cos + pltpu.roll(xn, HEAD_DIM // 2, 1) * sin
        o_ref[h] = (r * Q_SCALE).astype(bf16)


def q_projection(h, w_in, q_gain, cos, sin):
    G, T, K = h.shape
    tm = min(T, 1024)
    return pl.pallas_call(
        _qproj_kernel,
        grid=(G, T // tm, 4),
        in_specs=[pl.BlockSpec((None, tm, K), lambda g, t, j: (g, t, 0)),
                  pl.BlockSpec((K, 512), lambda g, t, j: (0, j)),
                  pl.BlockSpec((1, HEAD_DIM), lambda g, t, j: (0, 0)),
                  pl.BlockSpec((tm, 128), lambda g, t, j: (t, 0)),
                  pl.BlockSpec((tm, 128), lambda g, t, j: (t, 0))],
        out_specs=pl.BlockSpec((None, 4, tm, HEAD_DIM), lambda g, t, j: (g, j, t, 0)),
        out_shape=jax.ShapeDtypeStruct((G, N_HEADS, T, HEAD_DIM), bf16),
        compiler_params=_params("parallel", "parallel", "parallel"), name="q_proj",
    )(h, w_in, q_gain.reshape(1, HEAD_DIM), cos, sin)


def _kvproj_kernel(x_ref, wk_ref, wv_ref, gain_ref, cos_ref, sin_ref, k_ref, kb_ref, v_ref, vb_ref):
    x = x_ref[...]
    kacc = jnp.dot(x, wk_ref[...].astype(bf16), preferred_element_type=f32)
    cos, sin, gain = cos_ref[...], sin_ref[...], gain_ref[...]
    for h in range(KV_HEADS):
        sl = slice(HEAD_DIM * h, HEAD_DIM * (h + 1))
        xn = _head_rmsnorm(kacc[:, sl], gain)
        r = xn * cos + pltpu.roll(xn, HEAD_DIM // 2, 1) * sin
        k_ref[:, sl] = r
        kb_ref[:, sl] = r.astype(bf16)
    v = jnp.dot(x, wv_ref[...].astype(bf16), preferred_element_type=f32)
    v_ref[...] = v
    vb_ref[...] = v.astype(bf16)


def kv_projection(h, w_in, k_gain, cos, sin):
    G, T, K = h.shape
    tm = min(T, 1024)
    kcol = (N_HEADS * HEAD_DIM) // KV_COLS
    tok = pl.BlockSpec((None, tm, KV_COLS), lambda g, t: (g, t, 0))
    return pl.pallas_call(
        _kvproj_kernel,
        grid=(G, T // tm),
        in_specs=[pl.BlockSpec((None, tm, K), lambda g, t: (g, t, 0)),
                  pl.BlockSpec((K, KV_COLS), lambda g, t: (0, kcol)),
                  pl.BlockSpec((K, KV_COLS), lambda g, t: (0, kcol + 1)),
                  pl.BlockSpec((1, HEAD_DIM), lambda g, t: (0, 0)),
                  pl.BlockSpec((tm, 128), lambda g, t: (t, 0)),
                  pl.BlockSpec((tm, 128), lambda g, t: (t, 0))],
        out_specs=[tok, tok, tok, tok],
        out_shape=[jax.ShapeDtypeStruct((G, T, KV_COLS), f32), jax.ShapeDtypeStruct((G, T, KV_COLS), bf16),
                   jax.ShapeDtypeStruct((G, T, KV_COLS), f32), jax.ShapeDtypeStruct((G, T, KV_COLS), bf16)],
        compiler_params=_params("parallel", "parallel"), name="kv_proj",
    )(h, w_in, w_in, k_gain.reshape(1, HEAD_DIM), cos, sin)


IDX_W_COLS = 1152


def _idxproj_kernel(x_ref, w_ref, cos_ref, sin_ref, iq_ref, ik_ref, ikb_ref, iw_ref):
    acc = jnp.dot(x_ref[...], w_ref[...].astype(bf16), preferred_element_type=f32)
    cos, sin = cos_ref[...], sin_ref[...]
    lane = lax.broadcasted_iota(i32, cos.shape, 1)
    first = (lane % IDX_DIM) < (IDX_DIM // 2)

    def rope64(x):
        partner = jnp.where(first, pltpu.roll(x, 128 - IDX_DIM // 2, 1), pltpu.roll(x, IDX_DIM // 2, 1))
        return x * cos + partner * sin

    for p in range(IDX_HEADS // 2):
        r = rope64(acc[:, 128 * p:128 * (p + 1)])
        iq_ref[2 * p] = r[:, :IDX_DIM].astype(bf16)
        iq_ref[2 * p + 1] = r[:, IDX_DIM:].astype(bf16)
    slab = acc[:, IDX_HEADS * IDX_DIM:]
    r = rope64(slab)[:, :IDX_DIM]
    ik_ref[...] = r
    ikb_ref[...] = r.astype(bf16)
    iw_ref[...] = slab[:, IDX_DIM:IDX_DIM + IDX_HEADS] * (IDX_HEADS ** -0.5)


def idx_projection(h, w_idx, cos, sin):
    G, T, K = h.shape
    tm = min(T, 512)
    return pl.pallas_call(
        _idxproj_kernel,
        grid=(G, T // tm),
        in_specs=[pl.BlockSpec((None, tm, K), lambda g, t: (g, t, 0)),
                  pl.BlockSpec((K, IDX_W_COLS), lambda g, t: (0, 0)),
                  pl.BlockSpec((tm, 128), lambda g, t: (t, 0)),
                  pl.BlockSpec((tm, 128), lambda g, t: (t, 0))],
        out_specs=[pl.BlockSpec((None, IDX_HEADS, tm, IDX_DIM), lambda g, t: (g, 0, t, 0)),
                   pl.BlockSpec((None, tm, IDX_DIM), lambda g, t: (g, t, 0)),
                   pl.BlockSpec((None, tm, IDX_DIM), lambda g, t: (g, t, 0)),
                   pl.BlockSpec((None, tm, IDX_HEADS), lambda g, t: (g, t, 0))],
        out_shape=[jax.ShapeDtypeStruct((G, IDX_HEADS, T, IDX_DIM), bf16),
                   jax.ShapeDtypeStruct((G, T, IDX_DIM), f32),
                   jax.ShapeDtypeStruct((G, T, IDX_DIM), bf16),
                   jax.ShapeDtypeStruct((G, T, IDX_HEADS), f32)],
        compiler_params=_params("parallel", "parallel"), name="idx_proj",
    )(h, w_idx, cos, sin)


def _sortable(x):
    bits = pltpu.bitcast(jnp.where(x == 0.0, 0.0, x), i32)
    return bits ^ ((bits >> 31) & 0x7FFFFFFF)


def _kth_largest_key(count_ge, shape, k):
    t0 = jnp.where(count_ge(jnp.zeros(shape, i32)) >= k, 0, INT_MIN).astype(i32)

    def body(i, t):
        cand = t + jnp.left_shift(jnp.int32(1), 30 - i)
        return jnp.where(count_ge(cand) >= k, cand, t)

    return lax.fori_loop(0, 31, body, t0)


def _tie_cutoff(count_tied_below, shape, need, nbits):
    def body(i, j):
        cand = j + jnp.left_shift(jnp.int32(1), nbits - 1 - i)
        return jnp.where(count_tied_below(cand) < need, cand, j)

    return lax.fori_loop(0, nbits, body, jnp.zeros(shape, i32))


TQ = 256
TK = 512


def _attn_prompt_kernel(q_ref, k_ref, v_ref, iq_ref, iw_ref, ik_ref, o_ref, keys_ref, m_ref, acc_ref, *, topk):
    qi = pl.program_id(1)
    nkc = (qi * TQ + TQ - 1) // TK + 1
    iw = iw_ref[...]
    qpos = lax.broadcasted_iota(i32, (TQ, TK), 0) + qi * TQ
    lane = lax.broadcasted_iota(i32, (TQ, TK), 1)

    def score_chunk(c, carry):
        ikc = ik_ref[pl.ds(pl.multiple_of(c * TK, TK), TK), :]
        sc = jnp.zeros((TQ, TK), f32)
        for h in range(IDX_HEADS):
            s = lax.dot_general(iq_ref[h], ikc, NT_DIMS, preferred_element_type=f32)
            sc = sc + jnp.maximum(s, 0.0) * iw[:, h:h + 1]
        sc = jnp.where(lane + c * TK <= qpos, sc, NEG)
        keys_ref[c] = _sortable(sc)
        return carry

    lax.fori_loop(0, nkc, score_chunk, 0)

    def fold(x):
        return sum(x[:, 128 * j:128 * (j + 1)] for j in range(1, TK // 128)) + x[:, 0:128]

    def count(pred):
        def body(c, acc):
            return acc + fold(jnp.where(pred(c, keys_ref[c]), 1, 0).astype(i32))
        acc = lax.fori_loop(0, nkc, body, jnp.zeros((TQ, 128), i32))
        return jnp.sum(acc, axis=1, keepdims=True)

    thr = _kth_largest_key(lambda cand: count(lambda c, key: key >= cand), (TQ, 1), topk)

    n_ge = count(lambda c, key: key >= thr)

    @pl.when(jnp.max(jnp.where(n_ge != topk, 1, 0)) > 0)
    def _():
        n_gt = count(lambda c, key: key > thr)
        need = topk - n_gt
        cut = _tie_cutoff(lambda j: count(lambda c, key: (key == thr) & (lane + c * TK < j)), (TQ, 1), need, 12)

        def demote(c, carry):
            key = keys_ref[c]
            keys_ref[c] = jnp.where((key == thr) & (lane + c * TK > cut), thr - 1, key)
            return carry

        lax.fori_loop(0, nkc, demote, 0)

    m_ref[...] = jnp.full(m_ref.shape, MASKED, f32)
    acc_ref[...] = jnp.zeros(acc_ref.shape, f32)
    ones = jnp.ones((TK, HEAD_DIM), bf16)

    def attn_chunk(c, carry):
        off = pl.multiple_of(c * TK, TK)
        sel = (keys_ref[c] >= thr) & (lane + c * TK <= qpos)
        for g in range(KV_HEADS):
            kc = k_ref[pl.ds(off, TK), HEAD_DIM * g:HEAD_DIM * (g + 1)]
            vc = jnp.concatenate([v_ref[pl.ds(off, TK), HEAD_DIM * g:HEAD_DIM * (g + 1)], ones], axis=1)
            qg = q_ref[Q_PER_KV * g:Q_PER_KV * (g + 1)].reshape(Q_PER_KV * TQ, HEAD_DIM)
            s_all = lax.dot_general(qg, kc, NT_DIMS, preferred_element_type=f32)
            ps, alphas = [], []
            for r in range(Q_PER_KV):
                hd = Q_PER_KV * g + r
                s = jnp.where(sel, s_all[TQ * r:TQ * (r + 1)], MASKED)
                m_old = m_ref[hd]
                m_new = jnp.maximum(m_old, jnp.max(s, axis=-1, keepdims=True))
                m_ref[hd] = m_new
                alphas.append(jnp.exp2(m_old - m_new))
                ps.append(jnp.exp2(s - jnp.concatenate([m_new] * (TK // 128), axis=1)).astype(bf16))
            pv = jnp.dot(jnp.concatenate(ps, axis=0), vc, preferred_element_type=f32)
            for r in range(Q_PER_KV):
                hd = Q_PER_KV * g + r
                acc_ref[hd] = jnp.concatenate([alphas[r]] * 2, axis=1) * acc_ref[hd] + pv[TQ * r:TQ * (r + 1)]
        return carry

    lax.fori_loop(0, nkc, attn_chunk, 0)

    for hd in range(N_HEADS):
        a = acc_ref[hd]
        o_ref[:, HEAD_DIM * hd:HEAD_DIM * (hd + 1)] = (a[:, :HEAD_DIM] / a[:, HEAD_DIM:]).astype(bf16)


def attention_prompt(q, kb, vb, iq, iw, ikb):
    G, _, T, _ = q.shape
    topk = min(TOPK_MAX, T // 4)
    return pl.pallas_call(
        functools.partial(_attn_prompt_kernel, topk=topk),
        grid=(G, T // TQ),
        in_specs=[pl.BlockSpec((None, N_HEADS, TQ, HEAD_DIM), lambda g, t: (g, 0, t, 0)),
                  pl.BlockSpec((None, T, KV_COLS), lambda g, t: (g, 0, 0)),
                  pl.BlockSpec((None, T, KV_COLS), lambda g, t: (g, 0, 0)),
                  pl.BlockSpec((None, IDX_HEADS, TQ, IDX_DIM), lambda g, t: (g, 0, t, 0)),
                  pl.BlockSpec((None, TQ, IDX_HEADS), lambda g, t: (g, t, 0)),
                  pl.BlockSpec((None, T, IDX_DIM), lambda g, t: (g, 0, 0))],
        out_specs=pl.BlockSpec((None, TQ, N_HEADS * HEAD_DIM), lambda g, t: (g, t, 0)),
        out_shape=jax.ShapeDtypeStruct((G, T, N_HEADS * HEAD_DIM), bf16),
        scratch_shapes=[pltpu.VMEM((T // TK, TQ, TK), i32),
                        pltpu.VMEM((N_HEADS, TQ, 128), f32),
                        pltpu.VMEM((N_HEADS, TQ, 2 * HEAD_DIM), f32)],
        compiler_params=_params("parallel", "arbitrary"), name="attn_prompt",
    )(q, kb, vb, iq, iw, ikb)


SCORE_PAGES = 16
PAGES_PER_STEP = 16
PAGE_ROWS = PAGE_SIZE * KV_HEADS


def _sample_score_kernel(pt_ref, iq_ref, iw_ref, ikn_ref, *refs):
    page_refs = refs[:SCORE_PAGES]
    sc_ref, snew_ref = refs[SCORE_PAGES:]
    iq = iq_ref[...]
    iw = iw_ref[...]
    iqb = iq.astype(bf16)
    for p in range(SCORE_PAGES):
        s = lax.dot_general(iqb, page_refs[p][...].astype(bf16), NT_DIMS, preferred_element_type=f32)
        sc_ref[p:p + 1, :] = jnp.sum(jnp.maximum(s, 0.0) * iw, axis=0, keepdims=True)
    s_new = jnp.sum(iq * ikn_ref[...], axis=1, keepdims=True)
    s_new = jnp.sum(jnp.maximum(s_new, 0.0) * iw, axis=0, keepdims=True)
    snew_ref[...] = jnp.broadcast_to(s_new, snew_ref.shape)


def sample_scores(page_table, iq_s, iw_s, ik_new, cache_ik):
    B, n_pages = page_table.shape

    def page_spec(p):
        return pl.BlockSpec((None, PAGE_SIZE, IDX_DIM), lambda b, j, pt: (pt[b, j * SCORE_PAGES + p], 0, 0))

    grid_spec = pltpu.PrefetchScalarGridSpec(
        num_scalar_prefetch=1,
        grid=(B, n_pages // SCORE_PAGES),
        in_specs=[pl.BlockSpec((None, IDX_HEADS, IDX_DIM), lambda b, j, pt: (b, 0, 0)),
                  pl.BlockSpec((None, IDX_HEADS, 1), lambda b, j, pt: (b, 0, 0)),
                  pl.BlockSpec((None, 1, IDX_DIM), lambda b, j, pt: (b, 0, 0))]
                 + [page_spec(p) for p in range(SCORE_PAGES)],
        out_specs=[pl.BlockSpec((None, SCORE_PAGES, PAGE_SIZE), lambda b, j, pt: (b, j, 0)),
                   pl.BlockSpec((None, 1, PAGE_SIZE), lambda b, j, pt: (b, 0, 0))],
    )
    return pl.pallas_call(
        _sample_score_kernel, grid_spec=grid_spec,
        out_shape=[jax.ShapeDtypeStruct((B, n_pages, PAGE_SIZE), f32), jax.ShapeDtypeStruct((B, 1, PAGE_SIZE), f32)],
        compiler_params=_params("parallel", "arbitrary"), name="sample_scores",
    )(page_table, iq_s, iw_s, ik_new, *([cache_ik] * SCORE_PAGES))


def _sample_select_kernel(sc_ref, snew_ref, sel_ref, seln_ref, *, topk):
    B, past = sc_ref.shape
    keys = _sortable(sc_ref[...])
    key_new = _sortable(snew_ref[:, 0:1])
    idx = lax.broadcasted_iota(i32, keys.shape, 1)
    idx_new = jnp.full((B, 1), past, i32)

    def count(pred):
        new = jnp.where(pred(key_new, idx_new), 1, 0).astype(i32)
        return jnp.sum(jnp.where(pred(keys, idx), 1, 0).astype(i32), axis=1, keepdims=True) + new

    thr = _kth_largest_key(lambda cand: count(lambda k, i: k >= cand), (B, 1), topk)
    need = topk - count(lambda k, i: k > thr)
    cut = _tie_cutoff(lambda jj: count(lambda k, i: (k == thr) & (i < jj)), (B, 1), need, int(past).bit_length())
    chosen = lambda k, i: (k > thr) | ((k == thr) & (i <= cut))
    seln_ref[...] = jnp.broadcast_to(jnp.where(chosen(key_new, idx_new), 0.0, MASKED), seln_ref.shape)
    spread = (lax.broadcasted_iota(i32, (PAGE_SIZE, PAGE_ROWS), 1) // KV_HEADS
              == lax.broadcasted_iota(i32, (PAGE_SIZE, PAGE_ROWS), 0))
    spread = jnp.where(spread, 1.0, 0.0).astype(bf16)
    flags = jnp.where(chosen(keys, idx), 1.0, 0.0).astype(bf16)
    for pg in range(past // PAGE_SIZE):
        rep = jnp.dot(flags[:, PAGE_SIZE * pg:PAGE_SIZE * (pg + 1)], spread, preferred_element_type=f32)
        sel_ref[:, PAGE_ROWS * pg:PAGE_ROWS * (pg + 1)] = jnp.where(rep > 0.5, 0.0, MASKED)


def sample_select(scores, score_new):
    B, n_pages, _ = scores.shape
    past = n_pages * PAGE_SIZE
    topk = min(TOPK_MAX, (past + 1) // 4)
    sel, sel_new = pl.pallas_call(
        functools.partial(_sample_select_kernel, topk=topk),
        out_shape=[jax.ShapeDtypeStruct((B, n_pages * PAGE_ROWS), f32), jax.ShapeDtypeStruct((B, PAGE_SIZE), f32)],
        compiler_params=pltpu.CompilerParams(vmem_limit_bytes=VMEM_LIMIT), name="sample_select",
    )(scores.reshape(B, past), score_new.reshape(B, PAGE_SIZE))
    return sel.reshape(B, n_pages, PAGE_ROWS), sel_new.reshape(B, 1, PAGE_SIZE)


def _sample_attn_kernel(pt_ref, q_ref, kn_ref, vn_ref, sel_ref, seln_ref, *refs):
    k_refs = refs[:PAGES_PER_STEP]
    v_refs = refs[PAGES_PER_STEP:2 * PAGES_PER_STEP]
    o_ref, m_ref, l_ref, acc_ref = refs[2 * PAGES_PER_STEP:]
    j = pl.program_id(1)
    qb = q_ref[...]
    head_kv = lax.broadcasted_iota(i32, (N_HEADS, PAGE_ROWS), 0) // Q_PER_KV
    row_kv = lax.broadcasted_iota(i32, (N_HEADS, PAGE_ROWS), 1) % KV_HEADS
    own_rows = jnp.where(head_kv == row_kv, 0.0, MASKED)

    @pl.when(j == 0)
    def _():
        head_kv_col = lax.broadcasted_iota(i32, (N_HEADS, HEAD_DIM), 0) // Q_PER_KV
        kn = jnp.zeros((N_HEADS, HEAD_DIM), f32)
        vn = jnp.zeros((N_HEADS, HEAD_DIM), f32)
        for g in range(KV_HEADS):
            cols = slice(HEAD_DIM * g, HEAD_DIM * (g + 1))
            kn = jnp.where(head_kv_col == g, kn_ref[:, cols], kn)
            vn = jnp.where(head_kv_col == g, vn_ref[:, cols], vn)
        m_ref[...] = jnp.sum(qb.astype(f32) * kn, axis=1, keepdims=True) + seln_ref[:, 0:1]
        l_ref[...] = jnp.ones(l_ref.shape, f32)
        acc_ref[...] = vn

    s = jnp.concatenate(
        [lax.dot_general(qb, k_refs[p][...].astype(bf16), NT_DIMS, preferred_element_type=f32)
         + sel_ref[p:p + 1, :] + own_rows for p in range(PAGES_PER_STEP)], axis=1)
    m_old = m_ref[...]
    m_new = jnp.maximum(m_old, jnp.max(s, axis=1, keepdims=True))
    p_ = jnp.exp2(s - m_new)
    alpha = jnp.exp2(m_old - m_new)
    l_ref[...] = alpha * l_ref[...] + jnp.sum(p_, axis=1, keepdims=True)
    v = jnp.concatenate([v_refs[p][...].astype(bf16) for p in range(PAGES_PER_STEP)], axis=0)
    acc_ref[...] = alpha * acc_ref[...] + jnp.dot(p_.astype(bf16), v, preferred_element_type=f32)
    m_ref[...] = m_new

    @pl.when(j == pl.num_programs(1) - 1)
    def _():
        o_ref[...] = (acc_ref[...] / l_ref[...]).astype(bf16)


def sample_attention(page_table, q_s, k_new, v_new, sel, sel_new, cache_k, cache_v):
    B, n_pages = page_table.shape
    steps = n_pages // PAGES_PER_STEP

    def page_spec(p):
        return pl.BlockSpec((None, PAGE_ROWS, HEAD_DIM), lambda b, j, pt: (pt[b, j * PAGES_PER_STEP + p], 0, 0))

    grid_spec = pltpu.PrefetchScalarGridSpec(
        num_scalar_prefetch=1,
        grid=(B, steps),
        in_specs=[pl.BlockSpec((None, N_HEADS, HEAD_DIM), lambda b, j, pt: (b, 0, 0)),
                  pl.BlockSpec((None, 1, KV_COLS), lambda b, j, pt: (b, 0, 0)),
                  pl.BlockSpec((None, 1, KV_COLS), lambda b, j, pt: (b, 0, 0)),
                  pl.BlockSpec((None, PAGES_PER_STEP, PAGE_ROWS), lambda b, j, pt: (b, j, 0)),
                  pl.BlockSpec((None, 1, PAGE_SIZE), lambda b, j, pt: (b, 0, 0))]
                 + [page_spec(p) for p in range(PAGES_PER_STEP)] * 2,
        out_specs=pl.BlockSpec((None, N_HEADS, HEAD_DIM), lambda b, j, pt: (b, 0, 0)),
        scratch_shapes=[pltpu.VMEM((N_HEADS, 1), f32), pltpu.VMEM((N_HEADS, 1), f32),
                        pltpu.VMEM((N_HEADS, HEAD_DIM), f32)],
    )
    return pl.pallas_call(
        _sample_attn_kernel, grid_spec=grid_spec,
        out_shape=jax.ShapeDtypeStruct((B, N_HEADS, HEAD_DIM), bf16),
        compiler_params=_params("parallel", "arbitrary"), name="sample_attn",
    )(page_table, q_s, k_new, v_new, sel, sel_new, *([cache_k] * PAGES_PER_STEP), *([cache_v] * PAGES_PER_STEP))


def _glu_kernel(x_ref, wa_ref, wg_ref, ba_ref, bg_ref, o_ref):
    x = x_ref[...]
    a = jnp.dot(x, wa_ref[...].astype(bf16), preferred_element_type=f32) + ba_ref[...]
    g = jnp.dot(x, wg_ref[...].astype(bf16), preferred_element_type=f32) + bg_ref[...]
    o_ref[...] = a * jax.nn.sigmoid(g)


def glu_projection(h, w_pw1, b_pw1):
    G, T, K = h.shape
    C = w_pw1.shape[1] // 2
    tm, tn = min(T, 1024), 512
    nj = C // tn
    b2 = b_pw1.reshape(1, 2 * C)
    return pl.pallas_call(
        _glu_kernel,
        grid=(G, T // tm, nj),
        in_specs=[pl.BlockSpec((None, tm, K), lambda g, t, j: (g, t, 0)),
                  pl.BlockSpec((K, tn), lambda g, t, j: (0, j)),
                  pl.BlockSpec((K, tn), lambda g, t, j: (0, nj + j)),
                  pl.BlockSpec((1, tn), lambda g, t, j: (0, j)),
                  pl.BlockSpec((1, tn), lambda g, t, j: (0, nj + j))],
        out_specs=pl.BlockSpec((None, tm, tn), lambda g, t, j: (g, t, j)),
        out_shape=jax.ShapeDtypeStruct((G, T, C), f32),
        compiler_params=_params("parallel", "parallel", "parallel"), name="glu_proj",
    )(h, w_pw1, w_pw1, b2, b2)


def _ln_silu(y, g, b):
    mu = jnp.mean(y, axis=-1, keepdims=True)
    var = jnp.mean(jnp.square(y - mu), axis=-1, keepdims=True)
    z = (y - mu) * lax.rsqrt(var + LN_EPS) * g + b
    return z * jax.nn.sigmoid(z)


CONV_HALO = 32


CONV_COLS = 256


def _conv_prompt_kernel(u_ref, halo_ref, w_ref, b_ref, g_ref, be_ref, o_ref, ext_ref, sh_ref, y_ref):
    t = pl.program_id(1)
    tt, C = u_ref.shape
    halo = halo_ref[...]
    ext_ref[0:CONV_HALO, :] = jnp.where(t == 0, jnp.zeros_like(halo), halo)
    ext_ref[CONV_HALO:, :] = u_ref[...]
    base = CONV_HALO - (CONV_WIDTH - 1)
    y_ref[...] = jnp.zeros(y_ref.shape, f32) + b_ref[...]
    for shift in range(8):
        taps = [(a, 8 * a + shift - base) for a in range(CONV_HALO // 8 + 1)
                if 0 <= 8 * a + shift - base < CONV_WIDTH]
        rows = 8 * taps[-1][0] + tt
        sh_ref[0:rows, :] = ext_ref[shift:shift + rows, :]
        for c in range(C // CONV_COLS):
            cols = slice(CONV_COLS * c, CONV_COLS * (c + 1))
            acc = y_ref[:, cols]
            for a, k in taps:
                acc = acc + sh_ref[8 * a:8 * a + tt, cols] * w_ref[k:k + 1, cols]
            y_ref[:, cols] = acc
    o_ref[...] = _ln_silu(y_ref[...], g_ref[...], be_ref[...]).astype(bf16)


def conv_prompt(u, w_dw, b_dw, ln_g, ln_b):
    G, T, C = u.shape
    tt = 128
    hb = tt // CONV_HALO
    row = pl.BlockSpec((1, C), lambda g, t: (0, 0))
    return pl.pallas_call(
        _conv_prompt_kernel,
        grid=(G, T // tt),
        in_specs=[pl.BlockSpec((None, tt, C), lambda g, t: (g, t, 0)),
                  pl.BlockSpec((None, CONV_HALO, C), lambda g, t: (g, jnp.maximum(t * hb - 1, 0), 0)),
                  pl.BlockSpec((CONV_WIDTH, C), lambda g, t: (0, 0)), row, row, row],
        out_specs=pl.BlockSpec((None, tt, C), lambda g, t: (g, t, 0)),
        out_shape=jax.ShapeDtypeStruct((G, T, C), bf16),
        scratch_shapes=[pltpu.VMEM((CONV_HALO + tt, C), f32), pltpu.VMEM((CONV_HALO + tt, C), f32),
                        pltpu.VMEM((tt, C), f32)],
        compiler_params=_params("parallel", "parallel"), name="conv_prompt",
    )(u, u, w_dw, b_dw.reshape(1, C), ln_g.reshape(1, C), ln_b.reshape(1, C))


def _conv_sample_kernel(u_ref, hist_ref, w_ref, b_ref, g_ref, be_ref, o_ref):
    w = w_ref[...]
    y = jnp.sum(hist_ref[...] * w[None, :CONV_WIDTH - 1, :], axis=1) + u_ref[...] * w[CONV_WIDTH - 1:, :] + b_ref[...]
    o_ref[...] = _ln_silu(y, g_ref[...], be_ref[...]).astype(bf16)


def conv_sample(u, hist, w_dw, b_dw, ln_g, ln_b):
    B, C = u.shape
    tc = 512
    col = lambda r: pl.BlockSpec((r, tc), lambda j: (0, j))
    return pl.pallas_call(
        _conv_sample_kernel,
        grid=(C // tc,),
        in_specs=[col(B), pl.BlockSpec((B, CONV_WIDTH - 1, tc), lambda j: (0, 0, j)), col(CONV_WIDTH),
                  col(1), col(1), col(1)],
        out_specs=col(B),
        out_shape=jax.ShapeDtypeStruct((B, C), bf16),
        compiler_params=_params("arbitrary"), name="conv_sample",
    )(u, hist, w_dw, b_dw.reshape(1, C), ln_g.reshape(1, C), ln_b.reshape(1, C))


def _top_values(x, n):
    vals = []
    for _ in range(n):
        m = jnp.max(x, axis=0, keepdims=True)
        vals.append(m)
        x = jnp.where(x == m, -jnp.inf, x)
    return vals


NOT_TOP = float(PEER_TOPK)


def _top_ranked(x, n):
    vals = []
    rank = jnp.full(x.shape, NOT_TOP, f32)
    for r in range(n):
        m = jnp.max(x, axis=0, keepdims=True)
        hit = x == m
        vals.append(m)
        rank = jnp.where(hit, float(r), rank)
        x = jnp.where(hit, -jnp.inf, x)
    return vals, rank


def _peer_gate_kernel(q_ref, sk_ref, cnt_ref, ea_ref, rb_ref, eb_ref):
    for h in range(PEER_HEADS):
        st = []
        for p in range(2):
            col = (2 * h + p) * PEER_NKEYS
            st.append(lax.dot_general(sk_ref[h, p].astype(bf16), q_ref[:, col:col + PEER_NKEYS], NT_DIMS,
                                      preferred_element_type=f32))
        a, b = st
        va, rank_a = _top_ranked(a, PEER_TOPK)
        vbl, rank_b = _top_ranked(b, PEER_TOPK)
        row = lax.broadcasted_iota(i32, (PEER_TOPK, a.shape[1]), 0)
        vb = jnp.zeros((PEER_TOPK, a.shape[1]), f32)
        for r in range(PEER_TOPK):
            vb = jnp.where(row == r, vbl[r], vb)
        cands = [va[r] + vb for r in range(PEER_TOPK)]
        row8 = lax.broadcasted_iota(i32, (8, a.shape[1]), 0)
        va_tail = jnp.zeros((8, a.shape[1]), f32)
        for r in range(8):
            va_tail = jnp.where(row8 == r, va[8 + r], va_tail)
        search = [cands[0]] + [cands[r][:8] for r in range(1, 8)] + [va_tail + vbl[0]]
        thr = _top_values(jnp.concatenate(search, axis=0), PEER_TOPK)[-1]
        top = va[0] + vbl[0]
        z = jnp.zeros_like(top)
        cnt = jnp.zeros(a.shape, f32)
        for r in range(PEER_TOPK):
            keep = cands[r] >= thr
            z = z + jnp.sum(jnp.where(keep, jnp.exp(cands[r] - top), 0.0), axis=0, keepdims=True)
            kept = jnp.sum(jnp.where(keep, 1.0, 0.0), axis=0, keepdims=True)
            cnt = jnp.where(rank_a == float(r), kept, cnt)
        ea = jnp.exp(a - va[0]) / z
        for grp in range(PEER_NKEYS // 8):
            cnt_ref[grp, h] = cnt[8 * grp:8 * (grp + 1), :]
            ea_ref[grp, h] = ea[8 * grp:8 * (grp + 1), :]
        eb = jnp.exp(b - vbl[0])
        for s in range(PEER_NKEYS // BF16_ROWS):
            rows = slice(BF16_ROWS * s, BF16_ROWS * (s + 1))
            rb_ref[h, s] = rank_b[rows, :].astype(bf16)
            eb_ref[h, s] = eb[rows, :].astype(bf16)


def peer_gates(q, sub_keys):
    G, T, _ = q.shape
    tm = min(T, 256)
    jt = PEER_NKEYS // BF16_ROWS
    a_spec = pl.BlockSpec((None, PEER_NKEYS // 8, PEER_HEADS, 8, tm), lambda g, t: (g, 0, 0, 0, t))
    b_spec = pl.BlockSpec((None, PEER_HEADS, jt, BF16_ROWS, tm), lambda g, t: (g, 0, 0, 0, t))
    a_shape = jax.ShapeDtypeStruct((G, PEER_NKEYS // 8, PEER_HEADS, 8, T), f32)
    b_shape = jax.ShapeDtypeStruct((G, PEER_HEADS, jt, BF16_ROWS, T), bf16)
    return pl.pallas_call(
        _peer_gate_kernel,
        grid=(G, T // tm),
        in_specs=[pl.BlockSpec((None, tm, q.shape[2]), lambda g, t: (g, t, 0)),
                  pl.BlockSpec(sub_keys.shape, lambda g, t: (0, 0, 0, 0))],
        out_specs=[a_spec, a_spec, b_spec, b_spec],
        out_shape=[a_shape, a_shape, b_shape, b_shape],
        compiler_params=_params("parallel", "parallel"), name="peer_gates",
    )(q, sub_keys)


PEER_ROWS = 8
PEER_TE = PEER_ROWS * PEER_NKEYS
PEER_OUT_COLS = 512
PEER_PASS = 4


def _gelu(x):
    return 0.5 * x * (1.0 + lax.erf(x * (2.0 ** -0.5)))


def _peer_expert_kernel(x_ref, u_ref, v_ref, cnt_ref, ea_ref, rb_ref, eb_ref, o_ref):
    @pl.when(pl.program_id(2) == 0)
    def _():
        o_ref[...] = jnp.zeros(o_ref.shape, f32)

    tm = x_ref.shape[0]
    jt = PEER_NKEYS // BF16_ROWS
    hid = lax.dot_general(x_ref[...], u_ref[...], NT_DIMS, preferred_element_type=f32)
    act = _gelu(hid)
    gates = [[None] * (tm // 128) for _ in range(PEER_ROWS)]
    for c in range(tm // 128):
        tok = slice(128 * c, 128 * (c + 1))
        for r0 in range(0, PEER_ROWS, PEER_PASS):
            gt = [jnp.zeros((jt, BF16_ROWS, 128), bf16) for _ in range(PEER_PASS)]
            for h in range(PEER_HEADS):
                rb = rb_ref[h, :, :, tok]
                eb = eb_ref[h, :, :, tok]
                for k in range(PEER_PASS):
                    r = r0 + k
                    cnt = jnp.broadcast_to(cnt_ref[h, r:r + 1, tok], (BF16_ROWS, 128)).astype(bf16)
                    eai = jnp.broadcast_to(ea_ref[h, r:r + 1, tok], (BF16_ROWS, 128)).astype(bf16)
                    active = jnp.minimum(jnp.maximum(cnt[None] - rb, 0.0), 1.0)
                    gt[k] = gt[k] + active * (eai[None] * eb)
            for k in range(PEER_PASS):
                gates[r0 + k][c] = gt[k].astype(f32).reshape(PEER_NKEYS, 128).T
    blocks = [jnp.concatenate(gates[r], axis=0) * act[:, PEER_NKEYS * r:PEER_NKEYS * (r + 1)]
              for r in range(PEER_ROWS)]
    w = jnp.concatenate(blocks, axis=1).astype(bf16)
    for j in range(v_ref.shape[1] // PEER_OUT_COLS):
        cols = slice(PEER_OUT_COLS * j, PEER_OUT_COLS * (j + 1))
        o_ref[:, cols] += jnp.dot(w, v_ref[:, cols], preferred_element_type=f32)


def peer_experts(x, gates, u_tab, v_tab, layer):
    G, T, Dm = x.shape
    cnt, ea, rb, eb = gates
    tm = min(T, 512)
    a_spec = pl.BlockSpec((None, None, PEER_HEADS, 8, tm), lambda g, t, e: (g, e, 0, 0, t))
    b_spec = pl.BlockSpec((None, PEER_HEADS, PEER_NKEYS // BF16_ROWS, BF16_ROWS, tm), lambda g, t, e: (g, 0, 0, 0, t))
    tab_spec =pl.BlockSpec((None, PEER_TE, Dm), lambda g, t, e: (layer, e, 0))
    return pl.pallas_call(
        _peer_expert_kernel,
        grid=(G, T // tm, u_tab.shape[1] // PEER_TE),
        in_specs=[pl.BlockSpec((None, tm, Dm), lambda g, t, e: (g, t, 0)), tab_spec, tab_spec,
                  a_spec, a_spec, b_spec, b_spec],
        out_specs=pl.BlockSpec((None, tm, Dm), lambda g, t, e: (g, t, 0)),
        out_shape=jax.ShapeDtypeStruct((G, T, Dm), f32),
        compiler_params=_params("parallel", "parallel", "arbitrary"), name="peer_experts",
    )(x, u_tab, v_tab, cnt, ea, rb, eb)


def peer(h, w_q, sub_keys, u_tab, v_tab, layer):
    G, T, Dm = h.shape
    Tp = -(-T // 128) * 128
    if Tp != T:
        h = jnp.pad(h, ((0, 0), (0, Tp - T), (0, 0)))
    q = token_matmul(h, w_q, bf16)
    out = peer_experts(h, peer_gates(q, sub_keys), u_tab, v_tab, layer)
    return out[:, :T] if Tp != T else out


def _trunk(y, sample, mod, norm_g, attend, conv, weights):
    (w_in, w_idx, q_gain, k_gain, w_out, rope128, rope64, w_pw1, b_pw1, w_pw2, peer_w_q, peer_sub_keys, peer_u,
     peer_v) = weights
    nm = functools.partial(normmod, mod=mod, sample=sample)
    h = nm(y, norm_g=norm_g[0, 0], shift=(0, 0), scale=(0, 1))
    q = q_projection(h, w_in, q_gain, *rope128)
    k, kb, v, vb = kv_projection(h, w_in, k_gain, *rope128)
    iq, ik, ikb, iw = idx_projection(h, w_idx, *rope64)
    o = attend(q, k, kb, v, vb, iq, ik, ikb, iw)
    d = token_matmul(o, w_out, f32)
    y, h = nm(y, delta=d, gate=(0, 2), norm_g=norm_g[0, 1], shift=(0, 3), scale=(0, 4))
    d = peer(h, peer_w_q[0], peer_sub_keys[0], peer_u, peer_v, 0)
    y, h = nm(y, delta=d, gate=(0, 5), norm_g=norm_g[1, 0], shift=(1, 0), scale=(1, 1))
    u = glu_projection(h, w_pw1, b_pw1)
    c, state = conv(u)
    d = token_matmul(c, w_pw2, f32)
    y, h = nm(y, delta=d, gate=(1, 2), norm_g=norm_g[1, 1], shift=(1, 3), scale=(1, 4))
    d = peer(h, peer_w_q[1], peer_sub_keys[1], peer_u, peer_v, 1)
    y = nm(y, delta=d, gate=(1, 5))
    return y, k, v, ik, state


def kernel(x_prompt, x_sample, cache_k, cache_v, cache_idx_k, state_conv, page_table, c_prompt, c_sample, w_ada, b_ada, norm_g, attn_w_in, attn_q_gain, attn_k_gain, attn_w_out, conv_w_pw1, conv_b_pw1, conv_w_dw, conv_b_dw, conv_ln_g, conv_ln_b, conv_w_pw2, peer_w_q, peer_sub_keys, peer_u, peer_v):
    B, T, Dm = x_prompt.shape
    Bd = x_sample.shape[0]
    n_pages = page_table.shape[1]
    past = n_pages * PAGE_SIZE
    n_phys = cache_k.shape[1]

    c_all = jnp.concatenate([c_sample, c_prompt, jnp.zeros((ADA_ROWS - Bd - B, Dm), f32)], axis=0)
    mod = ada_modulation(c_all, w_ada, b_ada)
    ng = norm_g.reshape(norm_g.shape[0], 2, 1, Dm)

    peer_u_bf, peer_v_bf = peer_u.astype(bf16), peer_v.astype(bf16)
    w_in = attn_w_in[0]
    idx0 = N_HEADS * HEAD_DIM + 2 * KV_COLS
    w_idx = jnp.concatenate([w_in[:, idx0:], jnp.zeros((Dm, IDX_W_COLS - (w_in.shape[1] - idx0)), f32)], axis=1)

    def weights(pos):
        return (w_in, w_idx, attn_q_gain[0], attn_k_gain[0], attn_w_out[0], _rope_tables(pos, HEAD_DIM),
                _rope_tables(pos, IDX_DIM), conv_w_pw1[0], conv_b_pw1[0], conv_w_pw2[0], peer_w_q, peer_sub_keys,
                peer_u_bf, peer_v_bf)

    conv_tail = (conv_w_dw[0], conv_b_dw[0], conv_ln_g[0], conv_ln_b[0])

    def attend_prompt(q, k, kb, v, vb, iq, ik, ikb, iw):
        return attention_prompt(q, kb, vb, iq, iw, ikb)

    def conv_p(u):
        return conv_prompt(u, *conv_tail), u[:, T - (CONV_WIDTH - 1):]

    def attend_sample(q, k, kb, v, vb, iq, ik, ikb, iw):
        iq_s = jnp.transpose(iq[0], (1, 0, 2)).astype(f32)
        scores, score_new = sample_scores(page_table, iq_s, iw.reshape(Bd, IDX_HEADS, 1),
                                          ik.reshape(Bd, 1, IDX_DIM), cache_idx_k[0])
        sel, sel_new = sample_select(scores, score_new)
        o = sample_attention(page_table, jnp.transpose(q[0], (1, 0, 2)), k.reshape(Bd, 1, KV_COLS),
                             v.reshape(Bd, 1, KV_COLS), sel, sel_new,
                             cache_k[0].reshape(n_phys, PAGE_ROWS, HEAD_DIM), cache_v[0].reshape(n_phys, PAGE_ROWS, HEAD_DIM))
        return o.reshape(1, Bd, N_HEADS * HEAD_DIM)

    def conv_s(u):
        hist = state_conv[0]
        c = conv_sample(u[0], hist, *conv_tail)
        return c[None], jnp.concatenate([hist[:, 1:], u[0][:, None, :]], axis=1)

    yp, kp, vp, ikp, stp = _trunk(x_prompt, False, mod, ng, attend_prompt, conv_p,
                                  weights(jnp.arange(T, dtype=i32)))
    ys, ks, vs, iks, sts = _trunk(x_sample.reshape(1, Bd, Dm), True, mod, ng, attend_sample, conv_s,
                                  weights(jnp.full((Bd,), past, i32)))
    return (yp, ys.reshape(Bd, 1, Dm),
            kp.reshape(1, B, T, KV_HEADS, HEAD_DIM), vp.reshape(1, B, T, KV_HEADS, HEAD_DIM),
            ikp.reshape(1, B, T, IDX_DIM),
            ks.reshape(1, Bd, 1, KV_HEADS, HEAD_DIM), vs.reshape(1, Bd, 1, KV_HEADS, HEAD_DIM),
            iks.reshape(1, Bd, 1, IDX_DIM),
            stp[None], sts[None])
```

```python
import functools
import math

import numpy as np
import jax
import jax.numpy as jnp
from jax import lax
from jax.experimental import pallas as pl
from jax.experimental.pallas import tpu as pltpu

f32 = jnp.float32
bf16 = jnp.bfloat16
i32 = jnp.int32

D_MODEL = 2048
N_HEADS = 16
HEAD_DIM = 128
KV_HEADS = 4
Q_PER_KV = N_HEADS // KV_HEADS
KV_COLS = KV_HEADS * HEAD_DIM
IDX_HEADS = 16
IDX_DIM = 64
TOPK_MAX = 256
ROPE_THETA = 10000.0
PAGE_SIZE = 128
CONV_WIDTH = 31
PEER_HEADS = 8
PEER_NKEYS = 128
PEER_TOPK = 16
RMS_EPS = 1e-6
LN_EPS = 1e-5
NEG = -1e30
MASKED = -1e30
INT_MIN = -(2 ** 31)
Q_SCALE = HEAD_DIM ** -0.5 * math.log2(math.e)
VMEM_LIMIT = 56 * 1024 * 1024
BF16_ROWS = 16
ADA_ROWS = 40
ADA_PROMPT_ROW = 32

NT_DIMS = (((1,), (1,)), ((), ()))


def _params(*sem):
    return pltpu.CompilerParams(dimension_semantics=sem, vmem_limit_bytes=VMEM_LIMIT)


def _ada_kernel(c_ref, w_ref, b_ref, o_ref):
    c = c_ref[...]
    a = (c * jax.nn.sigmoid(c)).astype(bf16)
    o_ref[...] = jnp.dot(a, w_ref[...].astype(bf16), preferred_element_type=f32) + b_ref[...]


def ada_modulation(c_all, w_ada, b_ada):
    depth, d, _ = w_ada.shape
    tn = 1024
    nj = d // tn
    return pl.pallas_call(
        _ada_kernel,
        grid=(depth, 6, nj),
        in_specs=[
            pl.BlockSpec((ADA_ROWS, d), lambda l, k, j: (0, 0)),
            pl.BlockSpec((None, d, tn), lambda l, k, j: (l, 0, k * nj + j)),
            pl.BlockSpec((None, 1, tn), lambda l, k, j: (l, 0, k * nj + j)),
        ],
        out_specs=pl.BlockSpec((None, None, ADA_ROWS, tn), lambda l, k, j: (l, k, 0, j)),
        out_shape=jax.ShapeDtypeStruct((depth, 6, ADA_ROWS, d), f32),
        compiler_params=_params("parallel", "parallel", "parallel"),
        name="ada",
    )(c_all, w_ada, b_ada.reshape(depth, 1, 6 * d))


def _mod_operand(mod, sample):
    return mod if sample else mod.reshape(mod.shape[:3] + (1, mod.shape[3]))


def _mod_spec(sample, layer, k):
    if sample:
        return pl.BlockSpec((None, None, 32, D_MODEL), lambda g, t, *_: (layer, k, 0, 0))
    return pl.BlockSpec((None, None, None, 1, D_MODEL), lambda g, t, *_: (layer, k, ADA_PROMPT_ROW + g, 0, 0))


def _normmod_kernel(*refs, has_delta, want_h):
    refs = list(refs)
    y = refs.pop(0)[...]
    if has_delta:
        d_ref, gate_ref = refs.pop(0), refs.pop(0)
        y = y + gate_ref[...] * d_ref[...]
    if want_h:
        g_ref, sh_ref, sc_ref = refs.pop(0), refs.pop(0), refs.pop(0)
    if has_delta:
        refs.pop(0)[...] = y
    if want_h:
        ms = jnp.mean(y * y, axis=-1, keepdims=True)
        hn = y * lax.rsqrt(ms + RMS_EPS) * g_ref[...]
        refs.pop(0)[...] = (hn * (1.0 + sc_ref[...]) + sh_ref[...]).astype(bf16)


def normmod(y, mod, sample, *, delta=None, gate=None, norm_g=None, shift=None, scale=None):
    G, T, Dm = y.shape
    tm = min(T, 256)
    tok = pl.BlockSpec((None, tm, Dm), lambda g, t: (g, t, 0))
    modop = _mod_operand(mod, sample)
    ins, specs = [y], [tok]
    if delta is not None:
        ins += [delta, modop]
        specs += [tok, _mod_spec(sample, *gate)]
    if norm_g is not None:
        ins += [norm_g, modop, modop]
        specs += [pl.BlockSpec((1, Dm), lambda g, t: (0, 0)), _mod_spec(sample, *shift), _mod_spec(sample, *scale)]
    outs, ospecs = [], []
    if delta is not None:
        outs.append(jax.ShapeDtypeStruct((G, T, Dm), f32)); ospecs.append(tok)
    if norm_g is not None:
        outs.append(jax.ShapeDtypeStruct((G, T, Dm), bf16)); ospecs.append(tok)
    res = pl.pallas_call(
        functools.partial(_normmod_kernel, has_delta=delta is not None, want_h=norm_g is not None),
        grid=(G, T // tm), in_specs=specs, out_specs=ospecs, out_shape=outs,
        compiler_params=_params("parallel", "parallel"), name="normmod",
    )(*ins)
    return res if len(res) > 1 else res[0]


def _matmul_kernel(x_ref, w_ref, o_ref):
    o_ref[...] = jnp.dot(x_ref[...], w_ref[...].astype(bf16), preferred_element_type=f32).astype(o_ref.dtype)


def token_matmul(x, w, out_dtype):
    G, T, K = x.shape
    N = w.shape[1]
    tm, tn = min(T, 1024), 512
    return pl.pallas_call(
        _matmul_kernel,
        grid=(G, T // tm, N // tn),
        in_specs=[pl.BlockSpec((None, tm, K), lambda g, t, j: (g, t, 0)),
                  pl.BlockSpec((K, tn), lambda g, t, j: (0, j))],
        out_specs=pl.BlockSpec((None, tm, tn), lambda g, t, j: (g, t, j)),
        out_shape=jax.ShapeDtypeStruct((G, T, N), out_dtype),
        compiler_params=_params("parallel", "parallel", "parallel"), name="token_matmul",
    )(x, w)


def _rope_tables(pos, dim):
    half = dim // 2
    inv = ROPE_THETA ** (-jnp.arange(half, dtype=f32) / half)
    ang = pos.astype(f32)[:, None] * inv
    cos, sin = jnp.cos(ang), jnp.sin(ang)
    reps = 128 // dim
    return jnp.tile(jnp.concatenate([cos, cos], -1), (1, reps)), jnp.tile(jnp.concatenate([-sin, sin], -1), (1, reps))


def _head_rmsnorm(x, gain):
    ms = jnp.mean(x * x, axis=-1, keepdims=True)
    return x * lax.rsqrt(ms + RMS_EPS) * gain


def _qproj_kernel(x_ref, w_ref, gain_ref, cos_ref, sin_ref, o_ref):
    acc = jnp.dot(x_ref[...], w_ref[...].astype(bf16), preferred_element_type=f32)
    cos, sin, gain = cos_ref[...], sin_ref[...], gain_ref[...]
    ones = jnp.ones((HEAD_DIM, HEAD_DIM), bf16)
    for h in range(4):
        xh = acc[:, HEAD_DIM * h:HEAD_DIM * (h + 1)]
        ss = jnp.dot((xh * xh).astype(bf16), ones, preferred_element_type=f32)
        xn = xh * lax.rsqrt(ss * (1.0 / HEAD_DIM) + RMS_EPS) * gain
        r = xn * cos + pltpu.roll(xn, HEAD_DIM // 2, 1) * sin
        o_ref[h] = (r * Q_SCALE).astype(bf16)


def q_projection(h, w_in, q_gain, cos, sin):
    G, T, K = h.shape
    tm = min(T, 1024)
    return pl.pallas_call(
        _qproj_kernel,
        grid=(G, T // tm, 4),
        in_specs=[pl.BlockSpec((None, tm, K), lambda g, t, j: (g, t, 0)),
                  pl.BlockSpec((K, 512), lambda g, t, j: (0, j)),
                  pl.BlockSpec((1, HEAD_DIM), lambda g, t, j: (0, 0)),
                  pl.BlockSpec((tm, 128), lambda g, t, j: (t, 0)),
                  pl.BlockSpec((tm, 128), lambda g, t, j: (t, 0))],
        out_specs=pl.BlockSpec((None, 4, tm, HEAD_DIM), lambda g, t, j: (g, j, t, 0)),
        out_shape=jax.ShapeDtypeStruct((G, N_HEADS, T, HEAD_DIM), bf16),
        compiler_params=_params("parallel", "parallel", "parallel"), name="q_proj",
    )(h, w_in, q_gain.reshape(1, HEAD_DIM), cos, sin)


def _kvproj_kernel(x_ref, wk_ref, wv_ref, gain_ref, cos_ref, sin_ref, k_ref, kb_ref, v_ref, vb_ref):
    x = x_ref[...]
    kacc = jnp.dot(x, wk_ref[...].astype(bf16), preferred_element_type=f32)
    cos, sin, gain = cos_ref[...], sin_ref[...], gain_ref[...]
    for h in range(KV_HEADS):
        sl = slice(HEAD_DIM * h, HEAD_DIM * (h + 1))
        xn = _head_rmsnorm(kacc[:, sl], gain)
        r = xn * cos + pltpu.roll(xn, HEAD_DIM // 2, 1) * sin
        k_ref[:, sl] = r
        kb_ref[:, sl] = r.astype(bf16)
    v = jnp.dot(x, wv_ref[...].astype(bf16), preferred_element_type=f32)
    v_ref[...] = v
    vb_ref[...] = v.astype(bf16)


def kv_projection(h, w_in, k_gain, cos, sin):
    G, T, K = h.shape
    tm = min(T, 1024)
    kcol = (N_HEADS * HEAD_DIM) // KV_COLS
    tok = pl.BlockSpec((None, tm, KV_COLS), lambda g, t: (g, t, 0))
    return pl.pallas_call(
        _kvproj_kernel,
        grid=(G, T // tm),
        in_specs=[pl.BlockSpec((None, tm, K), lambda g, t: (g, t, 0)),
                  pl.BlockSpec((K, KV_COLS), lambda g, t: (0, kcol)),
                  pl.BlockSpec((K, KV_COLS), lambda g, t: (0, kcol + 1)),
                  pl.BlockSpec((1, HEAD_DIM), lambda g, t: (0, 0)),
                  pl.BlockSpec((tm, 128), lambda g, t: (t, 0)),
                  pl.BlockSpec((tm, 128), lambda g, t: (t, 0))],
        out_specs=[tok, tok, tok, tok],
        out_shape=[jax.ShapeDtypeStruct((G, T, KV_COLS), f32), jax.ShapeDtypeStruct((G, T, KV_COLS), bf16),
                   jax.ShapeDtypeStruct((G, T, KV_COLS), f32), jax.ShapeDtypeStruct((G, T, KV_COLS), bf16)],
        compiler_params=_params("parallel", "parallel"), name="kv_proj",
    )(h, w_in, w_in, k_gain.reshape(1, HEAD_DIM), cos, sin)


IDX_W_COLS = 1152


def _idxproj_kernel(x_ref, w_ref, cos_ref, sin_ref, iq_ref, ik_ref, ikb_ref, iw_ref):
    acc = jnp.dot(x_ref[...], w_ref[...].astype(bf16), preferred_element_type=f32)
    cos, sin = cos_ref[...], sin_ref[...]
    lane = lax.broadcasted_iota(i32, cos.shape, 1)
    first = (lane % IDX_DIM) < (IDX_DIM // 2)

    def rope64(x):
        partner = jnp.where(first, pltpu.roll(x, 128 - IDX_DIM // 2, 1), pltpu.roll(x, IDX_DIM // 2, 1))
        return x * cos + partner * sin

    for p in range(IDX_HEADS // 2):
        r = rope64(acc[:, 128 * p:128 * (p + 1)])
        iq_ref[2 * p] = r[:, :IDX_DIM].astype(bf16)
        iq_ref[2 * p + 1] = r[:, IDX_DIM:].astype(bf16)
    slab = acc[:, IDX_HEADS * IDX_DIM:]
    r = rope64(slab)[:, :IDX_DIM]
    ik_ref[...] = r
    ikb_ref[...] = r.astype(bf16)
    iw_ref[...] = slab[:, IDX_DIM:IDX_DIM + IDX_HEADS] * (IDX_HEADS ** -0.5)


def idx_projection(h, w_idx, cos, sin):
    G, T, K = h.shape
    tm = min(T, 512)
    return pl.pallas_call(
        _idxproj_kernel,
        grid=(G, T // tm),
        in_specs=[pl.BlockSpec((None, tm, K), lambda g, t: (g, t, 0)),
                  pl.BlockSpec((K, IDX_W_COLS), lambda g, t: (0, 0)),
                  pl.BlockSpec((tm, 128), lambda g, t: (t, 0)),
                  pl.BlockSpec((tm, 128), lambda g, t: (t, 0))],
        out_specs=[pl.BlockSpec((None, IDX_HEADS, tm, IDX_DIM), lambda g, t: (g, 0, t, 0)),
                   pl.BlockSpec((None, tm, IDX_DIM), lambda g, t: (g, t, 0)),
                   pl.BlockSpec((None, tm, IDX_DIM), lambda g, t: (g, t, 0)),
                   pl.BlockSpec((None, tm, IDX_HEADS), lambda g, t: (g, t, 0))],
        out_shape=[jax.ShapeDtypeStruct((G, IDX_HEADS, T, IDX_DIM), bf16),
                   jax.ShapeDtypeStruct((G, T, IDX_DIM), f32),
                   jax.ShapeDtypeStruct((G, T, IDX_DIM), bf16),
                   jax.ShapeDtypeStruct((G, T, IDX_HEADS), f32)],
        compiler_params=_params("parallel", "parallel"), name="idx_proj",
    )(h, w_idx, cos, sin)


def _sortable(x):
    bits = pltpu.bitcast(jnp.where(x == 0.0, 0.0, x), i32)
    return bits ^ ((bits >> 31) & 0x7FFFFFFF)


def _kth_largest_key(count_ge, shape, k):
    t0 = jnp.where(count_ge(jnp.zeros(shape, i32)) >= k, 0, INT_MIN).astype(i32)

    def body(i, t):
        cand = t + jnp.left_shift(jnp.int32(1), 30 - i)
        return jnp.where(count_ge(cand) >= k, cand, t)

    return lax.fori_loop(0, 31, body, t0)


def _tie_cutoff(count_tied_below, shape, need, nbits):
    def body(i, j):
        cand = j + jnp.left_shift(jnp.int32(1), nbits - 1 - i)
        return jnp.where(count_tied_below(cand) < need, cand, j)

    return lax.fori_loop(0, nbits, body, jnp.zeros(shape, i32))


TQ = 256
TK = 512


def _attn_prompt_kernel(q_ref, k_ref, v_ref, iq_ref, iw_ref, ik_ref, o_ref, keys_ref, m_ref, acc_ref, *, topk):
    qi = pl.program_id(1)
    nkc = (qi * TQ + TQ - 1) // TK + 1
    iw = iw_ref[...]
    qpos = lax.broadcasted_iota(i32, (TQ, TK), 0) + qi * TQ
    lane = lax.broadcasted_iota(i32, (TQ, TK), 1)

    def score_chunk(c, carry):
        ikc = ik_ref[pl.ds(pl.multiple_of(c * TK, TK), TK), :]
        sc = jnp.zeros((TQ, TK), f32)
        for h in range(IDX_HEADS):
            s = lax.dot_general(iq_ref[h], ikc, NT_DIMS, preferred_element_type=f32)
            sc = sc + jnp.maximum(s, 0.0) * iw[:, h:h + 1]
        sc = jnp.where(lane + c * TK <= qpos, sc, NEG)
        keys_ref[c] = _sortable(sc)
        return carry

    lax.fori_loop(0, nkc, score_chunk, 0)

    def fold(x):
        return sum(x[:, 128 * j:128 * (j + 1)] for j in range(1, TK // 128)) + x[:, 0:128]

    def count(pred):
        def body(c, acc):
            return acc + fold(jnp.where(pred(c, keys_ref[c]), 1, 0).astype(i32))
        acc = lax.fori_loop(0, nkc, body, jnp.zeros((TQ, 128), i32))
        return jnp.sum(acc, axis=1, keepdims=True)

    thr = _kth_largest_key(lambda cand: count(lambda c, key: key >= cand), (TQ, 1), topk)

    n_ge = count(lambda c, key: key >= thr)

    @pl.when(jnp.max(jnp.where(n_ge != topk, 1, 0)) > 0)
    def _():
        n_gt = count(lambda c, key: key > thr)
        need = topk - n_gt
        cut = _tie_cutoff(lambda j: count(lambda c, key: (key == thr) & (lane + c * TK < j)), (TQ, 1), need, 12)

        def demote(c, carry):
            key = keys_ref[c]
            keys_ref[c] = jnp.where((key == thr) & (lane + c * TK > cut), thr - 1, key)
            return carry

        lax.fori_loop(0, nkc, demote, 0)

    m_ref[...] = jnp.full(m_ref.shape, MASKED, f32)
    acc_ref[...] = jnp.zeros(acc_ref.shape, f32)
    ones = jnp.ones((TK, HEAD_DIM), bf16)

    def attn_chunk(c, carry):
        off = pl.multiple_of(c * TK, TK)
        sel = (keys_ref[c] >= thr) & (lane + c * TK <= qpos)
        for g in range(KV_HEADS):
            kc = k_ref[pl.ds(off, TK), HEAD_DIM * g:HEAD_DIM * (g + 1)]
            vc = jnp.concatenate([v_ref[pl.ds(off, TK), HEAD_DIM * g:HEAD_DIM * (g + 1)], ones], axis=1)
            qg = q_ref[Q_PER_KV * g:Q_PER_KV * (g + 1)].reshape(Q_PER_KV * TQ, HEAD_DIM)
            s_all = lax.dot_general(qg, kc, NT_DIMS, preferred_element_type=f32)
            ps, alphas = [], []
            for r in range(Q_PER_KV):
                hd = Q_PER_KV * g + r
                s = jnp.where(sel, s_all[TQ * r:TQ * (r + 1)], MASKED)
                m_old = m_ref[hd]
                m_new = jnp.maximum(m_old, jnp.max(s, axis=-1, keepdims=True))
                m_ref[hd] = m_new
                alphas.append(jnp.exp2(m_old - m_new))
                ps.append(jnp.exp2(s - jnp.concatenate([m_new] * (TK // 128), axis=1)).astype(bf16))
            pv = jnp.dot(jnp.concatenate(ps, axis=0), vc, preferred_element_type=f32)
            for r in range(Q_PER_KV):
                hd = Q_PER_KV * g + r
                acc_ref[hd] = jnp.concatenate([alphas[r]] * 2, axis=1) * acc_ref[hd] + pv[TQ * r:TQ * (r + 1)]
        return carry

    lax.fori_loop(0, nkc, attn_chunk, 0)

    for hd in range(N_HEADS):
        a = acc_ref[hd]
        o_ref[:, HEAD_DIM * hd:HEAD_DIM * (hd + 1)] = (a[:, :HEAD_DIM] / a[:, HEAD_DIM:]).astype(bf16)


def attention_prompt(q, kb, vb, iq, iw, ikb):
    G, _, T, _ = q.shape
    topk = min(TOPK_MAX, T // 4)
    return pl.pallas_call(
        functools.partial(_attn_prompt_kernel, topk=topk),
        grid=(G, T // TQ),
        in_specs=[pl.BlockSpec((None, N_HEADS, TQ, HEAD_DIM), lambda g, t: (g, 0, t, 0)),
                  pl.BlockSpec((None, T, KV_COLS), lambda g, t: (g, 0, 0)),
                  pl.BlockSpec((None, T, KV_COLS), lambda g, t: (g, 0, 0)),
                  pl.BlockSpec((None, IDX_HEADS, TQ, IDX_DIM), lambda g, t: (g, 0, t, 0)),
                  pl.BlockSpec((None, TQ, IDX_HEADS), lambda g, t: (g, t, 0)),
                  pl.BlockSpec((None, T, IDX_DIM), lambda g, t: (g, 0, 0))],
        out_specs=pl.BlockSpec((None, TQ, N_HEADS * HEAD_DIM), lambda g, t: (g, t, 0)),
        out_shape=jax.ShapeDtypeStruct((G, T, N_HEADS * HEAD_DIM), bf16),
        scratch_shapes=[pltpu.VMEM((T // TK, TQ, TK), i32),
                        pltpu.VMEM((N_HEADS, TQ, 128), f32),
                        pltpu.VMEM((N_HEADS, TQ, 2 * HEAD_DIM), f32)],
        compiler_params=_params("parallel", "arbitrary"), name="attn_prompt",
    )(q, kb, vb, iq, iw, ikb)


SCORE_PAGES = 16
PAGES_PER_STEP = 16
PAGE_ROWS = PAGE_SIZE * KV_HEADS


def _sample_score_kernel(pt_ref, iq_ref, iw_ref, ikn_ref, *refs):
    page_refs = refs[:SCORE_PAGES]
    sc_ref, snew_ref = refs[SCORE_PAGES:]
    iq = iq_ref[...]
    iw = iw_ref[...]
    iqb = iq.astype(bf16)
    for p in range(SCORE_PAGES):
        s = lax.dot_general(iqb, page_refs[p][...].astype(bf16), NT_DIMS, preferred_element_type=f32)
        sc_ref[p:p + 1, :] = jnp.sum(jnp.maximum(s, 0.0) * iw, axis=0, keepdims=True)
    s_new = jnp.sum(iq * ikn_ref[...], axis=1, keepdims=True)
    s_new = jnp.sum(jnp.maximum(s_new, 0.0) * iw, axis=0, keepdims=True)
    snew_ref[...] = jnp.broadcast_to(s_new, snew_ref.shape)


def sample_scores(page_table, iq_s, iw_s, ik_new, cache_ik):
    B, n_pages = page_table.shape

    def page_spec(p):
        return pl.BlockSpec((None, PAGE_SIZE, IDX_DIM), lambda b, j, pt: (pt[b, j * SCORE_PAGES + p], 0, 0))

    grid_spec = pltpu.PrefetchScalarGridSpec(
        num_scalar_prefetch=1,
        grid=(B, n_pages // SCORE_PAGES),
        in_specs=[pl.BlockSpec((None, IDX_HEADS, IDX_DIM), lambda b, j, pt: (b, 0, 0)),
                  pl.BlockSpec((None, IDX_HEADS, 1), lambda b, j, pt: (b, 0, 0)),
                  pl.BlockSpec((None, 1, IDX_DIM), lambda b, j, pt: (b, 0, 0))]
                 + [page_spec(p) for p in range(SCORE_PAGES)],
        out_specs=[pl.BlockSpec((None, SCORE_PAGES, PAGE_SIZE), lambda b, j, pt: (b, j, 0)),
                   pl.BlockSpec((None, 1, PAGE_SIZE), lambda b, j, pt: (b, 0, 0))],
    )
    return pl.pallas_call(
        _sample_score_kernel, grid_spec=grid_spec,
        out_shape=[jax.ShapeDtypeStruct((B, n_pages, PAGE_SIZE), f32), jax.ShapeDtypeStruct((B, 1, PAGE_SIZE), f32)],
        compiler_params=_params("parallel", "arbitrary"), name="sample_scores",
    )(page_table, iq_s, iw_s, ik_new, *([cache_ik] * SCORE_PAGES))


def _sample_select_kernel(sc_ref, snew_ref, sel_ref, seln_ref, *, topk):
    B, past = sc_ref.shape
    keys = _sortable(sc_ref[...])
    key_new = _sortable(snew_ref[:, 0:1])
    idx = lax.broadcasted_iota(i32, keys.shape, 1)
    idx_new = jnp.full((B, 1), past, i32)

    def count(pred):
        new = jnp.where(pred(key_new, idx_new), 1, 0).astype(i32)
        return jnp.sum(jnp.where(pred(keys, idx), 1, 0).astype(i32), axis=1, keepdims=True) + new

    thr = _kth_largest_key(lambda cand: count(lambda k, i: k >= cand), (B, 1), topk)
    need = topk - count(lambda k, i: k > thr)
    cut = _tie_cutoff(lambda jj: count(lambda k, i: (k == thr) & (i < jj)), (B, 1), need, int(past).bit_length())
    chosen = lambda k, i: (k > thr) | ((k == thr) & (i <= cut))
    seln_ref[...] = jnp.broadcast_to(jnp.where(chosen(key_new, idx_new), 0.0, MASKED), seln_ref.shape)
    spread = (lax.broadcasted_iota(i32, (PAGE_SIZE, PAGE_ROWS), 1) // KV_HEADS
              == lax.broadcasted_iota(i32, (PAGE_SIZE, PAGE_ROWS), 0))
    spread = jnp.where(spread, 1.0, 0.0).astype(bf16)
    flags = jnp.where(chosen(keys, idx), 1.0, 0.0).astype(bf16)
    for pg in range(past // PAGE_SIZE):
        rep = jnp.dot(flags[:, PAGE_SIZE * pg:PAGE_SIZE * (pg + 1)], spread, preferred_element_type=f32)
        sel_ref[:, PAGE_ROWS * pg:PAGE_ROWS * (pg + 1)] = jnp.where(rep > 0.5, 0.0, MASKED)


def sample_select(scores, score_new):
    B, n_pages, _ = scores.shape
    past = n_pages * PAGE_SIZE
    topk = min(TOPK_MAX, (past + 1) // 4)
    sel, sel_new = pl.pallas_call(
        functools.partial(_sample_select_kernel, topk=topk),
        out_shape=[jax.ShapeDtypeStruct((B, n_pages * PAGE_ROWS), f32), jax.ShapeDtypeStruct((B, PAGE_SIZE), f32)],
        compiler_params=pltpu.CompilerParams(vmem_limit_bytes=VMEM_LIMIT), name="sample_select",
    )(scores.reshape(B, past), score_new.reshape(B, PAGE_SIZE))
    return sel.reshape(B, n_pages, PAGE_ROWS), sel_new.reshape(B, 1, PAGE_SIZE)


def _sample_attn_kernel(pt_ref, q_ref, kn_ref, vn_ref, sel_ref, seln_ref, *refs):
    k_refs = refs[:PAGES_PER_STEP]
    v_refs = refs[PAGES_PER_STEP:2 * PAGES_PER_STEP]
    o_ref, m_ref, l_ref, acc_ref = refs[2 * PAGES_PER_STEP:]
    j = pl.program_id(1)
    qb = q_ref[...]
    head_kv = lax.broadcasted_iota(i32, (N_HEADS, PAGE_ROWS), 0) // Q_PER_KV
    row_kv = lax.broadcasted_iota(i32, (N_HEADS, PAGE_ROWS), 1) % KV_HEADS
    own_rows = jnp.where(head_kv == row_kv, 0.0, MASKED)

    @pl.when(j == 0)
    def _():
        head_kv_col = lax.broadcasted_iota(i32, (N_HEADS, HEAD_DIM), 0) // Q_PER_KV
        kn = jnp.zeros((N_HEADS, HEAD_DIM), f32)
        vn = jnp.zeros((N_HEADS, HEAD_DIM), f32)
        for g in range(KV_HEADS):
            cols = slice(HEAD_DIM * g, HEAD_DIM * (g + 1))
            kn = jnp.where(head_kv_col == g, kn_ref[:, cols], kn)
            vn = jnp.where(head_kv_col == g, vn_ref[:, cols], vn)
        m_ref[...] = jnp.sum(qb.astype(f32) * kn, axis=1, keepdims=True) + seln_ref[:, 0:1]
        l_ref[...] = jnp.ones(l_ref.shape, f32)
        acc_ref[...] = vn

    s = jnp.concatenate(
        [lax.dot_general(qb, k_refs[p][...].astype(bf16), NT_DIMS, preferred_element_type=f32)
         + sel_ref[p:p + 1, :] + own_rows for p in range(PAGES_PER_STEP)], axis=1)
    m_old = m_ref[...]
    m_new = jnp.maximum(m_old, jnp.max(s, axis=1, keepdims=True))
    p_ = jnp.exp2(s - m_new)
    alpha = jnp.exp2(m_old - m_new)
    l_ref[...] = alpha * l_ref[...] + jnp.sum(p_, axis=1, keepdims=True)
    v = jnp.concatenate([v_refs[p][...].astype(bf16) for p in range(PAGES_PER_STEP)], axis=0)
    acc_ref[...] = alpha * acc_ref[...] + jnp.dot(p_.astype(bf16), v, preferred_element_type=f32)
    m_ref[...] = m_new

    @pl.when(j == pl.num_programs(1) - 1)
    def _():
        o_ref[...] = (acc_ref[...] / l_ref[...]).astype(bf16)


def sample_attention(page_table, q_s, k_new, v_new, sel, sel_new, cache_k, cache_v):
    B, n_pages = page_table.shape
    steps = n_pages // PAGES_PER_STEP

    def page_spec(p):
        return pl.BlockSpec((None, PAGE_ROWS, HEAD_DIM), lambda b, j, pt: (pt[b, j * PAGES_PER_STEP + p], 0, 0))

    grid_spec = pltpu.PrefetchScalarGridSpec(
        num_scalar_prefetch=1,
        grid=(B, steps),
        in_specs=[pl.BlockSpec((None, N_HEADS, HEAD_DIM), lambda b, j, pt: (b, 0, 0)),
                  pl.BlockSpec((None, 1, KV_COLS), lambda b, j, pt: (b, 0, 0)),
                  pl.BlockSpec((None, 1, KV_COLS), lambda b, j, pt: (b, 0, 0)),
                  pl.BlockSpec((None, PAGES_PER_STEP, PAGE_ROWS), lambda b, j, pt: (b, j, 0)),
                  pl.BlockSpec((None, 1, PAGE_SIZE), lambda b, j, pt: (b, 0, 0))]
                 + [page_spec(p) for p in range(PAGES_PER_STEP)] * 2,
        out_specs=pl.BlockSpec((None, N_HEADS, HEAD_DIM), lambda b, j, pt: (b, 0, 0)),
        scratch_shapes=[pltpu.VMEM((N_HEADS, 1), f32), pltpu.VMEM((N_HEADS, 1), f32),
                        pltpu.VMEM((N_HEADS, HEAD_DIM), f32)],
    )
    return pl.pallas_call(
        _sample_attn_kernel, grid_spec=grid_spec,
        out_shape=jax.ShapeDtypeStruct((B, N_HEADS, HEAD_DIM), bf16),
        compiler_params=_params("parallel", "arbitrary"), name="sample_attn",
    )(page_table, q_s, k_new, v_new, sel, sel_new, *([cache_k] * PAGES_PER_STEP), *([cache_v] * PAGES_PER_STEP))


def _glu_kernel(x_ref, wa_ref, wg_ref, ba_ref, bg_ref, o_ref):
    x = x_ref[...]
    a = jnp.dot(x, wa_ref[...].astype(bf16), preferred_element_type=f32) + ba_ref[...]
    g = jnp.dot(x, wg_ref[...].astype(bf16), preferred_element_type=f32) + bg_ref[...]
    o_ref[...] = a * jax.nn.sigmoid(g)


def glu_projection(h, w_pw1, b_pw1):
    G, T, K = h.shape
    C = w_pw1.shape[1] // 2
    tm, tn = min(T, 1024), 512
    nj = C // tn
    b2 = b_pw1.reshape(1, 2 * C)
    return pl.pallas_call(
        _glu_kernel,
        grid=(G, T // tm, nj),
        in_specs=[pl.BlockSpec((None, tm, K), lambda g, t, j: (g, t, 0)),
                  pl.BlockSpec((K, tn), lambda g, t, j: (0, j)),
                  pl.BlockSpec((K, tn), lambda g, t, j: (0, nj + j)),
                  pl.BlockSpec((1, tn), lambda g, t, j: (0, j)),
                  pl.BlockSpec((1, tn), lambda g, t, j: (0, nj + j))],
        out_specs=pl.BlockSpec((None, tm, tn), lambda g, t, j: (g, t, j)),
        out_shape=jax.ShapeDtypeStruct((G, T, C), f32),
        compiler_params=_params("parallel", "parallel", "parallel"), name="glu_proj",
    )(h, w_pw1, w_pw1, b2, b2)


def _ln_silu(y, g, b):
    mu = jnp.mean(y, axis=-1, keepdims=True)
    var = jnp.mean(jnp.square(y - mu), axis=-1, keepdims=True)
    z = (y - mu) * lax.rsqrt(var + LN_EPS) * g + b
    return z * jax.nn.sigmoid(z)


CONV_HALO = 32


CONV_COLS = 256


def _conv_prompt_kernel(u_ref, halo_ref, w_ref, b_ref, g_ref, be_ref, o_ref, ext_ref, sh_ref, y_ref):
    t = pl.program_id(1)
    tt, C = u_ref.shape
    halo = halo_ref[...]
    ext_ref[0:CONV_HALO, :] = jnp.where(t == 0, jnp.zeros_like(halo), halo)
    ext_ref[CONV_HALO:, :] = u_ref[...]
    base = CONV_HALO - (CONV_WIDTH - 1)
    y_ref[...] = jnp.zeros(y_ref.shape, f32) + b_ref[...]
    for shift in range(8):
        taps = [(a, 8 * a + shift - base) for a in range(CONV_HALO // 8 + 1)
                if 0 <= 8 * a + shift - base < CONV_WIDTH]
        rows = 8 * taps[-1][0] + tt
        sh_ref[0:rows, :] = ext_ref[shift:shift + rows, :]
        for c in range(C // CONV_COLS):
            cols = slice(CONV_COLS * c, CONV_COLS * (c + 1))
            acc = y_ref[:, cols]
            for a, k in taps:
                acc = acc + sh_ref[8 * a:8 * a + tt, cols] * w_ref[k:k + 1, cols]
            y_ref[:, cols] = acc
    o_ref[...] = _ln_silu(y_ref[...], g_ref[...], be_ref[...]).astype(bf16)


def conv_prompt(u, w_dw, b_dw, ln_g, ln_b):
    G, T, C = u.shape
    tt = 128
    hb = tt // CONV_HALO
    row = pl.BlockSpec((1, C), lambda g, t: (0, 0))
    return pl.pallas_call(
        _conv_prompt_kernel,
        grid=(G, T // tt),
        in_specs=[pl.BlockSpec((None, tt, C), lambda g, t: (g, t, 0)),
                  pl.BlockSpec((None, CONV_HALO, C), lambda g, t: (g, jnp.maximum(t * hb - 1, 0), 0)),
                  pl.BlockSpec((CONV_WIDTH, C), lambda g, t: (0, 0)), row, row, row],
        out_specs=pl.BlockSpec((None, tt, C), lambda g, t: (g, t, 0)),
        out_shape=jax.ShapeDtypeStruct((G, T, C), bf16),
        scratch_shapes=[pltpu.VMEM((CONV_HALO + tt, C), f32), pltpu.VMEM((CONV_HALO + tt, C), f32),
                        pltpu.VMEM((tt, C), f32)],
        compiler_params=_params("parallel", "parallel"), name="conv_prompt",
    )(u, u, w_dw, b_dw.reshape(1, C), ln_g.reshape(1, C), ln_b.reshape(1, C))


def _conv_sample_kernel(u_ref, hist_ref, w_ref, b_ref, g_ref, be_ref, o_ref):
    w = w_ref[...]
    y = jnp.sum(hist_ref[...] * w[None, :CONV_WIDTH - 1, :], axis=1) + u_ref[...] * w[CONV_WIDTH - 1:, :] + b_ref[...]
    o_ref[...] = _ln_silu(y, g_ref[...], be_ref[...]).astype(bf16)


def conv_sample(u, hist, w_dw, b_dw, ln_g, ln_b):
    B, C = u.shape
    tc = 512
    col = lambda r: pl.BlockSpec((r, tc), lambda j: (0, j))
    return pl.pallas_call(
        _conv_sample_kernel,
        grid=(C // tc,),
        in_specs=[col(B), pl.BlockSpec((B, CONV_WIDTH - 1, tc), lambda j: (0, 0, j)), col(CONV_WIDTH),
                  col(1), col(1), col(1)],
        out_specs=col(B),
        out_shape=jax.ShapeDtypeStruct((B, C), bf16),
        compiler_params=_params("arbitrary"), name="conv_sample",
    )(u, hist, w_dw, b_dw.reshape(1, C), ln_g.reshape(1, C), ln_b.reshape(1, C))


def _top_values(x, n):
    vals = []
    for _ in range(n):
        m = jnp.max(x, axis=0, keepdims=True)
        vals.append(m)
        x = jnp.where(x == m, -jnp.inf, x)
    return vals


NOT_TOP = float(PEER_TOPK)


def _top_ranked(x, n):
    vals = []
    rank = jnp.full(x.shape, NOT_TOP, f32)
    for r in range(n):
        m = jnp.max(x, axis=0, keepdims=True)
        hit = x == m
        vals.append(m)
        rank = jnp.where(hit, float(r), rank)
        x = jnp.where(hit, -jnp.inf, x)
    return vals, rank


def _peer_gate_kernel(q_ref, sk_ref, cnt_ref, ea_ref, rb_ref, eb_ref):
    for h in range(PEER_HEADS):
        st = []
        for p in range(2):
            col = (2 * h + p) * PEER_NKEYS
            st.append(lax.dot_general(sk_ref[h, p].astype(bf16), q_ref[:, col:col + PEER_NKEYS], NT_DIMS,
                                      preferred_element_type=f32))
        a, b = st
        va, rank_a = _top_ranked(a, PEER_TOPK)
        vbl, rank_b = _top_ranked(b, PEER_TOPK)
        row = lax.broadcasted_iota(i32, (PEER_TOPK, a.shape[1]), 0)
        vb = jnp.zeros((PEER_TOPK, a.shape[1]), f32)
        for r in range(PEER_TOPK):
            vb = jnp.where(row == r, vbl[r], vb)
        cands = [va[r] + vb for r in range(PEER_TOPK)]
        row8 = lax.broadcasted_iota(i32, (8, a.shape[1]), 0)
        va_tail = jnp.zeros((8, a.shape[1]), f32)
        for r in range(8):
            va_tail = jnp.where(row8 == r, va[8 + r], va_tail)
        search = [cands[0]] + [cands[r][:8] for r in range(1, 8)] + [va_tail + vbl[0]]
        thr = _top_values(jnp.concatenate(search, axis=0), PEER_TOPK)[-1]
        top = va[0] + vbl[0]
        z = jnp.zeros_like(top)
        cnt = jnp.zeros(a.shape, f32)
        for r in range(PEER_TOPK):
            keep = cands[r] >= thr
            z = z + jnp.sum(jnp.where(keep, jnp.exp(cands[r] - top), 0.0), axis=0, keepdims=True)
            kept = jnp.sum(jnp.where(keep, 1.0, 0.0), axis=0, keepdims=True)
            cnt = jnp.where(rank_a == float(r), kept, cnt)
        ea = jnp.exp(a - va[0]) / z
        for grp in range(PEER_NKEYS // 8):
            cnt_ref[grp, h] = cnt[8 * grp:8 * (grp + 1), :]
            ea_ref[grp, h] = ea[8 * grp:8 * (grp + 1), :]
        eb = jnp.exp(b - vbl[0])
        for s in range(PEER_NKEYS // BF16_ROWS):
            rows = slice(BF16_ROWS * s, BF16_ROWS * (s + 1))
            rb_ref[h, s] = rank_b[rows, :].astype(bf16)
            eb_ref[h, s] = eb[rows, :].astype(bf16)


def peer_gates(q, sub_keys):
    G, T, _ = q.shape
    tm = min(T, 256)
    jt = PEER_NKEYS // BF16_ROWS
    a_spec = pl.BlockSpec((None, PEER_NKEYS // 8, PEER_HEADS, 8, tm), lambda g, t: (g, 0, 0, 0, t))
    b_spec = pl.BlockSpec((None, PEER_HEADS, jt, BF16_ROWS, tm), lambda g, t: (g, 0, 0, 0, t))
    a_shape = jax.ShapeDtypeStruct((G, PEER_NKEYS // 8, PEER_HEADS, 8, T), f32)
    b_shape = jax.ShapeDtypeStruct((G, PEER_HEADS, jt, BF16_ROWS, T), bf16)
    return pl.pallas_call(
        _peer_gate_kernel,
        grid=(G, T // tm),
        in_specs=[pl.BlockSpec((None, tm, q.shape[2]), lambda g, t: (g, t, 0)),
                  pl.BlockSpec(sub_keys.shape, lambda g, t: (0, 0, 0, 0))],
        out_specs=[a_spec, a_spec, b_spec, b_spec],
        out_shape=[a_shape, a_shape, b_shape, b_shape],
        compiler_params=_params("parallel", "parallel"), name="peer_gates",
    )(q, sub_keys)


PEER_ROWS = 8
PEER_TE = PEER_ROWS * PEER_NKEYS
PEER_OUT_COLS = 512
PEER_PASS = 4


def _gelu(x):
    return 0.5 * x * (1.0 + lax.erf(x * (2.0 ** -0.5)))


def _peer_expert_kernel(x_ref, u_ref, v_ref, cnt_ref, ea_ref, rb_ref, eb_ref, o_ref):
    @pl.when(pl.program_id(2) == 0)
    def _():
        o_ref[...] = jnp.zeros(o_ref.shape, f32)

    tm = x_ref.shape[0]
    jt = PEER_NKEYS // BF16_ROWS
    hid = lax.dot_general(x_ref[...], u_ref[...], NT_DIMS, preferred_element_type=f32)
    act = _gelu(hid)
    gates = [[None] * (tm // 128) for _ in range(PEER_ROWS)]
    for c in range(tm // 128):
        tok = slice(128 * c, 128 * (c + 1))
        for r0 in range(0, PEER_ROWS, PEER_PASS):
            gt = [jnp.zeros((jt, BF16_ROWS, 128), bf16) for _ in range(PEER_PASS)]
            for h in range(PEER_HEADS):
                rb = rb_ref[h, :, :, tok]
                eb = eb_ref[h, :, :, tok]
                for k in range(PEER_PASS):
                    r = r0 + k
                    cnt = jnp.broadcast_to(cnt_ref[h, r:r + 1, tok], (BF16_ROWS, 128)).astype(bf16)
                    eai = jnp.broadcast_to(ea_ref[h, r:r + 1, tok], (BF16_ROWS, 128)).astype(bf16)
                    gt[k] = gt[k] + eai[None] * jnp.minimum(eb, jnp.maximum(cnt[None] - rb, 0.0))
            for k in range(PEER_PASS):
                gates[r0 + k][c] = gt[k].astype(f32).reshape(PEER_NKEYS, 128).T
    blocks = [jnp.concatenate(gates[r], axis=0) * act[:, PEER_NKEYS * r:PEER_NKEYS * (r + 1)]
              for r in range(PEER_ROWS)]
    w = jnp.concatenate(blocks, axis=1).astype(bf16)
    for j in range(v_ref.shape[1] // PEER_OUT_COLS):
        cols = slice(PEER_OUT_COLS * j, PEER_OUT_COLS * (j + 1))
        o_ref[:, cols] += jnp.dot(w, v_ref[:, cols], preferred_element_type=f32)


def peer_experts(x, gates, u_tab, v_tab, layer):
    G, T, Dm = x.shape
    cnt, ea, rb, eb = gates
    tm = min(T, 512)
    a_spec = pl.BlockSpec((None, None, PEER_HEADS, 8, tm), lambda g, t, e: (g, e, 0, 0, t))
    b_spec = pl.BlockSpec((None, PEER_HEADS, PEER_NKEYS // BF16_ROWS, BF16_ROWS, tm), lambda g, t, e: (g, 0, 0, 0, t))
    tab_spec =pl.BlockSpec((None, PEER_TE, Dm), lambda g, t, e: (layer, e, 0))
    return pl.pallas_call(
        _peer_expert_kernel,
        grid=(G, T // tm, u_tab.shape[1] // PEER_TE),
        in_specs=[pl.BlockSpec((None, tm, Dm), lambda g, t, e: (g, t, 0)), tab_spec, tab_spec,
                  a_spec, a_spec, b_spec, b_spec],
        out_specs=pl.BlockSpec((None, tm, Dm), lambda g, t, e: (g, t, 0)),
        out_shape=jax.ShapeDtypeStruct((G, T, Dm), f32),
        compiler_params=_params("parallel", "parallel", "arbitrary"), name="peer_experts",
    )(x, u_tab, v_tab, cnt, ea, rb, eb)


def peer(h, w_q, sub_keys, u_tab, v_tab, layer):
    G, T, Dm = h.shape
    Tp = -(-T // 128) * 128
    if Tp != T:
        h = jnp.pad(h, ((0, 0), (0, Tp - T), (0, 0)))
    q = token_matmul(h, w_q, bf16)
    out = peer_experts(h, peer_gates(q, sub_keys), u_tab, v_tab, layer)
    return out[:, :T] if Tp != T else out


def _trunk(y, sample, mod, norm_g, attend, conv, weights):
    (w_in, w_idx, q_gain, k_gain, w_out, rope128, rope64, w_pw1, b_pw1, w_pw2, peer_w_q, peer_sub_keys, peer_u,
     peer_v) = weights
    nm = functools.partial(normmod, mod=mod, sample=sample)
    h = nm(y, norm_g=norm_g[0, 0], shift=(0, 0), scale=(0, 1))
    q = q_projection(h, w_in, q_gain, *rope128)
    k, kb, v, vb = kv_projection(h, w_in, k_gain, *rope128)
    iq, ik, ikb, iw = idx_projection(h, w_idx, *rope64)
    o = attend(q, k, kb, v, vb, iq, ik, ikb, iw)
    d = token_matmul(o, w_out, f32)
    y, h = nm(y, delta=d, gate=(0, 2), norm_g=norm_g[0, 1], shift=(0, 3), scale=(0, 4))
    d = peer(h, peer_w_q[0], peer_sub_keys[0], peer_u, peer_v, 0)
    y, h = nm(y, delta=d, gate=(0, 5), norm_g=norm_g[1, 0], shift=(1, 0), scale=(1, 1))
    u = glu_projection(h, w_pw1, b_pw1)
    c, state = conv(u)
    d = token_matmul(c, w_pw2, f32)
    y, h = nm(y, delta=d, gate=(1, 2), norm_g=norm_g[1, 1], shift=(1, 3), scale=(1, 4))
    d = peer(h, peer_w_q[1], peer_sub_keys[1], peer_u, peer_v, 1)
    y = nm(y, delta=d, gate=(1, 5))
    return y, k, v, ik, state


def kernel(x_prompt, x_sample, cache_k, cache_v, cache_idx_k, state_conv, page_table, c_prompt, c_sample, w_ada, b_ada, norm_g, attn_w_in, attn_q_gain, attn_k_gain, attn_w_out, conv_w_pw1, conv_b_pw1, conv_w_dw, conv_b_dw, conv_ln_g, conv_ln_b, conv_w_pw2, peer_w_q, peer_sub_keys, peer_u, peer_v):
    B, T, Dm = x_prompt.shape
    Bd = x_sample.shape[0]
    n_pages = page_table.shape[1]
    past = n_pages * PAGE_SIZE
    n_phys = cache_k.shape[1]

    c_all = jnp.concatenate([c_sample, c_prompt, jnp.zeros((ADA_ROWS - Bd - B, Dm), f32)], axis=0)
    mod = ada_modulation(c_all, w_ada, b_ada)
    ng = norm_g.reshape(norm_g.shape[0], 2, 1, Dm)

    peer_u_bf, peer_v_bf = peer_u.astype(bf16), peer_v.astype(bf16)
    w_in = attn_w_in[0]
    idx0 = N_HEADS * HEAD_DIM + 2 * KV_COLS
    w_idx = jnp.concatenate([w_in[:, idx0:], jnp.zeros((Dm, IDX_W_COLS - (w_in.shape[1] - idx0)), f32)], axis=1)

    def weights(pos):
        return (w_in, w_idx, attn_q_gain[0], attn_k_gain[0], attn_w_out[0], _rope_tables(pos, HEAD_DIM),
                _rope_tables(pos, IDX_DIM), conv_w_pw1[0], conv_b_pw1[0], conv_w_pw2[0], peer_w_q, peer_sub_keys,
                peer_u_bf, peer_v_bf)

    conv_tail = (conv_w_dw[0], conv_b_dw[0], conv_ln_g[0], conv_ln_b[0])

    def attend_prompt(q, k, kb, v, vb, iq, ik, ikb, iw):
        return attention_prompt(q, kb, vb, iq, iw, ikb)

    def conv_p(u):
        return conv_prompt(u, *conv_tail), u[:, T - (CONV_WIDTH - 1):]

    def attend_sample(q, k, kb, v, vb, iq, ik, ikb, iw):
        iq_s = jnp.transpose(iq[0], (1, 0, 2)).astype(f32)
        scores, score_new = sample_scores(page_table, iq_s, iw.reshape(Bd, IDX_HEADS, 1),
                                          ik.reshape(Bd, 1, IDX_DIM), cache_idx_k[0])
        sel, sel_new = sample_select(scores, score_new)
        o = sample_attention(page_table, jnp.transpose(q[0], (1, 0, 2)), k.reshape(Bd, 1, KV_COLS),
                             v.reshape(Bd, 1, KV_COLS), sel, sel_new,
                             cache_k[0].reshape(n_phys, PAGE_ROWS, HEAD_DIM), cache_v[0].reshape(n_phys, PAGE_ROWS, HEAD_DIM))
        return o.reshape(1, Bd, N_HEADS * HEAD_DIM)

    def conv_s(u):
        hist = state_conv[0]
        c = conv_sample(u[0], hist, *conv_tail)
        return c[None], jnp.concatenate([hist[:, 1:], u[0][:, None, :]], axis=1)

    yp, kp, vp, ikp, stp = _trunk(x_prompt, False, mod, ng, attend_prompt, conv_p,
                                  weights(jnp.arange(T, dtype=i32)))
    ys, ks, vs, iks, sts = _trunk(x_sample.reshape(1, Bd, Dm), True, mod, ng, attend_sample, conv_s,
                                  weights(jnp.full((Bd,), past, i32)))
    return (yp, ys.reshape(Bd, 1, Dm),
            kp.reshape(1, B, T, KV_HEADS, HEAD_DIM), vp.reshape(1, B, T, KV_HEADS, HEAD_DIM),
            ikp.reshape(1, B, T, IDX_DIM),
            ks.reshape(1, Bd, 1, KV_HEADS, HEAD_DIM), vs.reshape(1, Bd, 1, KV_HEADS, HEAD_DIM),
            iks.reshape(1, Bd, 1, IDX_DIM),
            stp[None], sts[None])
```

```python
import functools
import math

import numpy as np
import jax
import jax.numpy as jnp
from jax import lax
from jax.experimental import pallas as pl
from jax.experimental.pallas import tpu as pltpu

f32 = jnp.float32
bf16 = jnp.bfloat16
i32 = jnp.int32

D_MODEL = 2048
N_HEADS = 16
HEAD_DIM = 128
KV_HEADS = 4
Q_PER_KV = N_HEADS // KV_HEADS
KV_COLS = KV_HEADS * HEAD_DIM
IDX_HEADS = 16
IDX_DIM = 64
TOPK_MAX = 256
ROPE_THETA = 10000.0
PAGE_SIZE = 128
CONV_WIDTH = 31
PEER_HEADS = 8
PEER_NKEYS = 128
PEER_TOPK = 16
RMS_EPS = 1e-6
LN_EPS = 1e-5
NEG = -1e30
MASKED = -1e30
INT_MIN = -(2 ** 31)
Q_SCALE = HEAD_DIM ** -0.5 * math.log2(math.e)
VMEM_LIMIT = 56 * 1024 * 1024
BF16_ROWS = 16
ADA_ROWS = 40
ADA_PROMPT_ROW = 32

NT_DIMS = (((1,), (1,)), ((), ()))


def _params(*sem):
    return pltpu.CompilerParams(dimension_semantics=sem, vmem_limit_bytes=VMEM_LIMIT)


def _ada_kernel(c_ref, w_ref, b_ref, o_ref):
    c = c_ref[...]
    a = (c * jax.nn.sigmoid(c)).astype(bf16)
    o_ref[...] = jnp.dot(a, w_ref[...].astype(bf16), preferred_element_type=f32) + b_ref[...]


def ada_modulation(c_all, w_ada, b_ada):
    depth, d, _ = w_ada.shape
    tn = 1024
    nj = d // tn
    return pl.pallas_call(
        _ada_kernel,
        grid=(depth, 6, nj),
        in_specs=[
            pl.BlockSpec((ADA_ROWS, d), lambda l, k, j: (0, 0)),
            pl.BlockSpec((None, d, tn), lambda l, k, j: (l, 0, k * nj + j)),
            pl.BlockSpec((None, 1, tn), lambda l, k, j: (l, 0, k * nj + j)),
        ],
        out_specs=pl.BlockSpec((None, None, ADA_ROWS, tn), lambda l, k, j: (l, k, 0, j)),
        out_shape=jax.ShapeDtypeStruct((depth, 6, ADA_ROWS, d), f32),
        compiler_params=_params("parallel", "parallel", "parallel"),
        name="ada",
    )(c_all, w_ada, b_ada.reshape(depth, 1, 6 * d))


def _mod_operand(mod, sample):
    return mod if sample else mod.reshape(mod.shape[:3] + (1, mod.shape[3]))


def _mod_spec(sample, layer, k):
    if sample:
        return pl.BlockSpec((None, None, 32, D_MODEL), lambda g, t, *_: (layer, k, 0, 0))
    return pl.BlockSpec((None, None, None, 1, D_MODEL), lambda g, t, *_: (layer, k, ADA_PROMPT_ROW + g, 0, 0))


def _normmod_kernel(*refs, has_delta, want_h):
    refs = list(refs)
    y = refs.pop(0)[...]
    if has_delta:
        d_ref, gate_ref = refs.pop(0), refs.pop(0)
        y = y + gate_ref[...] * d_ref[...]
    if want_h:
        g_ref, sh_ref, sc_ref = refs.pop(0), refs.pop(0), refs.pop(0)
    if has_delta:
        refs.pop(0)[...] = y
    if want_h:
        ms = jnp.mean(y * y, axis=-1, keepdims=True)
        hn = y * lax.rsqrt(ms + RMS_EPS) * g_ref[...]
        refs.pop(0)[...] = (hn * (1.0 + sc_ref[...]) + sh_ref[...]).astype(bf16)


def normmod(y, mod, sample, *, delta=None, gate=None, norm_g=None, shift=None, scale=None):
    G, T, Dm = y.shape
    tm = min(T, 256)
    tok = pl.BlockSpec((None, tm, Dm), lambda g, t: (g, t, 0))
    modop = _mod_operand(mod, sample)
    ins, specs = [y], [tok]
    if delta is not None:
        ins += [delta, modop]
        specs += [tok, _mod_spec(sample, *gate)]
    if norm_g is not None:
        ins += [norm_g, modop, modop]
        specs += [pl.BlockSpec((1, Dm), lambda g, t: (0, 0)), _mod_spec(sample, *shift), _mod_spec(sample, *scale)]
    outs, ospecs = [], []
    if delta is not None:
        outs.append(jax.ShapeDtypeStruct((G, T, Dm), f32)); ospecs.append(tok)
    if norm_g is not None:
        outs.append(jax.ShapeDtypeStruct((G, T, Dm), bf16)); ospecs.append(tok)
    res = pl.pallas_call(
        functools.partial(_normmod_kernel, has_delta=delta is not None, want_h=norm_g is not None),
        grid=(G, T // tm), in_specs=specs, out_specs=ospecs, out_shape=outs,
        compiler_params=_params("parallel", "parallel"), name="normmod",
    )(*ins)
    return res if len(res) > 1 else res[0]


def _matmul_kernel(x_ref, w_ref, o_ref):
    o_ref[...] = jnp.dot(x_ref[...], w_ref[...].astype(bf16), preferred_element_type=f32).astype(o_ref.dtype)


def token_matmul(x, w, out_dtype):
    G, T, K = x.shape
    N = w.shape[1]
    tm, tn = min(T, 1024), 512
    return pl.pallas_call(
        _matmul_kernel,
        grid=(G, T // tm, N // tn),
        in_specs=[pl.BlockSpec((None, tm, K), lambda g, t, j: (g, t, 0)),
                  pl.BlockSpec((K, tn), lambda g, t, j: (0, j))],
        out_specs=pl.BlockSpec((None, tm, tn), lambda g, t, j: (g, t, j)),
        out_shape=jax.ShapeDtypeStruct((G, T, N), out_dtype),
        compiler_params=_params("parallel", "parallel", "parallel"), name="token_matmul",
    )(x, w)


def _rope_tables(pos, dim):
    half = dim // 2
    inv = ROPE_THETA ** (-jnp.arange(half, dtype=f32) / half)
    ang = pos.astype(f32)[:, None] * inv
    cos, sin = jnp.cos(ang), jnp.sin(ang)
    reps = 128 // dim
    return jnp.tile(jnp.concatenate([cos, cos], -1), (1, reps)), jnp.tile(jnp.concatenate([-sin, sin], -1), (1, reps))


def _head_rmsnorm(x, gain, ones):
    ss = jnp.dot((x * x).astype(bf16), ones, preferred_element_type=f32)
    return x * lax.rsqrt(ss * (1.0 / HEAD_DIM) + RMS_EPS) * gain


def _qproj_kernel(x_ref, w_ref, gain_ref, cos_ref, sin_ref, o_ref):
    acc = jnp.dot(x_ref[...], w_ref[...].astype(bf16), preferred_element_type=f32)
    cos, sin, gain = cos_ref[...], sin_ref[...], gain_ref[...]
    ones = jnp.ones((HEAD_DIM, HEAD_DIM), bf16)
    for h in range(4):
        xn = _head_rmsnorm(acc[:, HEAD_DIM * h:HEAD_DIM * (h + 1)], gain, ones)
        r = xn * cos + pltpu.roll(xn, HEAD_DIM // 2, 1) * sin
        o_ref[h] = (r * Q_SCALE).astype(bf16)


def q_projection(h, w_in, q_gain, cos, sin):
    G, T, K = h.shape
    tm = min(T, 1024)
    return pl.pallas_call(
        _qproj_kernel,
        grid=(G, T // tm, 4),
        in_specs=[pl.BlockSpec((None, tm, K), lambda g, t, j: (g, t, 0)),
                  pl.BlockSpec((K, 512), lambda g, t, j: (0, j)),
                  pl.BlockSpec((1, HEAD_DIM), lambda g, t, j: (0, 0)),
                  pl.BlockSpec((tm, 128), lambda g, t, j: (t, 0)),
                  pl.BlockSpec((tm, 128), lambda g, t, j: (t, 0))],
        out_specs=pl.BlockSpec((None, 4, tm, HEAD_DIM), lambda g, t, j: (g, j, t, 0)),
        out_shape=jax.ShapeDtypeStruct((G, N_HEADS, T, HEAD_DIM), bf16),
        compiler_params=_params("parallel", "parallel", "parallel"), name="q_proj",
    )(h, w_in, q_gain.reshape(1, HEAD_DIM), cos, sin)


def _kvproj_kernel(x_ref, wk_ref, wv_ref, gain_ref, cos_ref, sin_ref, k_ref, kb_ref, v_ref, vb_ref):
    x = x_ref[...]
    kacc = jnp.dot(x, wk_ref[...].astype(bf16), preferred_element_type=f32)
    cos, sin, gain = cos_ref[...], sin_ref[...], gain_ref[...]
    ones = jnp.ones((HEAD_DIM, HEAD_DIM), bf16)
    for h in range(KV_HEADS):
        sl = slice(HEAD_DIM * h, HEAD_DIM * (h + 1))
        xn = _head_rmsnorm(kacc[:, sl], gain, ones)
        r = xn * cos + pltpu.roll(xn, HEAD_DIM // 2, 1) * sin
        k_ref[:, sl] = r
        kb_ref[:, sl] = r.astype(bf16)
    v = jnp.dot(x, wv_ref[...].astype(bf16), preferred_element_type=f32)
    v_ref[...] = v
    vb_ref[...] = v.astype(bf16)


def kv_projection(h, w_in, k_gain, cos, sin):
    G, T, K = h.shape
    tm = min(T, 1024)
    kcol = (N_HEADS * HEAD_DIM) // KV_COLS
    tok = pl.BlockSpec((None, tm, KV_COLS), lambda g, t: (g, t, 0))
    return pl.pallas_call(
        _kvproj_kernel,
        grid=(G, T // tm),
        in_specs=[pl.BlockSpec((None, tm, K), lambda g, t: (g, t, 0)),
                  pl.BlockSpec((K, KV_COLS), lambda g, t: (0, kcol)),
                  pl.BlockSpec((K, KV_COLS), lambda g, t: (0, kcol + 1)),
                  pl.BlockSpec((1, HEAD_DIM), lambda g, t: (0, 0)),
                  pl.BlockSpec((tm, 128), lambda g, t: (t, 0)),
                  pl.BlockSpec((tm, 128), lambda g, t: (t, 0))],
        out_specs=[tok, tok, tok, tok],
        out_shape=[jax.ShapeDtypeStruct((G, T, KV_COLS), f32), jax.ShapeDtypeStruct((G, T, KV_COLS), bf16),
                   jax.ShapeDtypeStruct((G, T, KV_COLS), f32), jax.ShapeDtypeStruct((G, T, KV_COLS), bf16)],
        compiler_params=_params("parallel", "parallel"), name="kv_proj",
    )(h, w_in, w_in, k_gain.reshape(1, HEAD_DIM), cos, sin)


IDX_W_COLS = 1152


def _idxproj_kernel(x_ref, w_ref, cos_ref, sin_ref, iq_ref, ik_ref, ikb_ref, iw_ref):
    acc = jnp.dot(x_ref[...], w_ref[...].astype(bf16), preferred_element_type=f32)
    cos, sin = cos_ref[...], sin_ref[...]
    lane = lax.broadcasted_iota(i32, cos.shape, 1)
    first = (lane % IDX_DIM) < (IDX_DIM // 2)

    def rope64(x):
        partner = jnp.where(first, pltpu.roll(x, 128 - IDX_DIM // 2, 1), pltpu.roll(x, IDX_DIM // 2, 1))
        return x * cos + partner * sin

    for p in range(IDX_HEADS // 2):
        r = rope64(acc[:, 128 * p:128 * (p + 1)])
        iq_ref[2 * p] = r[:, :IDX_DIM].astype(bf16)
        iq_ref[2 * p + 1] = r[:, IDX_DIM:].astype(bf16)
    slab = acc[:, IDX_HEADS * IDX_DIM:]
    r = rope64(slab)[:, :IDX_DIM]
    ik_ref[...] = r
    ikb_ref[...] = r.astype(bf16)
    iw_ref[...] = slab[:, IDX_DIM:IDX_DIM + IDX_HEADS] * (IDX_HEADS ** -0.5)


def idx_projection(h, w_idx, cos, sin):
    G, T, K = h.shape
    tm = min(T, 512)
    return pl.pallas_call(
        _idxproj_kernel,
        grid=(G, T // tm),
        in_specs=[pl.BlockSpec((None, tm, K), lambda g, t: (g, t, 0)),
                  pl.BlockSpec((K, IDX_W_COLS), lambda g, t: (0, 0)),
                  pl.BlockSpec((tm, 128), lambda g, t: (t, 0)),
                  pl.BlockSpec((tm, 128), lambda g, t: (t, 0))],
        out_specs=[pl.BlockSpec((None, IDX_HEADS, tm, IDX_DIM), lambda g, t: (g, 0, t, 0)),
                   pl.BlockSpec((None, tm, IDX_DIM), lambda g, t: (g, t, 0)),
                   pl.BlockSpec((None, tm, IDX_DIM), lambda g, t: (g, t, 0)),
                   pl.BlockSpec((None, tm, IDX_HEADS), lambda g, t: (g, t, 0))],
        out_shape=[jax.ShapeDtypeStruct((G, IDX_HEADS, T, IDX_DIM), bf16),
                   jax.ShapeDtypeStruct((G, T, IDX_DIM), f32),
                   jax.ShapeDtypeStruct((G, T, IDX_DIM), bf16),
                   jax.ShapeDtypeStruct((G, T, IDX_HEADS), f32)],
        compiler_params=_params("parallel", "parallel"), name="idx_proj",
    )(h, w_idx, cos, sin)


def _sortable(x):
    bits = pltpu.bitcast(jnp.where(x == 0.0, 0.0, x), i32)
    return bits ^ ((bits >> 31) & 0x7FFFFFFF)


def _kth_largest_key(count_ge, shape, k):
    t0 = jnp.where(count_ge(jnp.zeros(shape, i32)) >= k, 0, INT_MIN).astype(i32)

    def body(i, t):
        cand = t + jnp.left_shift(jnp.int32(1), 30 - i)
        return jnp.where(count_ge(cand) >= k, cand, t)

    return lax.fori_loop(0, 31, body, t0)


def _tie_cutoff(count_tied_below, shape, need, nbits):
    def body(i, j):
        cand = j + jnp.left_shift(jnp.int32(1), nbits - 1 - i)
        return jnp.where(count_tied_below(cand) < need, cand, j)

    return lax.fori_loop(0, nbits, body, jnp.zeros(shape, i32))


TQ = 256
TK = 512


def _attn_prompt_kernel(q_ref, k_ref, v_ref, iq_ref, iw_ref, ik_ref, o_ref, keys_ref, m_ref, acc_ref, *, topk):
    qi = pl.program_id(1)
    nkc = (qi * TQ + TQ - 1) // TK + 1
    iw = iw_ref[...]
    qpos = lax.broadcasted_iota(i32, (TQ, TK), 0) + qi * TQ
    lane = lax.broadcasted_iota(i32, (TQ, TK), 1)

    def score_chunk(c, carry):
        ikc = ik_ref[pl.ds(pl.multiple_of(c * TK, TK), TK), :]
        sc = jnp.zeros((TQ, TK), f32)
        for h in range(IDX_HEADS):
            s = lax.dot_general(iq_ref[h], ikc, NT_DIMS, preferred_element_type=f32)
            sc = sc + jnp.maximum(s, 0.0) * iw[:, h:h + 1]
        sc = jnp.where(lane + c * TK <= qpos, sc, NEG)
        keys_ref[c] = _sortable(sc)
        return carry

    lax.fori_loop(0, nkc, score_chunk, 0)

    def fold(x):
        return sum(x[:, 128 * j:128 * (j + 1)] for j in range(1, TK // 128)) + x[:, 0:128]

    def count(pred):
        def body(c, acc):
            return acc + fold(jnp.where(pred(c, keys_ref[c]), 1, 0).astype(i32))
        acc = lax.fori_loop(0, nkc, body, jnp.zeros((TQ, 128), i32))
        return jnp.sum(acc, axis=1, keepdims=True)

    thr = _kth_largest_key(lambda cand: count(lambda c, key: key >= cand), (TQ, 1), topk)

    n_ge = count(lambda c, key: key >= thr)

    @pl.when(jnp.max(jnp.where(n_ge != topk, 1, 0)) > 0)
    def _():
        n_gt = count(lambda c, key: key > thr)
        need = topk - n_gt
        cut = _tie_cutoff(lambda j: count(lambda c, key: (key == thr) & (lane + c * TK < j)), (TQ, 1), need, 12)

        def demote(c, carry):
            key = keys_ref[c]
            keys_ref[c] = jnp.where((key == thr) & (lane + c * TK > cut), thr - 1, key)
            return carry

        lax.fori_loop(0, nkc, demote, 0)

    m_ref[...] = jnp.full(m_ref.shape, MASKED, f32)
    acc_ref[...] = jnp.zeros(acc_ref.shape, f32)
    ones = jnp.ones((TK, HEAD_DIM), bf16)

    def attn_chunk(c, carry):
        off = pl.multiple_of(c * TK, TK)
        sel = (keys_ref[c] >= thr) & (lane + c * TK <= qpos)
        for g in range(KV_HEADS):
            kc = k_ref[pl.ds(off, TK), HEAD_DIM * g:HEAD_DIM * (g + 1)]
            vc = jnp.concatenate([v_ref[pl.ds(off, TK), HEAD_DIM * g:HEAD_DIM * (g + 1)], ones], axis=1)
            qg = q_ref[Q_PER_KV * g:Q_PER_KV * (g + 1)].reshape(Q_PER_KV * TQ, HEAD_DIM)
            s_all = lax.dot_general(qg, kc, NT_DIMS, preferred_element_type=f32)
            ps, alphas = [], []
            for r in range(Q_PER_KV):
                hd = Q_PER_KV * g + r
                s = jnp.where(sel, s_all[TQ * r:TQ * (r + 1)], MASKED)
                m_old = m_ref[hd]
                m_new = jnp.maximum(m_old, jnp.max(s, axis=-1, keepdims=True))
                m_ref[hd] = m_new
                alphas.append(jnp.exp2(m_old - m_new))
                ps.append(jnp.exp2(s - jnp.concatenate([m_new] * (TK // 128), axis=1)).astype(bf16))
            pv = jnp.dot(jnp.concatenate(ps, axis=0), vc, preferred_element_type=f32)
            for r in range(Q_PER_KV):
                hd = Q_PER_KV * g + r
                acc_ref[hd] = jnp.concatenate([alphas[r]] * 2, axis=1) * acc_ref[hd] + pv[TQ * r:TQ * (r + 1)]
        return carry

    lax.fori_loop(0, nkc, attn_chunk, 0)

    for hd in range(N_HEADS):
        a = acc_ref[hd]
        o_ref[:, HEAD_DIM * hd:HEAD_DIM * (hd + 1)] = (a[:, :HEAD_DIM] / a[:, HEAD_DIM:]).astype(bf16)


def attention_prompt(q, kb, vb, iq, iw, ikb):
    G, _, T, _ = q.shape
    topk = min(TOPK_MAX, T // 4)
    return pl.pallas_call(
        functools.partial(_attn_prompt_kernel, topk=topk),
        grid=(G, T // TQ),
        in_specs=[pl.BlockSpec((None, N_HEADS, TQ, HEAD_DIM), lambda g, t: (g, 0, t, 0)),
                  pl.BlockSpec((None, T, KV_COLS), lambda g, t: (g, 0, 0)),
                  pl.BlockSpec((None, T, KV_COLS), lambda g, t: (g, 0, 0)),
                  pl.BlockSpec((None, IDX_HEADS, TQ, IDX_DIM), lambda g, t: (g, 0, t, 0)),
                  pl.BlockSpec((None, TQ, IDX_HEADS), lambda g, t: (g, t, 0)),
                  pl.BlockSpec((None, T, IDX_DIM), lambda g, t: (g, 0, 0))],
        out_specs=pl.BlockSpec((None, TQ, N_HEADS * HEAD_DIM), lambda g, t: (g, t, 0)),
        out_shape=jax.ShapeDtypeStruct((G, T, N_HEADS * HEAD_DIM), bf16),
        scratch_shapes=[pltpu.VMEM((T // TK, TQ, TK), i32),
                        pltpu.VMEM((N_HEADS, TQ, 128), f32),
                        pltpu.VMEM((N_HEADS, TQ, 2 * HEAD_DIM), f32)],
        compiler_params=_params("parallel", "arbitrary"), name="attn_prompt",
    )(q, kb, vb, iq, iw, ikb)


SCORE_PAGES = 16
PAGES_PER_STEP = 16
PAGE_ROWS = PAGE_SIZE * KV_HEADS


def _sample_score_kernel(pt_ref, iq_ref, iw_ref, ikn_ref, *refs):
    page_refs = refs[:SCORE_PAGES]
    sc_ref, snew_ref = refs[SCORE_PAGES:]
    iq = iq_ref[...]
    iw = iw_ref[...]
    iqb = iq.astype(bf16)
    for p in range(SCORE_PAGES):
        s = lax.dot_general(iqb, page_refs[p][...].astype(bf16), NT_DIMS, preferred_element_type=f32)
        sc_ref[p:p + 1, :] = jnp.sum(jnp.maximum(s, 0.0) * iw, axis=0, keepdims=True)
    s_new = jnp.sum(iq * ikn_ref[...], axis=1, keepdims=True)
    s_new = jnp.sum(jnp.maximum(s_new, 0.0) * iw, axis=0, keepdims=True)
    snew_ref[...] = jnp.broadcast_to(s_new, snew_ref.shape)


def sample_scores(page_table, iq_s, iw_s, ik_new, cache_ik):
    B, n_pages = page_table.shape

    def page_spec(p):
        return pl.BlockSpec((None, PAGE_SIZE, IDX_DIM), lambda b, j, pt: (pt[b, j * SCORE_PAGES + p], 0, 0))

    grid_spec = pltpu.PrefetchScalarGridSpec(
        num_scalar_prefetch=1,
        grid=(B, n_pages // SCORE_PAGES),
        in_specs=[pl.BlockSpec((None, IDX_HEADS, IDX_DIM), lambda b, j, pt: (b, 0, 0)),
                  pl.BlockSpec((None, IDX_HEADS, 1), lambda b, j, pt: (b, 0, 0)),
                  pl.BlockSpec((None, 1, IDX_DIM), lambda b, j, pt: (b, 0, 0))]
                 + [page_spec(p) for p in range(SCORE_PAGES)],
        out_specs=[pl.BlockSpec((None, SCORE_PAGES, PAGE_SIZE), lambda b, j, pt: (b, j, 0)),
                   pl.BlockSpec((None, 1, PAGE_SIZE), lambda b, j, pt: (b, 0, 0))],
    )
    return pl.pallas_call(
        _sample_score_kernel, grid_spec=grid_spec,
        out_shape=[jax.ShapeDtypeStruct((B, n_pages, PAGE_SIZE), f32), jax.ShapeDtypeStruct((B, 1, PAGE_SIZE), f32)],
        compiler_params=_params("parallel", "arbitrary"), name="sample_scores",
    )(page_table, iq_s, iw_s, ik_new, *([cache_ik] * SCORE_PAGES))


def _sample_select_kernel(sc_ref, snew_ref, sel_ref, seln_ref, *, topk):
    B, past = sc_ref.shape
    keys = _sortable(sc_ref[...])
    key_new = _sortable(snew_ref[:, 0:1])
    idx = lax.broadcasted_iota(i32, keys.shape, 1)
    idx_new = jnp.full((B, 1), past, i32)

    def count(pred):
        new = jnp.where(pred(key_new, idx_new), 1, 0).astype(i32)
        return jnp.sum(jnp.where(pred(keys, idx), 1, 0).astype(i32), axis=1, keepdims=True) + new

    thr = _kth_largest_key(lambda cand: count(lambda k, i: k >= cand), (B, 1), topk)
    need = topk - count(lambda k, i: k > thr)
    cut = _tie_cutoff(lambda jj: count(lambda k, i: (k == thr) & (i < jj)), (B, 1), need, int(past).bit_length())
    chosen = lambda k, i: (k > thr) | ((k == thr) & (i <= cut))
    seln_ref[...] = jnp.broadcast_to(jnp.where(chosen(key_new, idx_new), 0.0, MASKED), seln_ref.shape)
    spread = (lax.broadcasted_iota(i32, (PAGE_SIZE, PAGE_ROWS), 1) // KV_HEADS
              == lax.broadcasted_iota(i32, (PAGE_SIZE, PAGE_ROWS), 0))
    spread = jnp.where(spread, 1.0, 0.0).astype(bf16)
    flags = jnp.where(chosen(keys, idx), 1.0, 0.0).astype(bf16)
    for pg in range(past // PAGE_SIZE):
        rep = jnp.dot(flags[:, PAGE_SIZE * pg:PAGE_SIZE * (pg + 1)], spread, preferred_element_type=f32)
        sel_ref[:, PAGE_ROWS * pg:PAGE_ROWS * (pg + 1)] = jnp.where(rep > 0.5, 0.0, MASKED)


def sample_select(scores, score_new):
    B, n_pages, _ = scores.shape
    past = n_pages * PAGE_SIZE
    topk = min(TOPK_MAX, (past + 1) // 4)
    sel, sel_new = pl.pallas_call(
        functools.partial(_sample_select_kernel, topk=topk),
        out_shape=[jax.ShapeDtypeStruct((B, n_pages * PAGE_ROWS), f32), jax.ShapeDtypeStruct((B, PAGE_SIZE), f32)],
        compiler_params=pltpu.CompilerParams(vmem_limit_bytes=VMEM_LIMIT), name="sample_select",
    )(scores.reshape(B, past), score_new.reshape(B, PAGE_SIZE))
    return sel.reshape(B, n_pages, PAGE_ROWS), sel_new.reshape(B, 1, PAGE_SIZE)


def _sample_attn_kernel(pt_ref, q_ref, kn_ref, vn_ref, sel_ref, seln_ref, *refs):
    k_refs = refs[:PAGES_PER_STEP]
    v_refs = refs[PAGES_PER_STEP:2 * PAGES_PER_STEP]
    o_ref, m_ref, l_ref, acc_ref = refs[2 * PAGES_PER_STEP:]
    j = pl.program_id(1)
    qb = q_ref[...]
    head_kv = lax.broadcasted_iota(i32, (N_HEADS, PAGE_ROWS), 0) // Q_PER_KV
    row_kv = lax.broadcasted_iota(i32, (N_HEADS, PAGE_ROWS), 1) % KV_HEADS
    own_rows = jnp.where(head_kv == row_kv, 0.0, MASKED)

    @pl.when(j == 0)
    def _():
        head_kv_col = lax.broadcasted_iota(i32, (N_HEADS, HEAD_DIM), 0) // Q_PER_KV
        kn = jnp.zeros((N_HEADS, HEAD_DIM), f32)
        vn = jnp.zeros((N_HEADS, HEAD_DIM), f32)
        for g in range(KV_HEADS):
            cols = slice(HEAD_DIM * g, HEAD_DIM * (g + 1))
            kn = jnp.where(head_kv_col == g, kn_ref[:, cols], kn)
            vn = jnp.where(head_kv_col == g, vn_ref[:, cols], vn)
        m_ref[...] = jnp.sum(qb.astype(f32) * kn, axis=1, keepdims=True) + seln_ref[:, 0:1]
        l_ref[...] = jnp.ones(l_ref.shape, f32)
        acc_ref[...] = vn

    s = jnp.concatenate(
        [lax.dot_general(qb, k_refs[p][...].astype(bf16), NT_DIMS, preferred_element_type=f32)
         + sel_ref[p:p + 1, :] + own_rows for p in range(PAGES_PER_STEP)], axis=1)
    m_old = m_ref[...]
    m_new = jnp.maximum(m_old, jnp.max(s, axis=1, keepdims=True))
    p_ = jnp.exp2(s - m_new)
    alpha = jnp.exp2(m_old - m_new)
    l_ref[...] = alpha * l_ref[...] + jnp.sum(p_, axis=1, keepdims=True)
    v = jnp.concatenate([v_refs[p][...].astype(bf16) for p in range(PAGES_PER_STEP)], axis=0)
    acc_ref[...] = alpha * acc_ref[...] + jnp.dot(p_.astype(bf16), v, preferred_element_type=f32)
    m_ref[...] = m_new

    @pl.when(j == pl.num_programs(1) - 1)
    def _():
        o_ref[...] = (acc_ref[...] / l_ref[...]).astype(bf16)


def sample_attention(page_table, q_s, k_new, v_new, sel, sel_new, cache_k, cache_v):
    B, n_pages = page_table.shape
    steps = n_pages // PAGES_PER_STEP

    def page_spec(p):
        return pl.BlockSpec((None, PAGE_ROWS, HEAD_DIM), lambda b, j, pt: (pt[b, j * PAGES_PER_STEP + p], 0, 0))

    grid_spec = pltpu.PrefetchScalarGridSpec(
        num_scalar_prefetch=1,
        grid=(B, steps),
        in_specs=[pl.BlockSpec((None, N_HEADS, HEAD_DIM), lambda b, j, pt: (b, 0, 0)),
                  pl.BlockSpec((None, 1, KV_COLS), lambda b, j, pt: (b, 0, 0)),
                  pl.BlockSpec((None, 1, KV_COLS), lambda b, j, pt: (b, 0, 0)),
                  pl.BlockSpec((None, PAGES_PER_STEP, PAGE_ROWS), lambda b, j, pt: (b, j, 0)),
                  pl.BlockSpec((None, 1, PAGE_SIZE), lambda b, j, pt: (b, 0, 0))]
                 + [page_spec(p) for p in range(PAGES_PER_STEP)] * 2,
        out_specs=pl.BlockSpec((None, N_HEADS, HEAD_DIM), lambda b, j, pt: (b, 0, 0)),
        scratch_shapes=[pltpu.VMEM((N_HEADS, 1), f32), pltpu.VMEM((N_HEADS, 1), f32),
                        pltpu.VMEM((N_HEADS, HEAD_DIM), f32)],
    )
    return pl.pallas_call(
        _sample_attn_kernel, grid_spec=grid_spec,
        out_shape=jax.ShapeDtypeStruct((B, N_HEADS, HEAD_DIM), bf16),
        compiler_params=_params("parallel", "arbitrary"), name="sample_attn",
    )(page_table, q_s, k_new, v_new, sel, sel_new, *([cache_k] * PAGES_PER_STEP), *([cache_v] * PAGES_PER_STEP))


def _glu_kernel(x_ref, wa_ref, wg_ref, ba_ref, bg_ref, o_ref):
    x = x_ref[...]
    a = jnp.dot(x, wa_ref[...].astype(bf16), preferred_element_type=f32) + ba_ref[...]
    g = jnp.dot(x, wg_ref[...].astype(bf16), preferred_element_type=f32) + bg_ref[...]
    o_ref[...] = a * jax.nn.sigmoid(g)


def glu_projection(h, w_pw1, b_pw1):
    G, T, K = h.shape
    C = w_pw1.shape[1] // 2
    tm, tn = min(T, 1024), 512
    nj = C // tn
    b2 = b_pw1.reshape(1, 2 * C)
    return pl.pallas_call(
        _glu_kernel,
        grid=(G, T // tm, nj),
        in_specs=[pl.BlockSpec((None, tm, K), lambda g, t, j: (g, t, 0)),
                  pl.BlockSpec((K, tn), lambda g, t, j: (0, j)),
                  pl.BlockSpec((K, tn), lambda g, t, j: (0, nj + j)),
                  pl.BlockSpec((1, tn), lambda g, t, j: (0, j)),
                  pl.BlockSpec((1, tn), lambda g, t, j: (0, nj + j))],
        out_specs=pl.BlockSpec((None, tm, tn), lambda g, t, j: (g, t, j)),
        out_shape=jax.ShapeDtypeStruct((G, T, C), f32),
        compiler_params=_params("parallel", "parallel", "parallel"), name="glu_proj",
    )(h, w_pw1, w_pw1, b2, b2)


def _ln_silu(y, g, b):
    mu = jnp.mean(y, axis=-1, keepdims=True)
    var = jnp.mean(jnp.square(y - mu), axis=-1, keepdims=True)
    z = (y - mu) * lax.rsqrt(var + LN_EPS) * g + b
    return z * jax.nn.sigmoid(z)


CONV_HALO = 32


CONV_COLS = 256


def _conv_prompt_kernel(u_ref, halo_ref, w_ref, b_ref, g_ref, be_ref, o_ref, ext_ref, sh_ref, y_ref):
    t = pl.program_id(1)
    tt, C = u_ref.shape
    halo = halo_ref[...]
    ext_ref[0:CONV_HALO, :] = jnp.where(t == 0, jnp.zeros_like(halo), halo)
    ext_ref[CONV_HALO:, :] = u_ref[...]
    base = CONV_HALO - (CONV_WIDTH - 1)
    y_ref[...] = jnp.zeros(y_ref.shape, f32) + b_ref[...]
    for shift in range(8):
        taps = [(a, 8 * a + shift - base) for a in range(CONV_HALO // 8 + 1)
                if 0 <= 8 * a + shift - base < CONV_WIDTH]
        rows = 8 * taps[-1][0] + tt
        sh_ref[0:rows, :] = ext_ref[shift:shift + rows, :]
        for c in range(C // CONV_COLS):
            cols = slice(CONV_COLS * c, CONV_COLS * (c + 1))
            acc = y_ref[:, cols]
            for a, k in taps:
                acc = acc + sh_ref[8 * a:8 * a + tt, cols] * w_ref[k:k + 1, cols]
            y_ref[:, cols] = acc
    o_ref[...] = _ln_silu(y_ref[...], g_ref[...], be_ref[...]).astype(bf16)


def conv_prompt(u, w_dw, b_dw, ln_g, ln_b):
    G, T, C = u.shape
    tt = 128
    hb = tt // CONV_HALO
    row = pl.BlockSpec((1, C), lambda g, t: (0, 0))
    return pl.pallas_call(
        _conv_prompt_kernel,
        grid=(G, T // tt),
        in_specs=[pl.BlockSpec((None, tt, C), lambda g, t: (g, t, 0)),
                  pl.BlockSpec((None, CONV_HALO, C), lambda g, t: (g, jnp.maximum(t * hb - 1, 0), 0)),
                  pl.BlockSpec((CONV_WIDTH, C), lambda g, t: (0, 0)), row, row, row],
        out_specs=pl.BlockSpec((None, tt, C), lambda g, t: (g, t, 0)),
        out_shape=jax.ShapeDtypeStruct((G, T, C), bf16),
        scratch_shapes=[pltpu.VMEM((CONV_HALO + tt, C), f32), pltpu.VMEM((CONV_HALO + tt, C), f32),
                        pltpu.VMEM((tt, C), f32)],
        compiler_params=_params("parallel", "parallel"), name="conv_prompt",
    )(u, u, w_dw, b_dw.reshape(1, C), ln_g.reshape(1, C), ln_b.reshape(1, C))


def _conv_sample_kernel(u_ref, hist_ref, w_ref, b_ref, g_ref, be_ref, o_ref):
    w = w_ref[...]
    y = jnp.sum(hist_ref[...] * w[None, :CONV_WIDTH - 1, :], axis=1) + u_ref[...] * w[CONV_WIDTH - 1:, :] + b_ref[...]
    o_ref[...] = _ln_silu(y, g_ref[...], be_ref[...]).astype(bf16)


def conv_sample(u, hist, w_dw, b_dw, ln_g, ln_b):
    B, C = u.shape
    tc = 512
    col = lambda r: pl.BlockSpec((r, tc), lambda j: (0, j))
    return pl.pallas_call(
        _conv_sample_kernel,
        grid=(C // tc,),
        in_specs=[col(B), pl.BlockSpec((B, CONV_WIDTH - 1, tc), lambda j: (0, 0, j)), col(CONV_WIDTH),
                  col(1), col(1), col(1)],
        out_specs=col(B),
        out_shape=jax.ShapeDtypeStruct((B, C), bf16),
        compiler_params=_params("arbitrary"), name="conv_sample",
    )(u, hist, w_dw, b_dw.reshape(1, C), ln_g.reshape(1, C), ln_b.reshape(1, C))


def _top_values(x, n):
    vals = []
    for _ in range(n):
        m = jnp.max(x, axis=0, keepdims=True)
        vals.append(m)
        x = jnp.where(x == m, -jnp.inf, x)
    return vals


NOT_TOP = float(PEER_TOPK)


def _top_ranked(x, n):
    vals = []
    rank = jnp.full(x.shape, NOT_TOP, f32)
    for r in range(n):
        m = jnp.max(x, axis=0, keepdims=True)
        hit = x == m
        vals.append(m)
        rank = jnp.where(hit, float(r), rank)
        x = jnp.where(hit, -jnp.inf, x)
    return vals, rank


def _peer_gate_kernel(q_ref, sk_ref, cnt_ref, ea_ref, rb_ref, eb_ref):
    for h in range(PEER_HEADS):
        st = []
        for p in range(2):
            col = (2 * h + p) * PEER_NKEYS
            st.append(lax.dot_general(sk_ref[h, p].astype(bf16), q_ref[:, col:col + PEER_NKEYS], NT_DIMS,
                                      preferred_element_type=f32))
        a, b = st
        va, rank_a = _top_ranked(a, PEER_TOPK)
        vbl, rank_b = _top_ranked(b, PEER_TOPK)
        row = lax.broadcasted_iota(i32, (PEER_TOPK, a.shape[1]), 0)
        vb = jnp.zeros((PEER_TOPK, a.shape[1]), f32)
        for r in range(PEER_TOPK):
            vb = jnp.where(row == r, vbl[r], vb)
        cands = [va[r] + vb for r in range(PEER_TOPK)]
        row8 = lax.broadcasted_iota(i32, (8, a.shape[1]), 0)
        va_tail = jnp.zeros((8, a.shape[1]), f32)
        for r in range(8):
            va_tail = jnp.where(row8 == r, va[8 + r], va_tail)
        search = [cands[0]] + [cands[r][:8] for r in range(1, 8)] + [va_tail + vbl[0]]
        thr = _top_values(jnp.concatenate(search, axis=0), PEER_TOPK)[-1]
        top = va[0] + vbl[0]
        z = jnp.zeros_like(top)
        cnt = jnp.zeros(a.shape, f32)
        for r in range(PEER_TOPK):
            keep = cands[r] >= thr
            z = z + jnp.sum(jnp.where(keep, jnp.exp(cands[r] - top), 0.0), axis=0, keepdims=True)
            kept = jnp.sum(jnp.where(keep, 1.0, 0.0), axis=0, keepdims=True)
            cnt = jnp.where(rank_a == float(r), kept, cnt)
        ea = jnp.exp(a - va[0]) / z
        for grp in range(PEER_NKEYS // 8):
            cnt_ref[grp, h] = cnt[8 * grp:8 * (grp + 1), :]
            ea_ref[grp, h] = ea[8 * grp:8 * (grp + 1), :]
        eb = jnp.exp(b - vbl[0])
        for s in range(PEER_NKEYS // BF16_ROWS):
            rows = slice(BF16_ROWS * s, BF16_ROWS * (s + 1))
            rb_ref[h, s] = rank_b[rows, :].astype(bf16)
            eb_ref[h, s] = eb[rows, :].astype(bf16)


def peer_gates(q, sub_keys):
    G, T, _ = q.shape
    tm = min(T, 256)
    jt = PEER_NKEYS // BF16_ROWS
    a_spec = pl.BlockSpec((None, PEER_NKEYS // 8, PEER_HEADS, 8, tm), lambda g, t: (g, 0, 0, 0, t))
    b_spec = pl.BlockSpec((None, PEER_HEADS, jt, BF16_ROWS, tm), lambda g, t: (g, 0, 0, 0, t))
    a_shape = jax.ShapeDtypeStruct((G, PEER_NKEYS // 8, PEER_HEADS, 8, T), f32)
    b_shape = jax.ShapeDtypeStruct((G, PEER_HEADS, jt, BF16_ROWS, T), bf16)
    return pl.pallas_call(
        _peer_gate_kernel,
        grid=(G, T // tm),
        in_specs=[pl.BlockSpec((None, tm, q.shape[2]), lambda g, t: (g, t, 0)),
                  pl.BlockSpec(sub_keys.shape, lambda g, t: (0, 0, 0, 0))],
        out_specs=[a_spec, a_spec, b_spec, b_spec],
        out_shape=[a_shape, a_shape, b_shape, b_shape],
        compiler_params=_params("parallel", "parallel"), name="peer_gates",
    )(q, sub_keys)


PEER_ROWS = 8
PEER_TE = PEER_ROWS * PEER_NKEYS
PEER_OUT_COLS = 512
PEER_PASS = 4


def _gelu(x):
    return 0.5 * x * (1.0 + lax.erf(x * (2.0 ** -0.5)))


def _peer_expert_kernel(x_ref, u_ref, v_ref, cnt_ref, ea_ref, rb_ref, eb_ref, o_ref):
    @pl.when(pl.program_id(2) == 0)
    def _():
        o_ref[...] = jnp.zeros(o_ref.shape, f32)

    tm = x_ref.shape[0]
    jt = PEER_NKEYS // BF16_ROWS
    hid = lax.dot_general(x_ref[...], u_ref[...], NT_DIMS, preferred_element_type=f32)
    act = _gelu(hid)
    gates = [[None] * (tm // 128) for _ in range(PEER_ROWS)]
    for c in range(tm // 128):
        tok = slice(128 * c, 128 * (c + 1))
        for r0 in range(0, PEER_ROWS, PEER_PASS):
            gt = [jnp.zeros((jt, BF16_ROWS, 128), bf16) for _ in range(PEER_PASS)]
            for h in range(PEER_HEADS):
                rb = rb_ref[h, :, :, tok]
                eb = eb_ref[h, :, :, tok]
                for k in range(PEER_PASS):
                    r = r0 + k
                    cnt = jnp.broadcast_to(cnt_ref[h, r:r + 1, tok], (BF16_ROWS, 128)).astype(bf16)
                    eai = jnp.broadcast_to(ea_ref[h, r:r + 1, tok], (BF16_ROWS, 128)).astype(bf16)
                    active = jnp.minimum(jnp.maximum(cnt[None] - rb, 0.0), 1.0)
                    gt[k] = gt[k] + active * (eai[None] * eb)
            for k in range(PEER_PASS):
                gates[r0 + k][c] = gt[k].astype(f32).reshape(PEER_NKEYS, 128).T
    blocks = [jnp.concatenate(gates[r], axis=0) * act[:, PEER_NKEYS * r:PEER_NKEYS * (r + 1)]
              for r in range(PEER_ROWS)]
    w = jnp.concatenate(blocks, axis=1).astype(bf16)
    for j in range(v_ref.shape[1] // PEER_OUT_COLS):
        cols = slice(PEER_OUT_COLS * j, PEER_OUT_COLS * (j + 1))
        o_ref[:, cols] += jnp.dot(w, v_ref[:, cols], preferred_element_type=f32)


def peer_experts(x, gates, u_tab, v_tab, layer):
    G, T, Dm = x.shape
    cnt, ea, rb, eb = gates
    tm = min(T, 512)
    a_spec = pl.BlockSpec((None, None, PEER_HEADS, 8, tm), lambda g, t, e: (g, e, 0, 0, t))
    b_spec = pl.BlockSpec((None, PEER_HEADS, PEER_NKEYS // BF16_ROWS, BF16_ROWS, tm), lambda g, t, e: (g, 0, 0, 0, t))
    tab_spec =pl.BlockSpec((None, PEER_TE, Dm), lambda g, t, e: (layer, e, 0))
    return pl.pallas_call(
        _peer_expert_kernel,
        grid=(G, T // tm, u_tab.shape[1] // PEER_TE),
        in_specs=[pl.BlockSpec((None, tm, Dm), lambda g, t, e: (g, t, 0)), tab_spec, tab_spec,
                  a_spec, a_spec, b_spec, b_spec],
        out_specs=pl.BlockSpec((None, tm, Dm), lambda g, t, e: (g, t, 0)),
        out_shape=jax.ShapeDtypeStruct((G, T, Dm), f32),
        compiler_params=_params("parallel", "parallel", "arbitrary"), name="peer_experts",
    )(x, u_tab, v_tab, cnt, ea, rb, eb)


def peer(h, w_q, sub_keys, u_tab, v_tab, layer):
    G, T, Dm = h.shape
    Tp = -(-T // 128) * 128
    if Tp != T:
        h = jnp.pad(h, ((0, 0), (0, Tp - T), (0, 0)))
    q = token_matmul(h, w_q, bf16)
    out = peer_experts(h, peer_gates(q, sub_keys), u_tab, v_tab, layer)
    return out[:, :T] if Tp != T else out


def _trunk(y, sample, mod, norm_g, attend, conv, weights):
    (w_in, w_idx, q_gain, k_gain, w_out, rope128, rope64, w_pw1, b_pw1, w_pw2, peer_w_q, peer_sub_keys, peer_u,
     peer_v) = weights
    nm = functools.partial(normmod, mod=mod, sample=sample)
    h = nm(y, norm_g=norm_g[0, 0], shift=(0, 0), scale=(0, 1))
    q = q_projection(h, w_in, q_gain, *rope128)
    k, kb, v, vb = kv_projection(h, w_in, k_gain, *rope128)
    iq, ik, ikb, iw = idx_projection(h, w_idx, *rope64)
    o = attend(q, k, kb, v, vb, iq, ik, ikb, iw)
    d = token_matmul(o, w_out, f32)
    y, h = nm(y, delta=d, gate=(0, 2), norm_g=norm_g[0, 1], shift=(0, 3), scale=(0, 4))
    d = peer(h, peer_w_q[0], peer_sub_keys[0], peer_u, peer_v, 0)
    y, h = nm(y, delta=d, gate=(0, 5), norm_g=norm_g[1, 0], shift=(1, 0), scale=(1, 1))
    u = glu_projection(h, w_pw1, b_pw1)
    c, state = conv(u)
    d = token_matmul(c, w_pw2, f32)
    y, h = nm(y, delta=d, gate=(1, 2), norm_g=norm_g[1, 1], shift=(1, 3), scale=(1, 4))
    d = peer(h, peer_w_q[1], peer_sub_keys[1], peer_u, peer_v, 1)
    y = nm(y, delta=d, gate=(1, 5))
    return y, k, v, ik, state


def kernel(x_prompt, x_sample, cache_k, cache_v, cache_idx_k, state_conv, page_table, c_prompt, c_sample, w_ada, b_ada, norm_g, attn_w_in, attn_q_gain, attn_k_gain, attn_w_out, conv_w_pw1, conv_b_pw1, conv_w_dw, conv_b_dw, conv_ln_g, conv_ln_b, conv_w_pw2, peer_w_q, peer_sub_keys, peer_u, peer_v):
    B, T, Dm = x_prompt.shape
    Bd = x_sample.shape[0]
    n_pages = page_table.shape[1]
    past = n_pages * PAGE_SIZE
    n_phys = cache_k.shape[1]

    c_all = jnp.concatenate([c_sample, c_prompt, jnp.zeros((ADA_ROWS - Bd - B, Dm), f32)], axis=0)
    mod = ada_modulation(c_all, w_ada, b_ada)
    ng = norm_g.reshape(norm_g.shape[0], 2, 1, Dm)

    peer_u_bf, peer_v_bf = peer_u.astype(bf16), peer_v.astype(bf16)
    w_in = attn_w_in[0]
    idx0 = N_HEADS * HEAD_DIM + 2 * KV_COLS
    w_idx = jnp.concatenate([w_in[:, idx0:], jnp.zeros((Dm, IDX_W_COLS - (w_in.shape[1] - idx0)), f32)], axis=1)

    def weights(pos):
        return (w_in, w_idx, attn_q_gain[0], attn_k_gain[0], attn_w_out[0], _rope_tables(pos, HEAD_DIM),
                _rope_tables(pos, IDX_DIM), conv_w_pw1[0], conv_b_pw1[0], conv_w_pw2[0], peer_w_q, peer_sub_keys,
                peer_u_bf, peer_v_bf)

    conv_tail = (conv_w_dw[0], conv_b_dw[0], conv_ln_g[0], conv_ln_b[0])

    def attend_prompt(q, k, kb, v, vb, iq, ik, ikb, iw):
        return attention_prompt(q, kb, vb, iq, iw, ikb)

    def conv_p(u):
        return conv_prompt(u, *conv_tail), u[:, T - (CONV_WIDTH - 1):]

    def attend_sample(q, k, kb, v, vb, iq, ik, ikb, iw):
        iq_s = jnp.transpose(iq[0], (1, 0, 2)).astype(f32)
        scores, score_new = sample_scores(page_table, iq_s, iw.reshape(Bd, IDX_HEADS, 1),
                                          ik.reshape(Bd, 1, IDX_DIM), cache_idx_k[0])
        sel, sel_new = sample_select(scores, score_new)
        o = sample_attention(page_table, jnp.transpose(q[0], (1, 0, 2)), k.reshape(Bd, 1, KV_COLS),
                             v.reshape(Bd, 1, KV_COLS), sel, sel_new,
                             cache_k[0].reshape(n_phys, PAGE_ROWS, HEAD_DIM), cache_v[0].reshape(n_phys, PAGE_ROWS, HEAD_DIM))
        return o.reshape(1, Bd, N_HEADS * HEAD_DIM)

    def conv_s(u):
        hist = state_conv[0]
        c = conv_sample(u[0], hist, *conv_tail)
        return c[None], jnp.concatenate([hist[:, 1:], u[0][:, None, :]], axis=1)

    yp, kp, vp, ikp, stp = _trunk(x_prompt, False, mod, ng, attend_prompt, conv_p,
                                  weights(jnp.arange(T, dtype=i32)))
    ys, ks, vs, iks, sts = _trunk(x_sample.reshape(1, Bd, Dm), True, mod, ng, attend_sample, conv_s,
                                  weights(jnp.full((Bd,), past, i32)))
    return (yp, ys.reshape(Bd, 1, Dm),
            kp.reshape(1, B, T, KV_HEADS, HEAD_DIM), vp.reshape(1, B, T, KV_HEADS, HEAD_DIM),
            ikp.reshape(1, B, T, IDX_DIM),
            ks.reshape(1, Bd, 1, KV_HEADS, HEAD_DIM), vs.reshape(1, Bd, 1, KV_HEADS, HEAD_DIM),
            iks.reshape(1, Bd, 1, IDX_DIM),
            stp[None], sts[None])
```

```python
import functools
import math

import numpy as np
import jax
import jax.numpy as jnp
from jax import lax
from jax.experimental import pallas as pl
from jax.experimental.pallas import tpu as pltpu

f32 = jnp.float32
bf16 = jnp.bfloat16
i32 = jnp.int32

D_MODEL = 2048
N_HEADS = 16
HEAD_DIM = 128
KV_HEADS = 4
Q_PER_KV = N_HEADS // KV_HEADS
KV_COLS = KV_HEADS * HEAD_DIM
IDX_HEADS = 16
IDX_DIM = 64
TOPK_MAX = 256
ROPE_THETA = 10000.0
PAGE_SIZE = 128
CONV_WIDTH = 31
PEER_HEADS = 8
PEER_NKEYS = 128
PEER_TOPK = 16
RMS_EPS = 1e-6
LN_EPS = 1e-5
NEG = -1e30
MASKED = -1e30
INT_MIN = -(2 ** 31)
Q_SCALE = HEAD_DIM ** -0.5 * math.log2(math.e)
VMEM_LIMIT = 56 * 1024 * 1024
BF16_ROWS = 16
ADA_ROWS = 40
ADA_PROMPT_ROW = 32

NT_DIMS = (((1,), (1,)), ((), ()))


def _params(*sem):
    return pltpu.CompilerParams(dimension_semantics=sem, vmem_limit_bytes=VMEM_LIMIT)


def _ada_kernel(c_ref, w_ref, b_ref, o_ref):
    c = c_ref[...]
    a = (c * jax.nn.sigmoid(c)).astype(bf16)
    o_ref[...] = jnp.dot(a, w_ref[...].astype(bf16), preferred_element_type=f32) + b_ref[...]


def ada_modulation(c_all, w_ada, b_ada):
    depth, d, _ = w_ada.shape
    tn = 1024
    nj = d // tn
    return pl.pallas_call(
        _ada_kernel,
        grid=(depth, 6, nj),
        in_specs=[
            pl.BlockSpec((ADA_ROWS, d), lambda l, k, j: (0, 0)),
            pl.BlockSpec((None, d, tn), lambda l, k, j: (l, 0, k * nj + j)),
            pl.BlockSpec((None, 1, tn), lambda l, k, j: (l, 0, k * nj + j)),
        ],
        out_specs=pl.BlockSpec((None, None, ADA_ROWS, tn), lambda l, k, j: (l, k, 0, j)),
        out_shape=jax.ShapeDtypeStruct((depth, 6, ADA_ROWS, d), f32),
        compiler_params=_params("parallel", "parallel", "parallel"),
        name="ada",
    )(c_all, w_ada, b_ada.reshape(depth, 1, 6 * d))


def _mod_operand(mod, sample):
    return mod if sample else mod.reshape(mod.shape[:3] + (1, mod.shape[3]))


def _mod_spec(sample, layer, k):
    if sample:
        return pl.BlockSpec((None, None, 32, D_MODEL), lambda g, t, *_: (layer, k, 0, 0))
    return pl.BlockSpec((None, None, None, 1, D_MODEL), lambda g, t, *_: (layer, k, ADA_PROMPT_ROW + g, 0, 0))


def _normmod_kernel(*refs, has_delta, want_h):
    refs = list(refs)
    y = refs.pop(0)[...]
    if has_delta:
        d_ref, gate_ref = refs.pop(0), refs.pop(0)
        y = y + gate_ref[...] * d_ref[...]
    if want_h:
        g_ref, sh_ref, sc_ref = refs.pop(0), refs.pop(0), refs.pop(0)
    if has_delta:
        refs.pop(0)[...] = y
    if want_h:
        ms = jnp.mean(y * y, axis=-1, keepdims=True)
        hn = y * lax.rsqrt(ms + RMS_EPS) * g_ref[...]
        refs.pop(0)[...] = (hn * (1.0 + sc_ref[...]) + sh_ref[...]).astype(bf16)


def normmod(y, mod, sample, *, delta=None, gate=None, norm_g=None, shift=None, scale=None):
    G, T, Dm = y.shape
    tm = min(T, 256)
    tok = pl.BlockSpec((None, tm, Dm), lambda g, t: (g, t, 0))
    modop = _mod_operand(mod, sample)
    ins, specs = [y], [tok]
    if delta is not None:
        ins += [delta, modop]
        specs += [tok, _mod_spec(sample, *gate)]
    if norm_g is not None:
        ins += [norm_g, modop, modop]
        specs += [pl.BlockSpec((1, Dm), lambda g, t: (0, 0)), _mod_spec(sample, *shift), _mod_spec(sample, *scale)]
    outs, ospecs = [], []
    if delta is not None:
        outs.append(jax.ShapeDtypeStruct((G, T, Dm), f32)); ospecs.append(tok)
    if norm_g is not None:
        outs.append(jax.ShapeDtypeStruct((G, T, Dm), bf16)); ospecs.append(tok)
    res = pl.pallas_call(
        functools.partial(_normmod_kernel, has_delta=delta is not None, want_h=norm_g is not None),
        grid=(G, T // tm), in_specs=specs, out_specs=ospecs, out_shape=outs,
        compiler_params=_params("parallel", "parallel"), name="normmod",
    )(*ins)
    return res if len(res) > 1 else res[0]


def _matmul_kernel(x_ref, w_ref, o_ref):
    o_ref[...] = jnp.dot(x_ref[...], w_ref[...].astype(bf16), preferred_element_type=f32).astype(o_ref.dtype)


def token_matmul(x, w, out_dtype):
    G, T, K = x.shape
    N = w.shape[1]
    tm, tn = min(T, 1024), 512
    return pl.pallas_call(
        _matmul_kernel,
        grid=(G, T // tm, N // tn),
        in_specs=[pl.BlockSpec((None, tm, K), lambda g, t, j: (g, t, 0)),
                  pl.BlockSpec((K, tn), lambda g, t, j: (0, j))],
        out_specs=pl.BlockSpec((None, tm, tn), lambda g, t, j: (g, t, j)),
        out_shape=jax.ShapeDtypeStruct((G, T, N), out_dtype),
        compiler_params=_params("parallel", "parallel", "parallel"), name="token_matmul",
    )(x, w)


def _rope_tables(pos, dim):
    half = dim // 2
    inv = ROPE_THETA ** (-jnp.arange(half, dtype=f32) / half)
    ang = pos.astype(f32)[:, None] * inv
    cos, sin = jnp.cos(ang), jnp.sin(ang)
    reps = 128 // dim
    return jnp.tile(jnp.concatenate([cos, cos], -1), (1, reps)), jnp.tile(jnp.concatenate([-sin, sin], -1), (1, reps))


def _head_rmsnorm(x, gain, ones):
    ss = jnp.dot((x * x).astype(bf16), ones, preferred_element_type=f32)
    return x * lax.rsqrt(ss * (1.0 / HEAD_DIM) + RMS_EPS) * gain


def _qproj_kernel(x_ref, w_ref, gain_ref, cos_ref, sin_ref, o_ref):
    acc = jnp.dot(x_ref[...], w_ref[...].astype(bf16), preferred_element_type=f32)
    cos, sin, gain = cos_ref[...], sin_ref[...], gain_ref[...]
    ones = jnp.ones((HEAD_DIM, HEAD_DIM), bf16)
    for h in range(4):
        xn = _head_rmsnorm(acc[:, HEAD_DIM * h:HEAD_DIM * (h + 1)], gain, ones)
        r = xn * cos + pltpu.roll(xn, HEAD_DIM // 2, 1) * sin
        o_ref[h] = (r * Q_SCALE).astype(bf16)


def q_projection(h, w_in, q_gain, cos, sin):
    G, T, K = h.shape
    tm = min(T, 1024)
    return pl.pallas_call(
        _qproj_kernel,
        grid=(G, T // tm, 4),
        in_specs=[pl.BlockSpec((None, tm, K), lambda g, t, j: (g, t, 0)),
                  pl.BlockSpec((K, 512), lambda g, t, j: (0, j)),
                  pl.BlockSpec((1, HEAD_DIM), lambda g, t, j: (0, 0)),
                  pl.BlockSpec((tm, 128), lambda g, t, j: (t, 0)),
                  pl.BlockSpec((tm, 128), lambda g, t, j: (t, 0))],
        out_specs=pl.BlockSpec((None, 4, tm, HEAD_DIM), lambda g, t, j: (g, j, t, 0)),
        out_shape=jax.ShapeDtypeStruct((G, N_HEADS, T, HEAD_DIM), bf16),
        compiler_params=_params("parallel", "parallel", "parallel"), name="q_proj",
    )(h, w_in, q_gain.reshape(1, HEAD_DIM), cos, sin)


def _kvproj_kernel(x_ref, wk_ref, wv_ref, gain_ref, cos_ref, sin_ref, k_ref, kb_ref, v_ref, vb_ref):
    x = x_ref[...]
    kacc = jnp.dot(x, wk_ref[...].astype(bf16), preferred_element_type=f32)
    cos, sin, gain = cos_ref[...], sin_ref[...], gain_ref[...]
    ones = jnp.ones((HEAD_DIM, HEAD_DIM), bf16)
    for h in range(KV_HEADS):
        sl = slice(HEAD_DIM * h, HEAD_DIM * (h + 1))
        xn = _head_rmsnorm(kacc[:, sl], gain, ones)
        r = xn * cos + pltpu.roll(xn, HEAD_DIM // 2, 1) * sin
        k_ref[:, sl] = r
        kb_ref[:, sl] = r.astype(bf16)
    v = jnp.dot(x, wv_ref[...].astype(bf16), preferred_element_type=f32)
    v_ref[...] = v
    vb_ref[...] = v.astype(bf16)


def kv_projection(h, w_in, k_gain, cos, sin):
    G, T, K = h.shape
    tm = min(T, 1024)
    kcol = (N_HEADS * HEAD_DIM) // KV_COLS
    tok = pl.BlockSpec((None, tm, KV_COLS), lambda g, t: (g, t, 0))
    return pl.pallas_call(
        _kvproj_kernel,
        grid=(G, T // tm),
        in_specs=[pl.BlockSpec((None, tm, K), lambda g, t: (g, t, 0)),
                  pl.BlockSpec((K, KV_COLS), lambda g, t: (0, kcol)),
                  pl.BlockSpec((K, KV_COLS), lambda g, t: (0, kcol + 1)),
                  pl.BlockSpec((1, HEAD_DIM), lambda g, t: (0, 0)),
                  pl.BlockSpec((tm, 128), lambda g, t: (t, 0)),
                  pl.BlockSpec((tm, 128), lambda g, t: (t, 0))],
        out_specs=[tok, tok, tok, tok],
        out_shape=[jax.ShapeDtypeStruct((G, T, KV_COLS), f32), jax.ShapeDtypeStruct((G, T, KV_COLS), bf16),
                   jax.ShapeDtypeStruct((G, T, KV_COLS), f32), jax.ShapeDtypeStruct((G, T, KV_COLS), bf16)],
        compiler_params=_params("parallel", "parallel"), name="kv_proj",
    )(h, w_in, w_in, k_gain.reshape(1, HEAD_DIM), cos, sin)


IDX_W_COLS = 1152


def _idxproj_kernel(x_ref, w_ref, cos_ref, sin_ref, iq_ref, ik_ref, ikb_ref, iw_ref):
    acc = jnp.dot(x_ref[...], w_ref[...].astype(bf16), preferred_element_type=f32)
    cos, sin = cos_ref[...], sin_ref[...]
    lane = lax.broadcasted_iota(i32, cos.shape, 1)
    first = (lane % IDX_DIM) < (IDX_DIM // 2)

    def rope64(x):
        partner = jnp.where(first, pltpu.roll(x, 128 - IDX_DIM // 2, 1), pltpu.roll(x, IDX_DIM // 2, 1))
        return x * cos + partner * sin

    for p in range(IDX_HEADS // 2):
        r = rope64(acc[:, 128 * p:128 * (p + 1)])
        iq_ref[2 * p] = r[:, :IDX_DIM].astype(bf16)
        iq_ref[2 * p + 1] = r[:, IDX_DIM:].astype(bf16)
    slab = acc[:, IDX_HEADS * IDX_DIM:]
    r = rope64(slab)[:, :IDX_DIM]
    ik_ref[...] = r
    ikb_ref[...] = r.astype(bf16)
    iw_ref[...] = slab[:, IDX_DIM:IDX_DIM + IDX_HEADS] * (IDX_HEADS ** -0.5)


def idx_projection(h, w_idx, cos, sin):
    G, T, K = h.shape
    tm = min(T, 512)
    return pl.pallas_call(
        _idxproj_kernel,
        grid=(G, T // tm),
        in_specs=[pl.BlockSpec((None, tm, K), lambda g, t: (g, t, 0)),
                  pl.BlockSpec((K, IDX_W_COLS), lambda g, t: (0, 0)),
                  pl.BlockSpec((tm, 128), lambda g, t: (t, 0)),
                  pl.BlockSpec((tm, 128), lambda g, t: (t, 0))],
        out_specs=[pl.BlockSpec((None, IDX_HEADS, tm, IDX_DIM), lambda g, t: (g, 0, t, 0)),
                   pl.BlockSpec((None, tm, IDX_DIM), lambda g, t: (g, t, 0)),
                   pl.BlockSpec((None, tm, IDX_DIM), lambda g, t: (g, t, 0)),
                   pl.BlockSpec((None, tm, IDX_HEADS), lambda g, t: (g, t, 0))],
        out_shape=[jax.ShapeDtypeStruct((G, IDX_HEADS, T, IDX_DIM), bf16),
                   jax.ShapeDtypeStruct((G, T, IDX_DIM), f32),
                   jax.ShapeDtypeStruct((G, T, IDX_DIM), bf16),
                   jax.ShapeDtypeStruct((G, T, IDX_HEADS), f32)],
        compiler_params=_params("parallel", "parallel"), name="idx_proj",
    )(h, w_idx, cos, sin)


def _sortable(x):
    bits = pltpu.bitcast(jnp.where(x == 0.0, 0.0, x), i32)
    return bits ^ ((bits >> 31) & 0x7FFFFFFF)


def _kth_largest_key(count_ge, shape, k):
    t0 = jnp.where(count_ge(jnp.zeros(shape, i32)) >= k, 0, INT_MIN).astype(i32)

    def body(i, t):
        cand = t + jnp.left_shift(jnp.int32(1), 30 - i)
        return jnp.where(count_ge(cand) >= k, cand, t)

    return lax.fori_loop(0, 31, body, t0)


def _tie_cutoff(count_tied_below, shape, need, nbits):
    def body(i, j):
        cand = j + jnp.left_shift(jnp.int32(1), nbits - 1 - i)
        return jnp.where(count_tied_below(cand) < need, cand, j)

    return lax.fori_loop(0, nbits, body, jnp.zeros(shape, i32))


TQ = 256
TK = 512


def _attn_prompt_kernel(q_ref, k_ref, v_ref, iq_ref, iw_ref, ik_ref, o_ref, keys_ref, m_ref, acc_ref, *, topk):
    qi = pl.program_id(1)
    nkc = (qi * TQ + TQ - 1) // TK + 1
    iw = iw_ref[...]
    qpos = lax.broadcasted_iota(i32, (TQ, TK), 0) + qi * TQ
    lane = lax.broadcasted_iota(i32, (TQ, TK), 1)

    def score_chunk(c, carry):
        ikc = ik_ref[pl.ds(pl.multiple_of(c * TK, TK), TK), :]
        sc = jnp.zeros((TQ, TK), f32)
        for h in range(IDX_HEADS):
            s = lax.dot_general(iq_ref[h], ikc, NT_DIMS, preferred_element_type=f32)
            sc = sc + jnp.maximum(s, 0.0) * iw[:, h:h + 1]
        sc = jnp.where(lane + c * TK <= qpos, sc, NEG)
        keys_ref[c] = _sortable(sc)
        return carry

    lax.fori_loop(0, nkc, score_chunk, 0)

    def fold(x):
        return sum(x[:, 128 * j:128 * (j + 1)] for j in range(1, TK // 128)) + x[:, 0:128]

    def count(pred):
        def body(c, acc):
            return acc + fold(jnp.where(pred(c, keys_ref[c]), 1, 0).astype(i32))
        acc = lax.fori_loop(0, nkc, body, jnp.zeros((TQ, 128), i32))
        return jnp.sum(acc, axis=1, keepdims=True)

    thr = _kth_largest_key(lambda cand: count(lambda c, key: key >= cand), (TQ, 1), topk)

    n_ge = count(lambda c, key: key >= thr)

    @pl.when(jnp.max(jnp.where(n_ge != topk, 1, 0)) > 0)
    def _():
        n_gt = count(lambda c, key: key > thr)
        need = topk - n_gt
        cut = _tie_cutoff(lambda j: count(lambda c, key: (key == thr) & (lane + c * TK < j)), (TQ, 1), need, 12)

        def demote(c, carry):
            key = keys_ref[c]
            keys_ref[c] = jnp.where((key == thr) & (lane + c * TK > cut), thr - 1, key)
            return carry

        lax.fori_loop(0, nkc, demote, 0)

    m_ref[...] = jnp.full(m_ref.shape, MASKED, f32)
    acc_ref[...] = jnp.zeros(acc_ref.shape, f32)
    ones = jnp.ones((TK, HEAD_DIM), bf16)

    def attn_chunk(c, carry):
        off = pl.multiple_of(c * TK, TK)
        sel = (keys_ref[c] >= thr) & (lane + c * TK <= qpos)
        for g in range(KV_HEADS):
            kc = k_ref[pl.ds(off, TK), HEAD_DIM * g:HEAD_DIM * (g + 1)]
            vc = jnp.concatenate([v_ref[pl.ds(off, TK), HEAD_DIM * g:HEAD_DIM * (g + 1)], ones], axis=1)
            qg = q_ref[Q_PER_KV * g:Q_PER_KV * (g + 1)].reshape(Q_PER_KV * TQ, HEAD_DIM)
            s_all = lax.dot_general(qg, kc, NT_DIMS, preferred_element_type=f32)
            ps, alphas = [], []
            for r in range(Q_PER_KV):
                hd = Q_PER_KV * g + r
                s = jnp.where(sel, s_all[TQ * r:TQ * (r + 1)], MASKED)
                m_old = m_ref[hd]
                m_new = jnp.maximum(m_old, jnp.max(s, axis=-1, keepdims=True))
                m_ref[hd] = m_new
                alphas.append(jnp.exp2(m_old - m_new))
                ps.append(jnp.exp2(s - jnp.concatenate([m_new] * (TK // 128), axis=1)).astype(bf16))
            pv = jnp.dot(jnp.concatenate(ps, axis=0), vc, preferred_element_type=f32)
            for r in range(Q_PER_KV):
                hd = Q_PER_KV * g + r
                acc_ref[hd] = jnp.concatenate([alphas[r]] * 2, axis=1) * acc_ref[hd] + pv[TQ * r:TQ * (r + 1)]
        return carry

    lax.fori_loop(0, nkc, attn_chunk, 0)

    for hd in range(N_HEADS):
        a = acc_ref[hd]
        o_ref[:, HEAD_DIM * hd:HEAD_DIM * (hd + 1)] = (a[:, :HEAD_DIM] / a[:, HEAD_DIM:]).astype(bf16)


def attention_prompt(q, kb, vb, iq, iw, ikb):
    G, _, T, _ = q.shape
    topk = min(TOPK_MAX, T // 4)
    return pl.pallas_call(
        functools.partial(_attn_prompt_kernel, topk=topk),
        grid=(G, T // TQ),
        in_specs=[pl.BlockSpec((None, N_HEADS, TQ, HEAD_DIM), lambda g, t: (g, 0, t, 0)),
                  pl.BlockSpec((None, T, KV_COLS), lambda g, t: (g, 0, 0)),
                  pl.BlockSpec((None, T, KV_COLS), lambda g, t: (g, 0, 0)),
                  pl.BlockSpec((None, IDX_HEADS, TQ, IDX_DIM), lambda g, t: (g, 0, t, 0)),
                  pl.BlockSpec((None, TQ, IDX_HEADS), lambda g, t: (g, t, 0)),
                  pl.BlockSpec((None, T, IDX_DIM), lambda g, t: (g, 0, 0))],
        out_specs=pl.BlockSpec((None, TQ, N_HEADS * HEAD_DIM), lambda g, t: (g, t, 0)),
        out_shape=jax.ShapeDtypeStruct((G, T, N_HEADS * HEAD_DIM), bf16),
        scratch_shapes=[pltpu.VMEM((T // TK, TQ, TK), i32),
                        pltpu.VMEM((N_HEADS, TQ, 128), f32),
                        pltpu.VMEM((N_HEADS, TQ, 2 * HEAD_DIM), f32)],
        compiler_params=_params("parallel", "arbitrary"), name="attn_prompt",
    )(q, kb, vb, iq, iw, ikb)


SCORE_PAGES = 32
PAGES_PER_STEP = 32
PAGE_ROWS = PAGE_SIZE * KV_HEADS


def _sample_score_kernel(pt_ref, iq_ref, iw_ref, ikn_ref, *refs):
    page_refs = refs[:SCORE_PAGES]
    sc_ref, snew_ref = refs[SCORE_PAGES:]
    iq = iq_ref[...]
    iw = iw_ref[...]
    iqb = iq.astype(bf16)
    for p in range(SCORE_PAGES):
        s = lax.dot_general(iqb, page_refs[p][...].astype(bf16), NT_DIMS, preferred_element_type=f32)
        sc_ref[p:p + 1, :] = jnp.sum(jnp.maximum(s, 0.0) * iw, axis=0, keepdims=True)
    s_new = jnp.sum(iq * ikn_ref[...], axis=1, keepdims=True)
    s_new = jnp.sum(jnp.maximum(s_new, 0.0) * iw, axis=0, keepdims=True)
    snew_ref[...] = jnp.broadcast_to(s_new, snew_ref.shape)


def sample_scores(page_table, iq_s, iw_s, ik_new, cache_ik):
    B, n_pages = page_table.shape

    def page_spec(p):
        return pl.BlockSpec((None, PAGE_SIZE, IDX_DIM), lambda b, j, pt: (pt[b, j * SCORE_PAGES + p], 0, 0))

    grid_spec = pltpu.PrefetchScalarGridSpec(
        num_scalar_prefetch=1,
        grid=(B, n_pages // SCORE_PAGES),
        in_specs=[pl.BlockSpec((None, IDX_HEADS, IDX_DIM), lambda b, j, pt: (b, 0, 0)),
                  pl.BlockSpec((None, IDX_HEADS, 1), lambda b, j, pt: (b, 0, 0)),
                  pl.BlockSpec((None, 1, IDX_DIM), lambda b, j, pt: (b, 0, 0))]
                 + [page_spec(p) for p in range(SCORE_PAGES)],
        out_specs=[pl.BlockSpec((None, SCORE_PAGES, PAGE_SIZE), lambda b, j, pt: (b, j, 0)),
                   pl.BlockSpec((None, 1, PAGE_SIZE), lambda b, j, pt: (b, 0, 0))],
    )
    return pl.pallas_call(
        _sample_score_kernel, grid_spec=grid_spec,
        out_shape=[jax.ShapeDtypeStruct((B, n_pages, PAGE_SIZE), f32), jax.ShapeDtypeStruct((B, 1, PAGE_SIZE), f32)],
        compiler_params=_params("parallel", "arbitrary"), name="sample_scores",
    )(page_table, iq_s, iw_s, ik_new, *([cache_ik] * SCORE_PAGES))


def _sample_select_kernel(sc_ref, snew_ref, sel_ref, seln_ref, *, topk):
    B, past = sc_ref.shape
    keys = _sortable(sc_ref[...])
    key_new = _sortable(snew_ref[:, 0:1])
    idx = lax.broadcasted_iota(i32, keys.shape, 1)
    idx_new = jnp.full((B, 1), past, i32)

    def count(pred):
        new = jnp.where(pred(key_new, idx_new), 1, 0).astype(i32)
        return jnp.sum(jnp.where(pred(keys, idx), 1, 0).astype(i32), axis=1, keepdims=True) + new

    thr = _kth_largest_key(lambda cand: count(lambda k, i: k >= cand), (B, 1), topk)
    need = topk - count(lambda k, i: k > thr)
    cut = _tie_cutoff(lambda jj: count(lambda k, i: (k == thr) & (i < jj)), (B, 1), need, int(past).bit_length())
    chosen = lambda k, i: (k > thr) | ((k == thr) & (i <= cut))
    seln_ref[...] = jnp.broadcast_to(jnp.where(chosen(key_new, idx_new), 0.0, MASKED), seln_ref.shape)
    spread = (lax.broadcasted_iota(i32, (PAGE_SIZE, PAGE_ROWS), 1) // KV_HEADS
              == lax.broadcasted_iota(i32, (PAGE_SIZE, PAGE_ROWS), 0))
    spread = jnp.where(spread, 1.0, 0.0).astype(bf16)
    flags = jnp.where(chosen(keys, idx), 1.0, 0.0).astype(bf16)
    for pg in range(past // PAGE_SIZE):
        rep = jnp.dot(flags[:, PAGE_SIZE * pg:PAGE_SIZE * (pg + 1)], spread, preferred_element_type=f32)
        sel_ref[:, PAGE_ROWS * pg:PAGE_ROWS * (pg + 1)] = jnp.where(rep > 0.5, 0.0, MASKED)


def sample_select(scores, score_new):
    B, n_pages, _ = scores.shape
    past = n_pages * PAGE_SIZE
    topk = min(TOPK_MAX, (past + 1) // 4)
    sel, sel_new = pl.pallas_call(
        functools.partial(_sample_select_kernel, topk=topk),
        out_shape=[jax.ShapeDtypeStruct((B, n_pages * PAGE_ROWS), f32), jax.ShapeDtypeStruct((B, PAGE_SIZE), f32)],
        compiler_params=pltpu.CompilerParams(vmem_limit_bytes=VMEM_LIMIT), name="sample_select",
    )(scores.reshape(B, past), score_new.reshape(B, PAGE_SIZE))
    return sel.reshape(B, n_pages, PAGE_ROWS), sel_new.reshape(B, 1, PAGE_SIZE)


def _sample_attn_kernel(pt_ref, q_ref, kn_ref, vn_ref, sel_ref, seln_ref, *refs):
    k_refs = refs[:PAGES_PER_STEP]
    v_refs = refs[PAGES_PER_STEP:2 * PAGES_PER_STEP]
    o_ref, m_ref, l_ref, acc_ref = refs[2 * PAGES_PER_STEP:]
    j = pl.program_id(1)
    qb = q_ref[...]
    head_kv = lax.broadcasted_iota(i32, (N_HEADS, PAGE_ROWS), 0) // Q_PER_KV
    row_kv = lax.broadcasted_iota(i32, (N_HEADS, PAGE_ROWS), 1) % KV_HEADS
    own_rows = jnp.where(head_kv == row_kv, 0.0, MASKED)

    @pl.when(j == 0)
    def _():
        head_kv_col = lax.broadcasted_iota(i32, (N_HEADS, HEAD_DIM), 0) // Q_PER_KV
        kn = jnp.zeros((N_HEADS, HEAD_DIM), f32)
        vn = jnp.zeros((N_HEADS, HEAD_DIM), f32)
        for g in range(KV_HEADS):
            cols = slice(HEAD_DIM * g, HEAD_DIM * (g + 1))
            kn = jnp.where(head_kv_col == g, kn_ref[:, cols], kn)
            vn = jnp.where(head_kv_col == g, vn_ref[:, cols], vn)
        m_ref[...] = jnp.sum(qb.astype(f32) * kn, axis=1, keepdims=True) + seln_ref[:, 0:1]
        l_ref[...] = jnp.ones(l_ref.shape, f32)
        acc_ref[...] = vn

    s = jnp.concatenate(
        [lax.dot_general(qb, k_refs[p][...].astype(bf16), NT_DIMS, preferred_element_type=f32)
         + sel_ref[p:p + 1, :] + own_rows for p in range(PAGES_PER_STEP)], axis=1)
    m_old = m_ref[...]
    m_new = jnp.maximum(m_old, jnp.max(s, axis=1, keepdims=True))
    p_ = jnp.exp2(s - m_new)
    alpha = jnp.exp2(m_old - m_new)
    l_ref[...] = alpha * l_ref[...] + jnp.sum(p_, axis=1, keepdims=True)
    v = jnp.concatenate([v_refs[p][...].astype(bf16) for p in range(PAGES_PER_STEP)], axis=0)
    acc_ref[...] = alpha * acc_ref[...] + jnp.dot(p_.astype(bf16), v, preferred_element_type=f32)
    m_ref[...] = m_new

    @pl.when(j == pl.num_programs(1) - 1)
    def _():
        o_ref[...] = (acc_ref[...] / l_ref[...]).astype(bf16)


def sample_attention(page_table, q_s, k_new, v_new, sel, sel_new, cache_k, cache_v):
    B, n_pages = page_table.shape
    steps = n_pages // PAGES_PER_STEP

    def page_spec(p):
        return pl.BlockSpec((None, PAGE_ROWS, HEAD_DIM), lambda b, j, pt: (pt[b, j * PAGES_PER_STEP + p], 0, 0))

    grid_spec = pltpu.PrefetchScalarGridSpec(
        num_scalar_prefetch=1,
        grid=(B, steps),
        in_specs=[pl.BlockSpec((None, N_HEADS, HEAD_DIM), lambda b, j, pt: (b, 0, 0)),
                  pl.BlockSpec((None, 1, KV_COLS), lambda b, j, pt: (b, 0, 0)),
                  pl.BlockSpec((None, 1, KV_COLS), lambda b, j, pt: (b, 0, 0)),
                  pl.BlockSpec((None, PAGES_PER_STEP, PAGE_ROWS), lambda b, j, pt: (b, j, 0)),
                  pl.BlockSpec((None, 1, PAGE_SIZE), lambda b, j, pt: (b, 0, 0))]
                 + [page_spec(p) for p in range(PAGES_PER_STEP)] * 2,
        out_specs=pl.BlockSpec((None, N_HEADS, HEAD_DIM), lambda b, j, pt: (b, 0, 0)),
        scratch_shapes=[pltpu.VMEM((N_HEADS, 1), f32), pltpu.VMEM((N_HEADS, 1), f32),
                        pltpu.VMEM((N_HEADS, HEAD_DIM), f32)],
    )
    return pl.pallas_call(
        _sample_attn_kernel, grid_spec=grid_spec,
        out_shape=jax.ShapeDtypeStruct((B, N_HEADS, HEAD_DIM), bf16),
        compiler_params=_params("parallel", "arbitrary"), name="sample_attn",
    )(page_table, q_s, k_new, v_new, sel, sel_new, *([cache_k] * PAGES_PER_STEP), *([cache_v] * PAGES_PER_STEP))


def _glu_kernel(x_ref, wa_ref, wg_ref, ba_ref, bg_ref, o_ref):
    x = x_ref[...]
    a = jnp.dot(x, wa_ref[...].astype(bf16), preferred_element_type=f32) + ba_ref[...]
    g = jnp.dot(x, wg_ref[...].astype(bf16), preferred_element_type=f32) + bg_ref[...]
    o_ref[...] = a * jax.nn.sigmoid(g)


def glu_projection(h, w_pw1, b_pw1):
    G, T, K = h.shape
    C = w_pw1.shape[1] // 2
    tm, tn = min(T, 1024), 512
    nj = C // tn
    b2 = b_pw1.reshape(1, 2 * C)
    return pl.pallas_call(
        _glu_kernel,
        grid=(G, T // tm, nj),
        in_specs=[pl.BlockSpec((None, tm, K), lambda g, t, j: (g, t, 0)),
                  pl.BlockSpec((K, tn), lambda g, t, j: (0, j)),
                  pl.BlockSpec((K, tn), lambda g, t, j: (0, nj + j)),
                  pl.BlockSpec((1, tn), lambda g, t, j: (0, j)),
                  pl.BlockSpec((1, tn), lambda g, t, j: (0, nj + j))],
        out_specs=pl.BlockSpec((None, tm, tn), lambda g, t, j: (g, t, j)),
        out_shape=jax.ShapeDtypeStruct((G, T, C), f32),
        compiler_params=_params("parallel", "parallel", "parallel"), name="glu_proj",
    )(h, w_pw1, w_pw1, b2, b2)


def _ln_silu(y, g, b):
    mu = jnp.mean(y, axis=-1, keepdims=True)
    var = jnp.mean(jnp.square(y - mu), axis=-1, keepdims=True)
    z = (y - mu) * lax.rsqrt(var + LN_EPS) * g + b
    return z * jax.nn.sigmoid(z)


CONV_HALO = 32


CONV_COLS = 256


def _conv_prompt_kernel(u_ref, halo_ref, w_ref, b_ref, g_ref, be_ref, o_ref, ext_ref, sh_ref, y_ref):
    t = pl.program_id(1)
    tt, C = u_ref.shape
    halo = halo_ref[...]
    ext_ref[0:CONV_HALO, :] = jnp.where(t == 0, jnp.zeros_like(halo), halo)
    ext_ref[CONV_HALO:, :] = u_ref[...]
    base = CONV_HALO - (CONV_WIDTH - 1)
    y_ref[...] = jnp.zeros(y_ref.shape, f32) + b_ref[...]
    for shift in range(8):
        taps = [(a, 8 * a + shift - base) for a in range(CONV_HALO // 8 + 1)
                if 0 <= 8 * a + shift - base < CONV_WIDTH]
        rows = 8 * taps[-1][0] + tt
        sh_ref[0:rows, :] = ext_ref[shift:shift + rows, :]
        for c in range(C // CONV_COLS):
            cols = slice(CONV_COLS * c, CONV_COLS * (c + 1))
            acc = y_ref[:, cols]
            for a, k in taps:
                acc = acc + sh_ref[8 * a:8 * a + tt, cols] * w_ref[k:k + 1, cols]
            y_ref[:, cols] = acc
    o_ref[...] = _ln_silu(y_ref[...], g_ref[...], be_ref[...]).astype(bf16)


def conv_prompt(u, w_dw, b_dw, ln_g, ln_b):
    G, T, C = u.shape
    tt = 128
    hb = tt // CONV_HALO
    row = pl.BlockSpec((1, C), lambda g, t: (0, 0))
    return pl.pallas_call(
        _conv_prompt_kernel,
        grid=(G, T // tt),
        in_specs=[pl.BlockSpec((None, tt, C), lambda g, t: (g, t, 0)),
                  pl.BlockSpec((None, CONV_HALO, C), lambda g, t: (g, jnp.maximum(t * hb - 1, 0), 0)),
                  pl.BlockSpec((CONV_WIDTH, C), lambda g, t: (0, 0)), row, row, row],
        out_specs=pl.BlockSpec((None, tt, C), lambda g, t: (g, t, 0)),
        out_shape=jax.ShapeDtypeStruct((G, T, C), bf16),
        scratch_shapes=[pltpu.VMEM((CONV_HALO + tt, C), f32), pltpu.VMEM((CONV_HALO + tt, C), f32),
                        pltpu.VMEM((tt, C), f32)],
        compiler_params=_params("parallel", "parallel"), name="conv_prompt",
    )(u, u, w_dw, b_dw.reshape(1, C), ln_g.reshape(1, C), ln_b.reshape(1, C))


def _conv_sample_kernel(u_ref, hist_ref, w_ref, b_ref, g_ref, be_ref, o_ref):
    w = w_ref[...]
    y = jnp.sum(hist_ref[...] * w[None, :CONV_WIDTH - 1, :], axis=1) + u_ref[...] * w[CONV_WIDTH - 1:, :] + b_ref[...]
    o_ref[...] = _ln_silu(y, g_ref[...], be_ref[...]).astype(bf16)


def conv_sample(u, hist, w_dw, b_dw, ln_g, ln_b):
    B, C = u.shape
    tc = 512
    col = lambda r: pl.BlockSpec((r, tc), lambda j: (0, j))
    return pl.pallas_call(
        _conv_sample_kernel,
        grid=(C // tc,),
        in_specs=[col(B), pl.BlockSpec((B, CONV_WIDTH - 1, tc), lambda j: (0, 0, j)), col(CONV_WIDTH),
                  col(1), col(1), col(1)],
        out_specs=col(B),
        out_shape=jax.ShapeDtypeStruct((B, C), bf16),
        compiler_params=_params("arbitrary"), name="conv_sample",
    )(u, hist, w_dw, b_dw.reshape(1, C), ln_g.reshape(1, C), ln_b.reshape(1, C))


def _top_values(x, n):
    vals = []
    for _ in range(n):
        m = jnp.max(x, axis=0, keepdims=True)
        vals.append(m)
        x = jnp.where(x == m, -jnp.inf, x)
    return vals


NOT_TOP = float(PEER_TOPK)


def _top_ranked(x, n):
    vals = []
    rank = jnp.full(x.shape, NOT_TOP, f32)
    for r in range(n):
        m = jnp.max(x, axis=0, keepdims=True)
        hit = x == m
        vals.append(m)
        rank = jnp.where(hit, float(r), rank)
        x = jnp.where(hit, -jnp.inf, x)
    return vals, rank


def _peer_gate_kernel(q_ref, sk_ref, cnt_ref, ea_ref, rb_ref, eb_ref):
    for h in range(PEER_HEADS):
        st = []
        for p in range(2):
            col = (2 * h + p) * PEER_NKEYS
            st.append(lax.dot_general(sk_ref[h, p].astype(bf16), q_ref[:, col:col + PEER_NKEYS], NT_DIMS,
                                      preferred_element_type=f32))
        a, b = st
        va, rank_a = _top_ranked(a, PEER_TOPK)
        vbl, rank_b = _top_ranked(b, PEER_TOPK)
        row = lax.broadcasted_iota(i32, (PEER_TOPK, a.shape[1]), 0)
        vb = jnp.zeros((PEER_TOPK, a.shape[1]), f32)
        for r in range(PEER_TOPK):
            vb = jnp.where(row == r, vbl[r], vb)
        cands = [va[r] + vb for r in range(PEER_TOPK)]
        row8 = lax.broadcasted_iota(i32, (8, a.shape[1]), 0)
        va_tail = jnp.zeros((8, a.shape[1]), f32)
        for r in range(8):
            va_tail = jnp.where(row8 == r, va[8 + r], va_tail)
        search = [cands[0]] + [cands[r][:8] for r in range(1, 8)] + [va_tail + vbl[0]]
        thr = _top_values(jnp.concatenate(search, axis=0), PEER_TOPK)[-1]
        top = va[0] + vbl[0]
        z = jnp.zeros_like(top)
        cnt = jnp.zeros(a.shape, f32)
        for r in range(PEER_TOPK):
            keep = cands[r] >= thr
            z = z + jnp.sum(jnp.where(keep, jnp.exp(cands[r] - top), 0.0), axis=0, keepdims=True)
            kept = jnp.sum(jnp.where(keep, 1.0, 0.0), axis=0, keepdims=True)
            cnt = jnp.where(rank_a == float(r), kept, cnt)
        ea = jnp.exp(a - va[0]) / z
        for grp in range(PEER_NKEYS // 8):
            cnt_ref[grp, h] = cnt[8 * grp:8 * (grp + 1), :]
            ea_ref[grp, h] = ea[8 * grp:8 * (grp + 1), :]
        eb = jnp.exp(b - vbl[0])
        for s in range(PEER_NKEYS // BF16_ROWS):
            rows = slice(BF16_ROWS * s, BF16_ROWS * (s + 1))
            rb_ref[h, s] = rank_b[rows, :].astype(bf16)
            eb_ref[h, s] = eb[rows, :].astype(bf16)


def peer_gates(q, sub_keys):
    G, T, _ = q.shape
    tm = min(T, 256)
    jt = PEER_NKEYS // BF16_ROWS
    a_spec = pl.BlockSpec((None, PEER_NKEYS // 8, PEER_HEADS, 8, tm), lambda g, t: (g, 0, 0, 0, t))
    b_spec = pl.BlockSpec((None, PEER_HEADS, jt, BF16_ROWS, tm), lambda g, t: (g, 0, 0, 0, t))
    a_shape = jax.ShapeDtypeStruct((G, PEER_NKEYS // 8, PEER_HEADS, 8, T), f32)
    b_shape = jax.ShapeDtypeStruct((G, PEER_HEADS, jt, BF16_ROWS, T), bf16)
    return pl.pallas_call(
        _peer_gate_kernel,
        grid=(G, T // tm),
        in_specs=[pl.BlockSpec((None, tm, q.shape[2]), lambda g, t: (g, t, 0)),
                  pl.BlockSpec(sub_keys.shape, lambda g, t: (0, 0, 0, 0))],
        out_specs=[a_spec, a_spec, b_spec, b_spec],
        out_shape=[a_shape, a_shape, b_shape, b_shape],
        compiler_params=_params("parallel", "parallel"), name="peer_gates",
    )(q, sub_keys)


PEER_ROWS = 8
PEER_TE = PEER_ROWS * PEER_NKEYS
PEER_OUT_COLS = 512
PEER_PASS = 4


def _gelu(x):
    return 0.5 * x * (1.0 + lax.erf(x * (2.0 ** -0.5)))


def _peer_expert_kernel(x_ref, u_ref, v_ref, cnt_ref, ea_ref, rb_ref, eb_ref, o_ref):
    @pl.when(pl.program_id(2) == 0)
    def _():
        o_ref[...] = jnp.zeros(o_ref.shape, f32)

    tm = x_ref.shape[0]
    jt = PEER_NKEYS // BF16_ROWS
    hid = lax.dot_general(x_ref[...], u_ref[...], NT_DIMS, preferred_element_type=f32)
    act = _gelu(hid)
    gates = [[None] * (tm // 128) for _ in range(PEER_ROWS)]
    for c in range(tm // 128):
        tok = slice(128 * c, 128 * (c + 1))
        for r0 in range(0, PEER_ROWS, PEER_PASS):
            gt = [jnp.zeros((jt, BF16_ROWS, 128), bf16) for _ in range(PEER_PASS)]
            for h in range(PEER_HEADS):
                rb = rb_ref[h, :, :, tok]
                eb = eb_ref[h, :, :, tok]
                for k in range(PEER_PASS):
                    r = r0 + k
                    cnt = jnp.broadcast_to(cnt_ref[h, r:r + 1, tok], (BF16_ROWS, 128)).astype(bf16)
                    eai = jnp.broadcast_to(ea_ref[h, r:r + 1, tok], (BF16_ROWS, 128)).astype(bf16)
                    active = jnp.minimum(jnp.maximum(cnt[None] - rb, 0.0), 1.0)
                    gt[k] = gt[k] + active * (eai[None] * eb)
            for k in range(PEER_PASS):
                gates[r0 + k][c] = gt[k].astype(f32).reshape(PEER_NKEYS, 128).T
    blocks = [jnp.concatenate(gates[r], axis=0) * act[:, PEER_NKEYS * r:PEER_NKEYS * (r + 1)]
              for r in range(PEER_ROWS)]
    w = jnp.concatenate(blocks, axis=1).astype(bf16)
    for j in range(v_ref.shape[1] // PEER_OUT_COLS):
        cols = slice(PEER_OUT_COLS * j, PEER_OUT_COLS * (j + 1))
        o_ref[:, cols] += jnp.dot(w, v_ref[:, cols], preferred_element_type=f32)


def peer_experts(x, gates, u_tab, v_tab, layer):
    G, T, Dm = x.shape
    cnt, ea, rb, eb = gates
    tm = min(T, 512)
    a_spec = pl.BlockSpec((None, None, PEER_HEADS, 8, tm), lambda g, t, e: (g, e, 0, 0, t))
    b_spec = pl.BlockSpec((None, PEER_HEADS, PEER_NKEYS // BF16_ROWS, BF16_ROWS, tm), lambda g, t, e: (g, 0, 0, 0, t))
    tab_spec =pl.BlockSpec((None, PEER_TE, Dm), lambda g, t, e: (layer, e, 0))
    return pl.pallas_call(
        _peer_expert_kernel,
        grid=(G, T // tm, u_tab.shape[1] // PEER_TE),
        in_specs=[pl.BlockSpec((None, tm, Dm), lambda g, t, e: (g, t, 0)), tab_spec, tab_spec,
                  a_spec, a_spec, b_spec, b_spec],
        out_specs=pl.BlockSpec((None, tm, Dm), lambda g, t, e: (g, t, 0)),
        out_shape=jax.ShapeDtypeStruct((G, T, Dm), f32),
        compiler_params=_params("parallel", "parallel", "arbitrary"), name="peer_experts",
    )(x, u_tab, v_tab, cnt, ea, rb, eb)


def peer(h, w_q, sub_keys, u_tab, v_tab, layer):
    G, T, Dm = h.shape
    Tp = -(-T // 128) * 128
    if Tp != T:
        h = jnp.pad(h, ((0, 0), (0, Tp - T), (0, 0)))
    q = token_matmul(h, w_q, bf16)
    out = peer_experts(h, peer_gates(q, sub_keys), u_tab, v_tab, layer)
    return out[:, :T] if Tp != T else out


def _trunk(y, sample, mod, norm_g, attend, conv, weights):
    (w_in, w_idx, q_gain, k_gain, w_out, rope128, rope64, w_pw1, b_pw1, w_pw2, peer_w_q, peer_sub_keys, peer_u,
     peer_v) = weights
    nm = functools.partial(normmod, mod=mod, sample=sample)
    h = nm(y, norm_g=norm_g[0, 0], shift=(0, 0), scale=(0, 1))
    q = q_projection(h, w_in, q_gain, *rope128)
    k, kb, v, vb = kv_projection(h, w_in, k_gain, *rope128)
    iq, ik, ikb, iw = idx_projection(h, w_idx, *rope64)
    o = attend(q, k, kb, v, vb, iq, ik, ikb, iw)
    d = token_matmul(o, w_out, f32)
    y, h = nm(y, delta=d, gate=(0, 2), norm_g=norm_g[0, 1], shift=(0, 3), scale=(0, 4))
    d = peer(h, peer_w_q[0], peer_sub_keys[0], peer_u, peer_v, 0)
    y, h = nm(y, delta=d, gate=(0, 5), norm_g=norm_g[1, 0], shift=(1, 0), scale=(1, 1))
    u = glu_projection(h, w_pw1, b_pw1)
    c, state = conv(u)
    d = token_matmul(c, w_pw2, f32)
    y, h = nm(y, delta=d, gate=(1, 2), norm_g=norm_g[1, 1], shift=(1, 3), scale=(1, 4))
    d = peer(h, peer_w_q[1], peer_sub_keys[1], peer_u, peer_v, 1)
    y = nm(y, delta=d, gate=(1, 5))
    return y, k, v, ik, state


def kernel(x_prompt, x_sample, cache_k, cache_v, cache_idx_k, state_conv, page_table, c_prompt, c_sample, w_ada, b_ada, norm_g, attn_w_in, attn_q_gain, attn_k_gain, attn_w_out, conv_w_pw1, conv_b_pw1, conv_w_dw, conv_b_dw, conv_ln_g, conv_ln_b, conv_w_pw2, peer_w_q, peer_sub_keys, peer_u, peer_v):
    B, T, Dm = x_prompt.shape
    Bd = x_sample.shape[0]
    n_pages = page_table.shape[1]
    past = n_pages * PAGE_SIZE
    n_phys = cache_k.shape[1]

    c_all = jnp.concatenate([c_sample, c_prompt, jnp.zeros((ADA_ROWS - Bd - B, Dm), f32)], axis=0)
    mod = ada_modulation(c_all, w_ada, b_ada)
    ng = norm_g.reshape(norm_g.shape[0], 2, 1, Dm)

    peer_u_bf, peer_v_bf = peer_u.astype(bf16), peer_v.astype(bf16)
    w_in = attn_w_in[0]
    idx0 = N_HEADS * HEAD_DIM + 2 * KV_COLS
    w_idx = jnp.concatenate([w_in[:, idx0:], jnp.zeros((Dm, IDX_W_COLS - (w_in.shape[1] - idx0)), f32)], axis=1)

    def weights(pos):
        return (w_in, w_idx, attn_q_gain[0], attn_k_gain[0], attn_w_out[0], _rope_tables(pos, HEAD_DIM),
                _rope_tables(pos, IDX_DIM), conv_w_pw1[0], conv_b_pw1[0], conv_w_pw2[0], peer_w_q, peer_sub_keys,
                peer_u_bf, peer_v_bf)

    conv_tail = (conv_w_dw[0], conv_b_dw[0], conv_ln_g[0], conv_ln_b[0])

    def attend_prompt(q, k, kb, v, vb, iq, ik, ikb, iw):
        return attention_prompt(q, kb, vb, iq, iw, ikb)

    def conv_p(u):
        return conv_prompt(u, *conv_tail), u[:, T - (CONV_WIDTH - 1):]

    def attend_sample(q, k, kb, v, vb, iq, ik, ikb, iw):
        iq_s = jnp.transpose(iq[0], (1, 0, 2)).astype(f32)
        scores, score_new = sample_scores(page_table, iq_s, iw.reshape(Bd, IDX_HEADS, 1),
                                          ik.reshape(Bd, 1, IDX_DIM), cache_idx_k[0])
        sel, sel_new = sample_select(scores, score_new)
        o = sample_attention(page_table, jnp.transpose(q[0], (1, 0, 2)), k.reshape(Bd, 1, KV_COLS),
                             v.reshape(Bd, 1, KV_COLS), sel, sel_new,
                             cache_k[0].reshape(n_phys, PAGE_ROWS, HEAD_DIM), cache_v[0].reshape(n_phys, PAGE_ROWS, HEAD_DIM))
        return o.reshape(1, Bd, N_HEADS * HEAD_DIM)

    def conv_s(u):
        hist = state_conv[0]
        c = conv_sample(u[0], hist, *conv_tail)
        return c[None], jnp.concatenate([hist[:, 1:], u[0][:, None, :]], axis=1)

    yp, kp, vp, ikp, stp = _trunk(x_prompt, False, mod, ng, attend_prompt, conv_p,
                                  weights(jnp.arange(T, dtype=i32)))
    ys, ks, vs, iks, sts = _trunk(x_sample.reshape(1, Bd, Dm), True, mod, ng, attend_sample, conv_s,
                                  weights(jnp.full((Bd,), past, i32)))
    return (yp, ys.reshape(Bd, 1, Dm),
            kp.reshape(1, B, T, KV_HEADS, HEAD_DIM), vp.reshape(1, B, T, KV_HEADS, HEAD_DIM),
            ikp.reshape(1, B, T, IDX_DIM),
            ks.reshape(1, Bd, 1, KV_HEADS, HEAD_DIM), vs.reshape(1, Bd, 1, KV_HEADS, HEAD_DIM),
            iks.reshape(1, Bd, 1, IDX_DIM),
            stp[None], sts[None])
```

```python
import functools
import math

import numpy as np
import jax
import jax.numpy as jnp
from jax import lax
from jax.experimental import pallas as pl
from jax.experimental.pallas import tpu as pltpu

f32 = jnp.float32
bf16 = jnp.bfloat16
i32 = jnp.int32

D_MODEL = 2048
N_HEADS = 16
HEAD_DIM = 128
KV_HEADS = 4
Q_PER_KV = N_HEADS // KV_HEADS
KV_COLS = KV_HEADS * HEAD_DIM
IDX_HEADS = 16
IDX_DIM = 64
TOPK_MAX = 256
ROPE_THETA = 10000.0
PAGE_SIZE = 128
CONV_WIDTH = 31
PEER_HEADS = 8
PEER_NKEYS = 128
PEER_TOPK = 16
RMS_EPS = 1e-6
LN_EPS = 1e-5
NEG = -1e30
MASKED = -1e30
INT_MIN = -(2 ** 31)
Q_SCALE = HEAD_DIM ** -0.5 * math.log2(math.e)
VMEM_LIMIT = 56 * 1024 * 1024
BF16_ROWS = 16
ADA_ROWS = 40
ADA_PROMPT_ROW = 32

NT_DIMS = (((1,), (1,)), ((), ()))


def _params(*sem):
    return pltpu.CompilerParams(dimension_semantics=sem, vmem_limit_bytes=VMEM_LIMIT)


def _ada_kernel(c_ref, w_ref, b_ref, o_ref):
    c = c_ref[...]
    a = (c * jax.nn.sigmoid(c)).astype(bf16)
    o_ref[...] = jnp.dot(a, w_ref[...].astype(bf16), preferred_element_type=f32) + b_ref[...]


def ada_modulation(c_all, w_ada, b_ada):
    depth, d, _ = w_ada.shape
    tn = 1024
    nj = d // tn
    return pl.pallas_call(
        _ada_kernel,
        grid=(depth, 6, nj),
        in_specs=[
            pl.BlockSpec((ADA_ROWS, d), lambda l, k, j: (0, 0)),
            pl.BlockSpec((None, d, tn), lambda l, k, j: (l, 0, k * nj + j)),
            pl.BlockSpec((None, 1, tn), lambda l, k, j: (l, 0, k * nj + j)),
        ],
        out_specs=pl.BlockSpec((None, None, ADA_ROWS, tn), lambda l, k, j: (l, k, 0, j)),
        out_shape=jax.ShapeDtypeStruct((depth, 6, ADA_ROWS, d), f32),
        compiler_params=_params("parallel", "parallel", "parallel"),
        name="ada",
    )(c_all, w_ada, b_ada.reshape(depth, 1, 6 * d))


def _mod_operand(mod, sample):
    return mod if sample else mod.reshape(mod.shape[:3] + (1, mod.shape[3]))


def _mod_spec(sample, layer, k):
    if sample:
        return pl.BlockSpec((None, None, 32, D_MODEL), lambda g, t, *_: (layer, k, 0, 0))
    return pl.BlockSpec((None, None, None, 1, D_MODEL), lambda g, t, *_: (layer, k, ADA_PROMPT_ROW + g, 0, 0))


def _normmod_kernel(*refs, has_delta, want_h):
    refs = list(refs)
    y = refs.pop(0)[...]
    if has_delta:
        d_ref, gate_ref = refs.pop(0), refs.pop(0)
        y = y + gate_ref[...] * d_ref[...]
    if want_h:
        g_ref, sh_ref, sc_ref = refs.pop(0), refs.pop(0), refs.pop(0)
    if has_delta:
        refs.pop(0)[...] = y
    if want_h:
        ms = jnp.mean(y * y, axis=-1, keepdims=True)
        hn = y * lax.rsqrt(ms + RMS_EPS) * g_ref[...]
        refs.pop(0)[...] = (hn * (1.0 + sc_ref[...]) + sh_ref[...]).astype(bf16)


def normmod(y, mod, sample, *, delta=None, gate=None, norm_g=None, shift=None, scale=None):
    G, T, Dm = y.shape
    tm = min(T, 256)
    tok = pl.BlockSpec((None, tm, Dm), lambda g, t: (g, t, 0))
    modop = _mod_operand(mod, sample)
    ins, specs = [y], [tok]
    if delta is not None:
        ins += [delta, modop]
        specs += [tok, _mod_spec(sample, *gate)]
    if norm_g is not None:
        ins += [norm_g, modop, modop]
        specs += [pl.BlockSpec((1, Dm), lambda g, t: (0, 0)), _mod_spec(sample, *shift), _mod_spec(sample, *scale)]
    outs, ospecs = [], []
    if delta is not None:
        outs.append(jax.ShapeDtypeStruct((G, T, Dm), f32)); ospecs.append(tok)
    if norm_g is not None:
        outs.append(jax.ShapeDtypeStruct((G, T, Dm), bf16)); ospecs.append(tok)
    res = pl.pallas_call(
        functools.partial(_normmod_kernel, has_delta=delta is not None, want_h=norm_g is not None),
        grid=(G, T // tm), in_specs=specs, out_specs=ospecs, out_shape=outs,
        compiler_params=_params("parallel", "parallel"), name="normmod",
    )(*ins)
    return res if len(res) > 1 else res[0]


def _matmul_kernel(x_ref, w_ref, o_ref):
    o_ref[...] = jnp.dot(x_ref[...], w_ref[...].astype(bf16), preferred_element_type=f32).astype(o_ref.dtype)


def token_matmul(x, w, out_dtype):
    G, T, K = x.shape
    N = w.shape[1]
    tm, tn = min(T, 1024), 512
    return pl.pallas_call(
        _matmul_kernel,
        grid=(G, T // tm, N // tn),
        in_specs=[pl.BlockSpec((None, tm, K), lambda g, t, j: (g, t, 0)),
                  pl.BlockSpec((K, tn), lambda g, t, j: (0, j))],
        out_specs=pl.BlockSpec((None, tm, tn), lambda g, t, j: (g, t, j)),
        out_shape=jax.ShapeDtypeStruct((G, T, N), out_dtype),
        compiler_params=_params("parallel", "parallel", "parallel"), name="token_matmul",
    )(x, w)


def _rope_tables(pos, dim):
    half = dim // 2
    inv = ROPE_THETA ** (-jnp.arange(half, dtype=f32) / half)
    ang = pos.astype(f32)[:, None] * inv
    cos, sin = jnp.cos(ang), jnp.sin(ang)
    reps = 128 // dim
    return jnp.tile(jnp.concatenate([cos, cos], -1), (1, reps)), jnp.tile(jnp.concatenate([-sin, sin], -1), (1, reps))


def _head_rmsnorm(x, gain, ones):
    ss = jnp.dot((x * x).astype(bf16), ones, preferred_element_type=f32)
    return x * lax.rsqrt(ss * (1.0 / HEAD_DIM) + RMS_EPS) * gain


def _qproj_kernel(x_ref, w_ref, gain_ref, cos_ref, sin_ref, o_ref):
    acc = jnp.dot(x_ref[...], w_ref[...].astype(bf16), preferred_element_type=f32)
    cos, sin, gain = cos_ref[...], sin_ref[...], gain_ref[...]
    ones = jnp.ones((HEAD_DIM, HEAD_DIM), bf16)
    for h in range(4):
        xn = _head_rmsnorm(acc[:, HEAD_DIM * h:HEAD_DIM * (h + 1)], gain, ones)
        r = xn * cos + pltpu.roll(xn, HEAD_DIM // 2, 1) * sin
        o_ref[h] = (r * Q_SCALE).astype(bf16)


def q_projection(h, w_in, q_gain, cos, sin):
    G, T, K = h.shape
    tm = min(T, 1024)
    return pl.pallas_call(
        _qproj_kernel,
        grid=(G, T // tm, 4),
        in_specs=[pl.BlockSpec((None, tm, K), lambda g, t, j: (g, t, 0)),
                  pl.BlockSpec((K, 512), lambda g, t, j: (0, j)),
                  pl.BlockSpec((1, HEAD_DIM), lambda g, t, j: (0, 0)),
                  pl.BlockSpec((tm, 128), lambda g, t, j: (t, 0)),
                  pl.BlockSpec((tm, 128), lambda g, t, j: (t, 0))],
        out_specs=pl.BlockSpec((None, 4, tm, HEAD_DIM), lambda g, t, j: (g, j, t, 0)),
        out_shape=jax.ShapeDtypeStruct((G, N_HEADS, T, HEAD_DIM), bf16),
        compiler_params=_params("parallel", "parallel", "parallel"), name="q_proj",
    )(h, w_in, q_gain.reshape(1, HEAD_DIM), cos, sin)


def _kvproj_kernel(x_ref, wk_ref, wv_ref, gain_ref, cos_ref, sin_ref, k_ref, kb_ref, v_ref, vb_ref):
    x = x_ref[...]
    kacc = jnp.dot(x, wk_ref[...].astype(bf16), preferred_element_type=f32)
    cos, sin, gain = cos_ref[...], sin_ref[...], gain_ref[...]
    ones = jnp.ones((HEAD_DIM, HEAD_DIM), bf16)
    for h in range(KV_HEADS):
        sl = slice(HEAD_DIM * h, HEAD_DIM * (h + 1))
        xn = _head_rmsnorm(kacc[:, sl], gain, ones)
        r = xn * cos + pltpu.roll(xn, HEAD_DIM // 2, 1) * sin
        k_ref[:, sl] = r
        kb_ref[:, sl] = r.astype(bf16)
    v = jnp.dot(x, wv_ref[...].astype(bf16), preferred_element_type=f32)
    v_ref[...] = v
    vb_ref[...] = v.astype(bf16)


def kv_projection(h, w_in, k_gain, cos, sin):
    G, T, K = h.shape
    tm = min(T, 1024)
    kcol = (N_HEADS * HEAD_DIM) // KV_COLS
    tok = pl.BlockSpec((None, tm, KV_COLS), lambda g, t: (g, t, 0))
    return pl.pallas_call(
        _kvproj_kernel,
        grid=(G, T // tm),
        in_specs=[pl.BlockSpec((None, tm, K), lambda g, t: (g, t, 0)),
                  pl.BlockSpec((K, KV_COLS), lambda g, t: (0, kcol)),
                  pl.BlockSpec((K, KV_COLS), lambda g, t: (0, kcol + 1)),
                  pl.BlockSpec((1, HEAD_DIM), lambda g, t: (0, 0)),
                  pl.BlockSpec((tm, 128), lambda g, t: (t, 0)),
                  pl.BlockSpec((tm, 128), lambda g, t: (t, 0))],
        out_specs=[tok, tok, tok, tok],
        out_shape=[jax.ShapeDtypeStruct((G, T, KV_COLS), f32), jax.ShapeDtypeStruct((G, T, KV_COLS), bf16),
                   jax.ShapeDtypeStruct((G, T, KV_COLS), f32), jax.ShapeDtypeStruct((G, T, KV_COLS), bf16)],
        compiler_params=_params("parallel", "parallel"), name="kv_proj",
    )(h, w_in, w_in, k_gain.reshape(1, HEAD_DIM), cos, sin)


IDX_W_COLS = 1152


def _idxproj_kernel(x_ref, w_ref, cos_ref, sin_ref, iq_ref, ik_ref, ikb_ref, iw_ref):
    acc = jnp.dot(x_ref[...], w_ref[...].astype(bf16), preferred_element_type=f32)
    cos, sin = cos_ref[...], sin_ref[...]
    lane = lax.broadcasted_iota(i32, cos.shape, 1)
    first = (lane % IDX_DIM) < (IDX_DIM // 2)

    def rope64(x):
        partner = jnp.where(first, pltpu.roll(x, 128 - IDX_DIM // 2, 1), pltpu.roll(x, IDX_DIM // 2, 1))
        return x * cos + partner * sin

    for p in range(IDX_HEADS // 2):
        r = rope64(acc[:, 128 * p:128 * (p + 1)])
        iq_ref[2 * p] = r[:, :IDX_DIM].astype(bf16)
        iq_ref[2 * p + 1] = r[:, IDX_DIM:].astype(bf16)
    slab = acc[:, IDX_HEADS * IDX_DIM:]
    r = rope64(slab)[:, :IDX_DIM]
    ik_ref[...] = r
    ikb_ref[...] = r.astype(bf16)
    iw_ref[...] = slab[:, IDX_DIM:IDX_DIM + IDX_HEADS] * (IDX_HEADS ** -0.5)


def idx_projection(h, w_idx, cos, sin):
    G, T, K = h.shape
    tm = min(T, 512)
    return pl.pallas_call(
        _idxproj_kernel,
        grid=(G, T // tm),
        in_specs=[pl.BlockSpec((None, tm, K), lambda g, t: (g, t, 0)),
                  pl.BlockSpec((K, IDX_W_COLS), lambda g, t: (0, 0)),
                  pl.BlockSpec((tm, 128), lambda g, t: (t, 0)),
                  pl.BlockSpec((tm, 128), lambda g, t: (t, 0))],
        out_specs=[pl.BlockSpec((None, IDX_HEADS, tm, IDX_DIM), lambda g, t: (g, 0, t, 0)),
                   pl.BlockSpec((None, tm, IDX_DIM), lambda g, t: (g, t, 0)),
                   pl.BlockSpec((None, tm, IDX_DIM), lambda g, t: (g, t, 0)),
                   pl.BlockSpec((None, tm, IDX_HEADS), lambda g, t: (g, t, 0))],
        out_shape=[jax.ShapeDtypeStruct((G, IDX_HEADS, T, IDX_DIM), bf16),
                   jax.ShapeDtypeStruct((G, T, IDX_DIM), f32),
                   jax.ShapeDtypeStruct((G, T, IDX_DIM), bf16),
                   jax.ShapeDtypeStruct((G, T, IDX_HEADS), f32)],
        compiler_params=_params("parallel", "parallel"), name="idx_proj",
    )(h, w_idx, cos, sin)


def _sortable(x):
    bits = pltpu.bitcast(jnp.where(x == 0.0, 0.0, x), i32)
    return bits ^ ((bits >> 31) & 0x7FFFFFFF)


def _kth_largest_key(count_ge, shape, k):
    t0 = jnp.where(count_ge(jnp.zeros(shape, i32)) >= k, 0, INT_MIN).astype(i32)

    def body(i, t):
        cand = t + jnp.left_shift(jnp.int32(1), 30 - i)
        return jnp.where(count_ge(cand) >= k, cand, t)

    return lax.fori_loop(0, 31, body, t0)


def _tie_cutoff(count_tied_below, shape, need, nbits):
    def body(i, j):
        cand = j + jnp.left_shift(jnp.int32(1), nbits - 1 - i)
        return jnp.where(count_tied_below(cand) < need, cand, j)

    return lax.fori_loop(0, nbits, body, jnp.zeros(shape, i32))


TQ = 256
TK = 512


def _attn_prompt_kernel(q_ref, k_ref, v_ref, iq_ref, iw_ref, ik_ref, o_ref, keys_ref, m_ref, acc_ref, *, topk):
    qi = pl.program_id(1)
    nkc = (qi * TQ + TQ - 1) // TK + 1
    iw = iw_ref[...]
    qpos = lax.broadcasted_iota(i32, (TQ, TK), 0) + qi * TQ
    lane = lax.broadcasted_iota(i32, (TQ, TK), 1)

    def score_chunk(c, carry):
        ikc = ik_ref[pl.ds(pl.multiple_of(c * TK, TK), TK), :]
        sc = jnp.zeros((TQ, TK), f32)
        for h in range(IDX_HEADS):
            s = lax.dot_general(iq_ref[h], ikc, NT_DIMS, preferred_element_type=f32)
            sc = sc + jnp.maximum(s, 0.0) * iw[:, h:h + 1]
        sc = jnp.where(lane + c * TK <= qpos, sc, NEG)
        keys_ref[c] = _sortable(sc)
        return carry

    lax.fori_loop(0, nkc, score_chunk, 0)

    def fold(x):
        return sum(x[:, 128 * j:128 * (j + 1)] for j in range(1, TK // 128)) + x[:, 0:128]

    def count(pred):
        def body(c, acc):
            return acc + fold(jnp.where(pred(c, keys_ref[c]), 1, 0).astype(i32))
        acc = lax.fori_loop(0, nkc, body, jnp.zeros((TQ, 128), i32))
        return jnp.sum(acc, axis=1, keepdims=True)

    thr = _kth_largest_key(lambda cand: count(lambda c, key: key >= cand), (TQ, 1), topk)

    n_ge = count(lambda c, key: key >= thr)

    @pl.when(jnp.max(jnp.where(n_ge != topk, 1, 0)) > 0)
    def _():
        n_gt = count(lambda c, key: key > thr)
        need = topk - n_gt
        cut = _tie_cutoff(lambda j: count(lambda c, key: (key == thr) & (lane + c * TK < j)), (TQ, 1), need, 12)

        def demote(c, carry):
            key = keys_ref[c]
            keys_ref[c] = jnp.where((key == thr) & (lane + c * TK > cut), thr - 1, key)
            return carry

        lax.fori_loop(0, nkc, demote, 0)

    m_ref[...] = jnp.full(m_ref.shape, MASKED, f32)
    acc_ref[...] = jnp.zeros(acc_ref.shape, f32)
    ones = jnp.ones((TK, HEAD_DIM), bf16)

    def attn_chunk(c, carry):
        off = pl.multiple_of(c * TK, TK)
        sel = (keys_ref[c] >= thr) & (lane + c * TK <= qpos)
        for g in range(KV_HEADS):
            kc = k_ref[pl.ds(off, TK), HEAD_DIM * g:HEAD_DIM * (g + 1)]
            vc = jnp.concatenate([v_ref[pl.ds(off, TK), HEAD_DIM * g:HEAD_DIM * (g + 1)], ones], axis=1)
            qg = q_ref[Q_PER_KV * g:Q_PER_KV * (g + 1)].reshape(Q_PER_KV * TQ, HEAD_DIM)
            s_all = lax.dot_general(qg, kc, NT_DIMS, preferred_element_type=f32)
            ps, alphas = [], []
            for r in range(Q_PER_KV):
                hd = Q_PER_KV * g + r
                s = jnp.where(sel, s_all[TQ * r:TQ * (r + 1)], MASKED)
                m_old = m_ref[hd]
                m_new = jnp.maximum(m_old, jnp.max(s, axis=-1, keepdims=True))
                m_ref[hd] = m_new
                alphas.append(jnp.exp2(m_old - m_new))
                ps.append(jnp.exp2(s - jnp.concatenate([m_new] * (TK // 128), axis=1)).astype(bf16))
            pv = jnp.dot(jnp.concatenate(ps, axis=0), vc, preferred_element_type=f32)
            for r in range(Q_PER_KV):
                hd = Q_PER_KV * g + r
                acc_ref[hd] = jnp.concatenate([alphas[r]] * 2, axis=1) * acc_ref[hd] + pv[TQ * r:TQ * (r + 1)]
        return carry

    lax.fori_loop(0, nkc, attn_chunk, 0)

    for hd in range(N_HEADS):
        a = acc_ref[hd]
        o_ref[:, HEAD_DIM * hd:HEAD_DIM * (hd + 1)] = (a[:, :HEAD_DIM] / a[:, HEAD_DIM:]).astype(bf16)


def attention_prompt(q, kb, vb, iq, iw, ikb):
    G, _, T, _ = q.shape
    topk = min(TOPK_MAX, T // 4)
    return pl.pallas_call(
        functools.partial(_attn_prompt_kernel, topk=topk),
        grid=(G, T // TQ),
        in_specs=[pl.BlockSpec((None, N_HEADS, TQ, HEAD_DIM), lambda g, t: (g, 0, t, 0)),
                  pl.BlockSpec((None, T, KV_COLS), lambda g, t: (g, 0, 0)),
                  pl.BlockSpec((None, T, KV_COLS), lambda g, t: (g, 0, 0)),
                  pl.BlockSpec((None, IDX_HEADS, TQ, IDX_DIM), lambda g, t: (g, 0, t, 0)),
                  pl.BlockSpec((None, TQ, IDX_HEADS), lambda g, t: (g, t, 0)),
                  pl.BlockSpec((None, T, IDX_DIM), lambda g, t: (g, 0, 0))],
        out_specs=pl.BlockSpec((None, TQ, N_HEADS * HEAD_DIM), lambda g, t: (g, t, 0)),
        out_shape=jax.ShapeDtypeStruct((G, T, N_HEADS * HEAD_DIM), bf16),
        scratch_shapes=[pltpu.VMEM((T // TK, TQ, TK), i32),
                        pltpu.VMEM((N_HEADS, TQ, 128), f32),
                        pltpu.VMEM((N_HEADS, TQ, 2 * HEAD_DIM), f32)],
        compiler_params=_params("parallel", "arbitrary"), name="attn_prompt",
    )(q, kb, vb, iq, iw, ikb)


SCORE_PAGES = 32
PAGES_PER_STEP = 32
PAGE_ROWS = PAGE_SIZE * KV_HEADS


def _sample_score_kernel(pt_ref, iq_ref, iw_ref, ikn_ref, *refs):
    page_refs = refs[:SCORE_PAGES]
    sc_ref, snew_ref = refs[SCORE_PAGES:]
    iq = iq_ref[...]
    iw = iw_ref[...]
    iqb = iq.astype(bf16)
    for p in range(SCORE_PAGES):
        s = jnp.dot(iqb, page_refs[p][...].astype(bf16), preferred_element_type=f32)
        sc_ref[p:p + 1, :] = jnp.sum(jnp.maximum(s, 0.0) * iw, axis=0, keepdims=True)
    s_new = jnp.sum(iq * ikn_ref[...], axis=1, keepdims=True)
    s_new = jnp.sum(jnp.maximum(s_new, 0.0) * iw, axis=0, keepdims=True)
    snew_ref[...] = jnp.broadcast_to(s_new, snew_ref.shape)


def sample_scores(page_table, iq_s, iw_s, ik_new, cache_ik):
    B, n_pages = page_table.shape

    def page_spec(p):
        return pl.BlockSpec((None, IDX_DIM, PAGE_SIZE), lambda b, j, pt: (pt[b, j * SCORE_PAGES + p], 0, 0))

    grid_spec = pltpu.PrefetchScalarGridSpec(
        num_scalar_prefetch=1,
        grid=(B, n_pages // SCORE_PAGES),
        in_specs=[pl.BlockSpec((None, IDX_HEADS, IDX_DIM), lambda b, j, pt: (b, 0, 0)),
                  pl.BlockSpec((None, IDX_HEADS, 1), lambda b, j, pt: (b, 0, 0)),
                  pl.BlockSpec((None, 1, IDX_DIM), lambda b, j, pt: (b, 0, 0))]
                 + [page_spec(p) for p in range(SCORE_PAGES)],
        out_specs=[pl.BlockSpec((None, SCORE_PAGES, PAGE_SIZE), lambda b, j, pt: (b, j, 0)),
                   pl.BlockSpec((None, 1, PAGE_SIZE), lambda b, j, pt: (b, 0, 0))],
    )
    return pl.pallas_call(
        _sample_score_kernel, grid_spec=grid_spec,
        out_shape=[jax.ShapeDtypeStruct((B, n_pages, PAGE_SIZE), f32), jax.ShapeDtypeStruct((B, 1, PAGE_SIZE), f32)],
        compiler_params=_params("parallel", "arbitrary"), name="sample_scores",
    )(page_table, iq_s, iw_s, ik_new, *([cache_ik] * SCORE_PAGES))


def _sample_select_kernel(sc_ref, snew_ref, sel_ref, seln_ref, *, topk):
    B, past = sc_ref.shape
    keys = _sortable(sc_ref[...])
    key_new = _sortable(snew_ref[:, 0:1])
    idx = lax.broadcasted_iota(i32, keys.shape, 1)
    idx_new = jnp.full((B, 1), past, i32)

    def count(pred):
        new = jnp.where(pred(key_new, idx_new), 1, 0).astype(i32)
        return jnp.sum(jnp.where(pred(keys, idx), 1, 0).astype(i32), axis=1, keepdims=True) + new

    thr = _kth_largest_key(lambda cand: count(lambda k, i: k >= cand), (B, 1), topk)
    need = topk - count(lambda k, i: k > thr)
    cut = _tie_cutoff(lambda jj: count(lambda k, i: (k == thr) & (i < jj)), (B, 1), need, int(past).bit_length())
    chosen = lambda k, i: (k > thr) | ((k == thr) & (i <= cut))
    seln_ref[...] = jnp.broadcast_to(jnp.where(chosen(key_new, idx_new), 0.0, MASKED), seln_ref.shape)
    spread = (lax.broadcasted_iota(i32, (PAGE_SIZE, PAGE_ROWS), 1) // KV_HEADS
              == lax.broadcasted_iota(i32, (PAGE_SIZE, PAGE_ROWS), 0))
    spread = jnp.where(spread, 1.0, 0.0).astype(bf16)
    flags = jnp.where(chosen(keys, idx), 1.0, 0.0).astype(bf16)
    for pg in range(past // PAGE_SIZE):
        rep = jnp.dot(flags[:, PAGE_SIZE * pg:PAGE_SIZE * (pg + 1)], spread, preferred_element_type=f32)
        sel_ref[:, PAGE_ROWS * pg:PAGE_ROWS * (pg + 1)] = jnp.where(rep > 0.5, 0.0, MASKED)


def sample_select(scores, score_new):
    B, n_pages, _ = scores.shape
    past = n_pages * PAGE_SIZE
    topk = min(TOPK_MAX, (past + 1) // 4)
    sel, sel_new = pl.pallas_call(
        functools.partial(_sample_select_kernel, topk=topk),
        out_shape=[jax.ShapeDtypeStruct((B, n_pages * PAGE_ROWS), f32), jax.ShapeDtypeStruct((B, PAGE_SIZE), f32)],
        compiler_params=pltpu.CompilerParams(vmem_limit_bytes=VMEM_LIMIT), name="sample_select",
    )(scores.reshape(B, past), score_new.reshape(B, PAGE_SIZE))
    return sel.reshape(B, n_pages, PAGE_ROWS), sel_new.reshape(B, 1, PAGE_SIZE)


def _sample_attn_kernel(pt_ref, q_ref, kn_ref, vn_ref, sel_ref, seln_ref, *refs):
    k_refs = refs[:PAGES_PER_STEP]
    v_refs = refs[PAGES_PER_STEP:2 * PAGES_PER_STEP]
    o_ref, m_ref, l_ref, acc_ref = refs[2 * PAGES_PER_STEP:]
    j = pl.program_id(1)
    qb = q_ref[...]
    head_kv = lax.broadcasted_iota(i32, (N_HEADS, PAGE_ROWS), 0) // Q_PER_KV
    row_kv = lax.broadcasted_iota(i32, (N_HEADS, PAGE_ROWS), 1) % KV_HEADS
    own_rows = jnp.where(head_kv == row_kv, 0.0, MASKED)

    @pl.when(j == 0)
    def _():
        head_kv_col = lax.broadcasted_iota(i32, (N_HEADS, HEAD_DIM), 0) // Q_PER_KV
        kn = jnp.zeros((N_HEADS, HEAD_DIM), f32)
        vn = jnp.zeros((N_HEADS, HEAD_DIM), f32)
        for g in range(KV_HEADS):
            cols = slice(HEAD_DIM * g, HEAD_DIM * (g + 1))
            kn = jnp.where(head_kv_col == g, kn_ref[:, cols], kn)
            vn = jnp.where(head_kv_col == g, vn_ref[:, cols], vn)
        m_ref[...] = jnp.sum(qb.astype(f32) * kn, axis=1, keepdims=True) + seln_ref[:, 0:1]
        l_ref[...] = jnp.ones(l_ref.shape, f32)
        acc_ref[...] = vn

    s = jnp.concatenate(
        [lax.dot_general(qb, k_refs[p][...].astype(bf16), NT_DIMS, preferred_element_type=f32)
         + sel_ref[p:p + 1, :] + own_rows for p in range(PAGES_PER_STEP)], axis=1)
    m_old = m_ref[...]
    m_new = jnp.maximum(m_old, jnp.max(s, axis=1, keepdims=True))
    p_ = jnp.exp2(s - m_new)
    alpha = jnp.exp2(m_old - m_new)
    l_ref[...] = alpha * l_ref[...] + jnp.sum(p_, axis=1, keepdims=True)
    v = jnp.concatenate([v_refs[p][...].astype(bf16) for p in range(PAGES_PER_STEP)], axis=0)
    acc_ref[...] = alpha * acc_ref[...] + jnp.dot(p_.astype(bf16), v, preferred_element_type=f32)
    m_ref[...] = m_new

    @pl.when(j == pl.num_programs(1) - 1)
    def _():
        o_ref[...] = (acc_ref[...] / l_ref[...]).astype(bf16)


def sample_attention(page_table, q_s, k_new, v_new, sel, sel_new, cache_k, cache_v):
    B, n_pages = page_table.shape
    steps = n_pages // PAGES_PER_STEP

    def page_spec(p):
        return pl.BlockSpec((None, PAGE_ROWS, HEAD_DIM), lambda b, j, pt: (pt[b, j * PAGES_PER_STEP + p], 0, 0))

    grid_spec = pltpu.PrefetchScalarGridSpec(
        num_scalar_prefetch=1,
        grid=(B, steps),
        in_specs=[pl.BlockSpec((None, N_HEADS, HEAD_DIM), lambda b, j, pt: (b, 0, 0)),
                  pl.BlockSpec((None, 1, KV_COLS), lambda b, j, pt: (b, 0, 0)),
                  pl.BlockSpec((None, 1, KV_COLS), lambda b, j, pt: (b, 0, 0)),
                  pl.BlockSpec((None, PAGES_PER_STEP, PAGE_ROWS), lambda b, j, pt: (b, j, 0)),
                  pl.BlockSpec((None, 1, PAGE_SIZE), lambda b, j, pt: (b, 0, 0))]
                 + [page_spec(p) for p in range(PAGES_PER_STEP)] * 2,
        out_specs=pl.BlockSpec((None, N_HEADS, HEAD_DIM), lambda b, j, pt: (b, 0, 0)),
        scratch_shapes=[pltpu.VMEM((N_HEADS, 1), f32), pltpu.VMEM((N_HEADS, 1), f32),
                        pltpu.VMEM((N_HEADS, HEAD_DIM), f32)],
    )
    return pl.pallas_call(
        _sample_attn_kernel, grid_spec=grid_spec,
        out_shape=jax.ShapeDtypeStruct((B, N_HEADS, HEAD_DIM), bf16),
        compiler_params=_params("parallel", "arbitrary"), name="sample_attn",
    )(page_table, q_s, k_new, v_new, sel, sel_new, *([cache_k] * PAGES_PER_STEP), *([cache_v] * PAGES_PER_STEP))


def _glu_kernel(x_ref, wa_ref, wg_ref, ba_ref, bg_ref, o_ref):
    x = x_ref[...]
    a = jnp.dot(x, wa_ref[...].astype(bf16), preferred_element_type=f32) + ba_ref[...]
    g = jnp.dot(x, wg_ref[...].astype(bf16), preferred_element_type=f32) + bg_ref[...]
    o_ref[...] = a * jax.nn.sigmoid(g)


def glu_projection(h, w_pw1, b_pw1):
    G, T, K = h.shape
    C = w_pw1.shape[1] // 2
    tm, tn = min(T, 1024), 512
    nj = C // tn
    b2 = b_pw1.reshape(1, 2 * C)
    return pl.pallas_call(
        _glu_kernel,
        grid=(G, T // tm, nj),
        in_specs=[pl.BlockSpec((None, tm, K), lambda g, t, j: (g, t, 0)),
                  pl.BlockSpec((K, tn), lambda g, t, j: (0, j)),
                  pl.BlockSpec((K, tn), lambda g, t, j: (0, nj + j)),
                  pl.BlockSpec((1, tn), lambda g, t, j: (0, j)),
                  pl.BlockSpec((1, tn), lambda g, t, j: (0, nj + j))],
        out_specs=pl.BlockSpec((None, tm, tn), lambda g, t, j: (g, t, j)),
        out_shape=jax.ShapeDtypeStruct((G, T, C), f32),
        compiler_params=_params("parallel", "parallel", "parallel"), name="glu_proj",
    )(h, w_pw1, w_pw1, b2, b2)


def _ln_silu(y, g, b):
    mu = jnp.mean(y, axis=-1, keepdims=True)
    var = jnp.mean(jnp.square(y - mu), axis=-1, keepdims=True)
    z = (y - mu) * lax.rsqrt(var + LN_EPS) * g + b
    return z * jax.nn.sigmoid(z)


CONV_HALO = 32


CONV_COLS = 256


def _conv_prompt_kernel(u_ref, halo_ref, w_ref, b_ref, g_ref, be_ref, o_ref, ext_ref, sh_ref, y_ref):
    t = pl.program_id(1)
    tt, C = u_ref.shape
    halo = halo_ref[...]
    ext_ref[0:CONV_HALO, :] = jnp.where(t == 0, jnp.zeros_like(halo), halo)
    ext_ref[CONV_HALO:, :] = u_ref[...]
    base = CONV_HALO - (CONV_WIDTH - 1)
    y_ref[...] = jnp.zeros(y_ref.shape, f32) + b_ref[...]
    for shift in range(8):
        taps = [(a, 8 * a + shift - base) for a in range(CONV_HALO // 8 + 1)
                if 0 <= 8 * a + shift - base < CONV_WIDTH]
        rows = 8 * taps[-1][0] + tt
        sh_ref[0:rows, :] = ext_ref[shift:shift + rows, :]
        for c in range(C // CONV_COLS):
            cols = slice(CONV_COLS * c, CONV_COLS * (c + 1))
            acc = y_ref[:, cols]
            for a, k in taps:
                acc = acc + sh_ref[8 * a:8 * a + tt, cols] * w_ref[k:k + 1, cols]
            y_ref[:, cols] = acc
    o_ref[...] = _ln_silu(y_ref[...], g_ref[...], be_ref[...]).astype(bf16)


def conv_prompt(u, w_dw, b_dw, ln_g, ln_b):
    G, T, C = u.shape
    tt = 128
    hb = tt // CONV_HALO
    row = pl.BlockSpec((1, C), lambda g, t: (0, 0))
    return pl.pallas_call(
        _conv_prompt_kernel,
        grid=(G, T // tt),
        in_specs=[pl.BlockSpec((None, tt, C), lambda g, t: (g, t, 0)),
                  pl.BlockSpec((None, CONV_HALO, C), lambda g, t: (g, jnp.maximum(t * hb - 1, 0), 0)),
                  pl.BlockSpec((CONV_WIDTH, C), lambda g, t: (0, 0)), row, row, row],
        out_specs=pl.BlockSpec((None, tt, C), lambda g, t: (g, t, 0)),
        out_shape=jax.ShapeDtypeStruct((G, T, C), bf16),
        scratch_shapes=[pltpu.VMEM((CONV_HALO + tt, C), f32), pltpu.VMEM((CONV_HALO + tt, C), f32),
                        pltpu.VMEM((tt, C), f32)],
        compiler_params=_params("parallel", "parallel"), name="conv_prompt",
    )(u, u, w_dw, b_dw.reshape(1, C), ln_g.reshape(1, C), ln_b.reshape(1, C))


def _conv_sample_kernel(u_ref, hist_ref, w_ref, b_ref, g_ref, be_ref, o_ref):
    w = w_ref[...]
    y = jnp.sum(hist_ref[...] * w[None, :CONV_WIDTH - 1, :], axis=1) + u_ref[...] * w[CONV_WIDTH - 1:, :] + b_ref[...]
    o_ref[...] = _ln_silu(y, g_ref[...], be_ref[...]).astype(bf16)


def conv_sample(u, hist, w_dw, b_dw, ln_g, ln_b):
    B, C = u.shape
    tc = 512
    col = lambda r: pl.BlockSpec((r, tc), lambda j: (0, j))
    return pl.pallas_call(
        _conv_sample_kernel,
        grid=(C // tc,),
        in_specs=[col(B), pl.BlockSpec((B, CONV_WIDTH - 1, tc), lambda j: (0, 0, j)), col(CONV_WIDTH),
                  col(1), col(1), col(1)],
        out_specs=col(B),
        out_shape=jax.ShapeDtypeStruct((B, C), bf16),
        compiler_params=_params("arbitrary"), name="conv_sample",
    )(u, hist, w_dw, b_dw.reshape(1, C), ln_g.reshape(1, C), ln_b.reshape(1, C))


def _top_values(x, n):
    vals = []
    for _ in range(n):
        m = jnp.max(x, axis=0, keepdims=True)
        vals.append(m)
        x = jnp.where(x == m, -jnp.inf, x)
    return vals


NOT_TOP = float(PEER_TOPK)


def _top_ranked(x, n):
    vals = []
    rank = jnp.full(x.shape, NOT_TOP, f32)
    for r in range(n):
        m = jnp.max(x, axis=0, keepdims=True)
        hit = x == m
        vals.append(m)
        rank = jnp.where(hit, float(r), rank)
        x = jnp.where(hit, -jnp.inf, x)
    return vals, rank


def _peer_gate_kernel(q_ref, sk_ref, cnt_ref, ea_ref, rb_ref, eb_ref):
    for h in range(PEER_HEADS):
        st = []
        for p in range(2):
            col = (2 * h + p) * PEER_NKEYS
            st.append(lax.dot_general(sk_ref[h, p].astype(bf16), q_ref[:, col:col + PEER_NKEYS], NT_DIMS,
                                      preferred_element_type=f32))
        a, b = st
        va, rank_a = _top_ranked(a, PEER_TOPK)
        vbl, rank_b = _top_ranked(b, PEER_TOPK)
        row = lax.broadcasted_iota(i32, (PEER_TOPK, a.shape[1]), 0)
        vb = jnp.zeros((PEER_TOPK, a.shape[1]), f32)
        for r in range(PEER_TOPK):
            vb = jnp.where(row == r, vbl[r], vb)
        cands = [va[r] + vb for r in range(PEER_TOPK)]
        row8 = lax.broadcasted_iota(i32, (8, a.shape[1]), 0)
        va_tail = jnp.zeros((8, a.shape[1]), f32)
        for r in range(8):
            va_tail = jnp.where(row8 == r, va[8 + r], va_tail)
        search = [cands[0]] + [cands[r][:8] for r in range(1, 8)] + [va_tail + vbl[0]]
        thr = _top_values(jnp.concatenate(search, axis=0), PEER_TOPK)[-1]
        top = va[0] + vbl[0]
        z = jnp.zeros_like(top)
        cnt = jnp.zeros(a.shape, f32)
        for r in range(PEER_TOPK):
            keep = cands[r] >= thr
            z = z + jnp.sum(jnp.where(keep, jnp.exp(cands[r] - top), 0.0), axis=0, keepdims=True)
            kept = jnp.sum(jnp.where(keep, 1.0, 0.0), axis=0, keepdims=True)
            cnt = jnp.where(rank_a == float(r), kept, cnt)
        ea = jnp.exp(a - va[0]) / z
        for grp in range(PEER_NKEYS // 8):
            cnt_ref[grp, h] = cnt[8 * grp:8 * (grp + 1), :]
            ea_ref[grp, h] = ea[8 * grp:8 * (grp + 1), :]
        eb = jnp.exp(b - vbl[0])
        for s in range(PEER_NKEYS // BF16_ROWS):
            rows = slice(BF16_ROWS * s, BF16_ROWS * (s + 1))
            rb_ref[h, s] = rank_b[rows, :].astype(bf16)
            eb_ref[h, s] = eb[rows, :].astype(bf16)


def peer_gates(q, sub_keys):
    G, T, _ = q.shape
    tm = min(T, 256)
    jt = PEER_NKEYS // BF16_ROWS
    a_spec = pl.BlockSpec((None, PEER_NKEYS // 8, PEER_HEADS, 8, tm), lambda g, t: (g, 0, 0, 0, t))
    b_spec = pl.BlockSpec((None, PEER_HEADS, jt, BF16_ROWS, tm), lambda g, t: (g, 0, 0, 0, t))
    a_shape = jax.ShapeDtypeStruct((G, PEER_NKEYS // 8, PEER_HEADS, 8, T), f32)
    b_shape = jax.ShapeDtypeStruct((G, PEER_HEADS, jt, BF16_ROWS, T), bf16)
    return pl.pallas_call(
        _peer_gate_kernel,
        grid=(G, T // tm),
        in_specs=[pl.BlockSpec((None, tm, q.shape[2]), lambda g, t: (g, t, 0)),
                  pl.BlockSpec(sub_keys.shape, lambda g, t: (0, 0, 0, 0))],
        out_specs=[a_spec, a_spec, b_spec, b_spec],
        out_shape=[a_shape, a_shape, b_shape, b_shape],
        compiler_params=_params("parallel", "parallel"), name="peer_gates",
    )(q, sub_keys)


PEER_ROWS = 8
PEER_TE = PEER_ROWS * PEER_NKEYS
PEER_OUT_COLS = 512
PEER_PASS = 4


def _gelu(x):
    return 0.5 * x * (1.0 + lax.erf(x * (2.0 ** -0.5)))


def _peer_expert_kernel(x_ref, u_ref, v_ref, cnt_ref, ea_ref, rb_ref, eb_ref, o_ref):
    @pl.when(pl.program_id(2) == 0)
    def _():
        o_ref[...] = jnp.zeros(o_ref.shape, f32)

    tm = x_ref.shape[0]
    jt = PEER_NKEYS // BF16_ROWS
    hid = lax.dot_general(x_ref[...], u_ref[...], NT_DIMS, preferred_element_type=f32)
    act = _gelu(hid)
    gates = [[None] * (tm // 128) for _ in range(PEER_ROWS)]
    for c in range(tm // 128):
        tok = slice(128 * c, 128 * (c + 1))
        for r0 in range(0, PEER_ROWS, PEER_PASS):
            gt = [jnp.zeros((jt, BF16_ROWS, 128), bf16) for _ in range(PEER_PASS)]
            for h in range(PEER_HEADS):
                rb = rb_ref[h, :, :, tok]
                eb = eb_ref[h, :, :, tok]
                for k in range(PEER_PASS):
                    r = r0 + k
                    cnt = jnp.broadcast_to(cnt_ref[h, r:r + 1, tok], (BF16_ROWS, 128)).astype(bf16)
                    eai = jnp.broadcast_to(ea_ref[h, r:r + 1, tok], (BF16_ROWS, 128)).astype(bf16)
                    active = jnp.minimum(jnp.maximum(cnt[None] - rb, 0.0), 1.0)
                    gt[k] = gt[k] + active * (eai[None] * eb)
            for k in range(PEER_PASS):
                gates[r0 + k][c] = gt[k].astype(f32).reshape(PEER_NKEYS, 128).T
    blocks = [jnp.concatenate(gates[r], axis=0) * act[:, PEER_NKEYS * r:PEER_NKEYS * (r + 1)]
              for r in range(PEER_ROWS)]
    w = jnp.concatenate(blocks, axis=1).astype(bf16)
    for j in range(v_ref.shape[1] // PEER_OUT_COLS):
        cols = slice(PEER_OUT_COLS * j, PEER_OUT_COLS * (j + 1))
        o_ref[:, cols] += jnp.dot(w, v_ref[:, cols], preferred_element_type=f32)


def peer_experts(x, gates, u_tab, v_tab, layer):
    G, T, Dm = x.shape
    cnt, ea, rb, eb = gates
    tm = min(T, 512)
    a_spec = pl.BlockSpec((None, None, PEER_HEADS, 8, tm), lambda g, t, e: (g, e, 0, 0, t))
    b_spec = pl.BlockSpec((None, PEER_HEADS, PEER_NKEYS // BF16_ROWS, BF16_ROWS, tm), lambda g, t, e: (g, 0, 0, 0, t))
    tab_spec =pl.BlockSpec((None, PEER_TE, Dm), lambda g, t, e: (layer, e, 0))
    return pl.pallas_call(
        _peer_expert_kernel,
        grid=(G, T // tm, u_tab.shape[1] // PEER_TE),
        in_specs=[pl.BlockSpec((None, tm, Dm), lambda g, t, e: (g, t, 0)), tab_spec, tab_spec,
                  a_spec, a_spec, b_spec, b_spec],
        out_specs=pl.BlockSpec((None, tm, Dm), lambda g, t, e: (g, t, 0)),
        out_shape=jax.ShapeDtypeStruct((G, T, Dm), f32),
        compiler_params=_params("parallel", "parallel", "arbitrary"), name="peer_experts",
    )(x, u_tab, v_tab, cnt, ea, rb, eb)


def peer(h, w_q, sub_keys, u_tab, v_tab, layer):
    G, T, Dm = h.shape
    Tp = -(-T // 128) * 128
    if Tp != T:
        h = jnp.pad(h, ((0, 0), (0, Tp - T), (0, 0)))
    q = token_matmul(h, w_q, bf16)
    out = peer_experts(h, peer_gates(q, sub_keys), u_tab, v_tab, layer)
    return out[:, :T] if Tp != T else out


def _trunk(y, sample, mod, norm_g, attend, conv, weights):
    (w_in, w_idx, q_gain, k_gain, w_out, rope128, rope64, w_pw1, b_pw1, w_pw2, peer_w_q, peer_sub_keys, peer_u,
     peer_v) = weights
    nm = functools.partial(normmod, mod=mod, sample=sample)
    h = nm(y, norm_g=norm_g[0, 0], shift=(0, 0), scale=(0, 1))
    q = q_projection(h, w_in, q_gain, *rope128)
    k, kb, v, vb = kv_projection(h, w_in, k_gain, *rope128)
    iq, ik, ikb, iw = idx_projection(h, w_idx, *rope64)
    o = attend(q, k, kb, v, vb, iq, ik, ikb, iw)
    d = token_matmul(o, w_out, f32)
    y, h = nm(y, delta=d, gate=(0, 2), norm_g=norm_g[0, 1], shift=(0, 3), scale=(0, 4))
    d = peer(h, peer_w_q[0], peer_sub_keys[0], peer_u, peer_v, 0)
    y, h = nm(y, delta=d, gate=(0, 5), norm_g=norm_g[1, 0], shift=(1, 0), scale=(1, 1))
    u = glu_projection(h, w_pw1, b_pw1)
    c, state = conv(u)
    d = token_matmul(c, w_pw2, f32)
    y, h = nm(y, delta=d, gate=(1, 2), norm_g=norm_g[1, 1], shift=(1, 3), scale=(1, 4))
    d = peer(h, peer_w_q[1], peer_sub_keys[1], peer_u, peer_v, 1)
    y = nm(y, delta=d, gate=(1, 5))
    return y, k, v, ik, state


def kernel(x_prompt, x_sample, cache_k, cache_v, cache_idx_k, state_conv, page_table, c_prompt, c_sample, w_ada, b_ada, norm_g, attn_w_in, attn_q_gain, attn_k_gain, attn_w_out, conv_w_pw1, conv_b_pw1, conv_w_dw, conv_b_dw, conv_ln_g, conv_ln_b, conv_w_pw2, peer_w_q, peer_sub_keys, peer_u, peer_v):
    B, T, Dm = x_prompt.shape
    Bd = x_sample.shape[0]
    n_pages = page_table.shape[1]
    past = n_pages * PAGE_SIZE
    n_phys = cache_k.shape[1]

    c_all = jnp.concatenate([c_sample, c_prompt, jnp.zeros((ADA_ROWS - Bd - B, Dm), f32)], axis=0)
    mod = ada_modulation(c_all, w_ada, b_ada)
    ng = norm_g.reshape(norm_g.shape[0], 2, 1, Dm)

    peer_u_bf, peer_v_bf = peer_u.astype(bf16), peer_v.astype(bf16)
    w_in = attn_w_in[0]
    idx0 = N_HEADS * HEAD_DIM + 2 * KV_COLS
    w_idx = jnp.concatenate([w_in[:, idx0:], jnp.zeros((Dm, IDX_W_COLS - (w_in.shape[1] - idx0)), f32)], axis=1)

    def weights(pos):
        return (w_in, w_idx, attn_q_gain[0], attn_k_gain[0], attn_w_out[0], _rope_tables(pos, HEAD_DIM),
                _rope_tables(pos, IDX_DIM), conv_w_pw1[0], conv_b_pw1[0], conv_w_pw2[0], peer_w_q, peer_sub_keys,
                peer_u_bf, peer_v_bf)

    conv_tail = (conv_w_dw[0], conv_b_dw[0], conv_ln_g[0], conv_ln_b[0])

    def attend_prompt(q, k, kb, v, vb, iq, ik, ikb, iw):
        return attention_prompt(q, kb, vb, iq, iw, ikb)

    def conv_p(u):
        return conv_prompt(u, *conv_tail), u[:, T - (CONV_WIDTH - 1):]

    def attend_sample(q, k, kb, v, vb, iq, ik, ikb, iw):
        iq_s = jnp.transpose(iq[0], (1, 0, 2)).astype(f32)
        scores, score_new = sample_scores(page_table, iq_s, iw.reshape(Bd, IDX_HEADS, 1),
                                          ik.reshape(Bd, 1, IDX_DIM), jnp.swapaxes(cache_idx_k[0], 1, 2))
        sel, sel_new = sample_select(scores, score_new)
        o = sample_attention(page_table, jnp.transpose(q[0], (1, 0, 2)), k.reshape(Bd, 1, KV_COLS),
                             v.reshape(Bd, 1, KV_COLS), sel, sel_new,
                             cache_k[0].reshape(n_phys, PAGE_ROWS, HEAD_DIM), cache_v[0].reshape(n_phys, PAGE_ROWS, HEAD_DIM))
        return o.reshape(1, Bd, N_HEADS * HEAD_DIM)

    def conv_s(u):
        hist = state_conv[0]
        c = conv_sample(u[0], hist, *conv_tail)
        return c[None], jnp.concatenate([hist[:, 1:], u[0][:, None, :]], axis=1)

    yp, kp, vp, ikp, stp = _trunk(x_prompt, False, mod, ng, attend_prompt, conv_p,
                                  weights(jnp.arange(T, dtype=i32)))
    ys, ks, vs, iks, sts = _trunk(x_sample.reshape(1, Bd, Dm), True, mod, ng, attend_sample, conv_s,
                                  weights(jnp.full((Bd,), past, i32)))
    return (yp, ys.reshape(Bd, 1, Dm),
            kp.reshape(1, B, T, KV_HEADS, HEAD_DIM), vp.reshape(1, B, T, KV_HEADS, HEAD_DIM),
            ikp.reshape(1, B, T, IDX_DIM),
            ks.reshape(1, Bd, 1, KV_HEADS, HEAD_DIM), vs.reshape(1, Bd, 1, KV_HEADS, HEAD_DIM),
            iks.reshape(1, Bd, 1, IDX_DIM),
            stp[None], sts[None])
```
